```python
import jax, jax.numpy as jnp
from jax import lax
import numpy as np

D_MODEL = 1024
BATCH = 8
SEQ = 2048
DEPTH = 1
DEC_BATCH = 128
DEC_SEQ = 8
PAST_LEN = 16384
PAGE_SIZE = 128

N_META = 16
SSD_INNER = 2 * D_MODEL
SSD_HEAD_DIM = 64
SSD_HEADS = SSD_INNER // SSD_HEAD_DIM
SSD_GROUPS = 4
SSD_HPG = SSD_HEADS // SSD_GROUPS
SSD_STATE = 128
SSD_CONV = 4
SSD_CONV_DIM = SSD_INNER + 2 * SSD_GROUPS * SSD_STATE
SSD_CHUNK = 128
SC_WIDTH = D_MODEL
SC_CONV = 3
N_EGROUPS = 4
EXPERTS_PER_GROUP = 8
N_EXPERTS = N_EGROUPS * EXPERTS_PER_GROUP
TOP_K = 2
D_EXPERT = D_MODEL // 4
EPS = 1e-6
COL_SIZES = (SSD_INNER, SSD_CONV_DIM, SSD_HEADS, SC_WIDTH, SC_WIDTH, SC_WIDTH, D_MODEL, D_MODEL)
IN_COLS = sum(COL_SIZES)

kernel_name = 'hybrid_ssd_shortconv_hiermoe_step'


def _rmsnorm(x, w):
    xf = x.astype(jnp.float32)
    xf = xf * lax.rsqrt(jnp.mean(xf * xf, axis=-1, keepdims=True) + EPS)
    return (xf * w.astype(jnp.float32)).astype(x.dtype)


def _causal_dwconv(u, hist, w):
    K = w.shape[0]
    L = u.shape[1]
    full = jnp.concatenate([hist.astype(u.dtype), u], axis=1)
    out = full[:, 0:L] * w[0]
    for k in range(1, K):
        out = out + full[:, k:k + L] * w[k]
    return out, full[:, L:]


def _ssd_chunked(x, dt, a_neg, Bm, Cm, h0, chunk):
    f32 = jnp.float32
    b, L = x.shape[0], x.shape[1]
    nc = L // chunk
    G, R, P, N = SSD_GROUPS, SSD_HPG, SSD_HEAD_DIM, SSD_STATE
    x = x.astype(f32).reshape(b, nc, chunk, G, R, P)
    dt = dt.astype(f32).reshape(b, nc, chunk, G, R)
    Bm = Bm.astype(f32).reshape(b, nc, chunk, G, N)
    Cm = Cm.astype(f32).reshape(b, nc, chunk, G, N)
    acum = jnp.cumsum(dt * a_neg.reshape(G, R), axis=2)
    causal = jnp.tril(jnp.ones((chunk, chunk), dtype=bool))
    seg = acum[:, :, :, None] - acum[:, :, None, :]
    decay = jnp.exp(jnp.where(causal[:, :, None, None], seg, -jnp.inf))
    cb = jnp.einsum('bclgn,bcsgn->bclsg', Cm, Bm)
    wts = cb[..., None] * decay * dt[:, :, None]
    y_diag = jnp.einsum('bclsgr,bcsgrp->bclgrp', wts, x)
    to_end = jnp.exp(acum[:, :, -1:] - acum) * dt
    states = jnp.einsum('bclgn,bclgr,bclgrp->bcgrpn', Bm, to_end, x)
    chunk_decay = jnp.exp(acum[:, :, -1])

    def step(h, inp):
        s_c, d_c = inp
        return d_c[..., None, None] * h + s_c, h

    h_last, h_prev = lax.scan(step, h0.astype(f32).reshape(b, G, R, P, N),
                              (jnp.moveaxis(states, 1, 0), jnp.moveaxis(chunk_decay, 1, 0)))
    h_prev = jnp.moveaxis(h_prev, 0, 1)
    y_off = jnp.einsum('bclgn,bcgrpn,bclgr->bclgrp', Cm, h_prev, jnp.exp(acum))
    y = (y_diag + y_off).reshape(b, L, SSD_HEADS, P)
    return y, h_last.reshape(b, SSD_HEADS, P, N)


def _token_mixers(h, ssm0, conv_hist, sc_hist, segments, w_in, conv_w, conv_b, dt_bias, A_log,
                  D_skip, norm_w, w_ssd_out, sc_conv_w, w_sc_out, w_o):
    f32 = jnp.float32
    b, L, _ = h.shape
    idx = [int(i) for i in np.cumsum(COL_SIZES)[:-1]]
    z, xbc, dt_raw, sc_b, sc_c, sc_h, g_ssd, g_sc = jnp.split(h @ w_in, idx, axis=-1)
    xbc, conv_hist_new = _causal_dwconv(xbc, conv_hist, conv_w)
    xbc = jax.nn.silu(xbc + conv_b)
    xs, Bm, Cm = jnp.split(xbc, [SSD_INNER, SSD_INNER + SSD_GROUPS * SSD_STATE], axis=-1)
    xs = xs.reshape(b, L, SSD_HEADS, SSD_HEAD_DIM)
    Bm = Bm.reshape(b, L, SSD_GROUPS, SSD_STATE)
    Cm = Cm.reshape(b, L, SSD_GROUPS, SSD_STATE)
    dt = jax.nn.softplus((dt_raw + dt_bias).astype(f32))
    a_neg = -jnp.exp(A_log.astype(f32))
    hstate = ssm0
    ys = []
    start = 0
    for length, chunk in segments:
        sl = slice(start, start + length)
        y_seg, hstate = _ssd_chunked(xs[:, sl], dt[:, sl], a_neg, Bm[:, sl], Cm[:, sl], hstate, chunk)
        ys.append(y_seg)
        start += length
    y = jnp.concatenate(ys, axis=1) + xs.astype(f32) * D_skip.astype(f32)[:, None]
    u = y.reshape(b, L, SSD_INNER) * jax.nn.silu(z.astype(f32))
    ug = u.reshape(b, L, SSD_GROUPS, SSD_INNER // SSD_GROUPS)
    ug = ug * lax.rsqrt(jnp.mean(ug * ug, axis=-1, keepdims=True) + EPS)
    u = (ug.reshape(b, L, SSD_INNER) * norm_w.astype(f32)).astype(h.dtype)
    y_ssd = u @ w_ssd_out
    cout, sc_hist_new = _causal_dwconv(sc_c * sc_h, sc_hist, sc_conv_w)
    y_sc = (sc_b * cout) @ w_sc_out
    merged = jax.nn.sigmoid(g_ssd) * y_ssd + jax.nn.sigmoid(g_sc) * y_sc
    return merged @ w_o, hstate.astype(h.dtype), conv_hist_new, sc_hist_new


def _hier_moe(h, w_rg, b_rg, w_re, b_re, w_gate, w_up, w_down):
    f32 = jnp.float32
    shp = h.shape
    t = h.reshape(-1, D_MODEL)
    p_grp = jax.nn.softmax((t @ w_rg + b_rg).astype(f32), axis=-1)
    g_sel = jnp.argmax(p_grp, axis=-1)
    g_prob = jnp.take_along_axis(p_grp, g_sel[:, None], axis=-1)
    le = (t @ w_re + b_re).astype(f32).reshape(-1, N_EGROUPS, EXPERTS_PER_GROUP)
    le = jnp.take_along_axis(le, g_sel[:, None, None], axis=1)[:, 0]
    p_exp = jax.nn.softmax(le, axis=-1)
    top_v, top_i = lax.top_k(p_exp, TOP_K)
    top_v = top_v / jnp.sum(top_v, axis=-1, keepdims=True)
    w_local = jnp.einsum('tk,tke->te', top_v, jax.nn.one_hot(top_i, EXPERTS_PER_GROUP, dtype=f32)) * g_prob
    gate = (jax.nn.one_hot(g_sel, N_EGROUPS, dtype=f32)[:, :, None] * w_local[:, None, :]).astype(h.dtype)
    out = jnp.zeros_like(t)
    for g in range(N_EGROUPS):
        a = jnp.einsum('td,edf->tef', t, w_gate[g])
        up = jnp.einsum('td,edf->tef', t, w_up[g])
        act = jax.nn.silu(a) * up * gate[:, g, :, None]
        out = out + jnp.einsum('tef,efd->td', act, w_down[g])
    return out.reshape(shp)


def setup_inputs(seed: int = 0) -> dict:
    key = jax.random.key(seed)
    ks = jax.random.split(key, 32)
    nrm = jax.random.normal
    f32 = jnp.float32
    dt0 = jnp.exp(jax.random.uniform(ks[10], (DEPTH, SSD_HEADS), f32) * (np.log(0.1) - np.log(0.001)) + np.log(0.001))
    dt_bias = dt0 + jnp.log(-jnp.expm1(-dt0))
    return {
        'x_prompt': nrm(ks[0], (BATCH, SEQ, D_MODEL), f32),
        'x_sample': nrm(ks[1], (DEC_BATCH, DEC_SEQ, D_MODEL), f32),
        'state_ssm': 0.1 * nrm(ks[2], (DEPTH, DEC_BATCH, SSD_HEADS, SSD_HEAD_DIM, SSD_STATE), f32),
        'state_ssd_conv': nrm(ks[3], (DEPTH, DEC_BATCH, SSD_CONV - 1, SSD_CONV_DIM), f32),
        'state_short_conv': nrm(ks[4], (DEPTH, DEC_BATCH, SC_CONV - 1, SC_WIDTH), f32),
        'meta_tokens': nrm(ks[5], (N_META, D_MODEL), f32),
        'ln1_w': 1.0 + 0.02 * nrm(ks[6], (DEPTH, D_MODEL), f32),
        'w_in': nrm(ks[7], (DEPTH, D_MODEL, IN_COLS), f32) * D_MODEL ** -0.5,
        'ssd_conv_w': nrm(ks[8], (DEPTH, SSD_CONV, SSD_CONV_DIM), f32) * SSD_CONV ** -0.5,
        'ssd_conv_b': 0.01 * nrm(ks[9], (DEPTH, SSD_CONV_DIM), f32),
        'ssd_dt_bias': dt_bias,
        'ssd_A_log': jnp.log(jax.random.uniform(ks[11], (DEPTH, SSD_HEADS), f32, 1.0, 16.0)),
        'ssd_D': 1.0 + 0.1 * nrm(ks[12], (DEPTH, SSD_HEADS), f32),
        'ssd_norm_w': 1.0 + 0.02 * nrm(ks[13], (DEPTH, SSD_INNER), f32),
        'w_ssd_out': nrm(ks[14], (DEPTH, SSD_INNER, D_MODEL), f32) * SSD_INNER ** -0.5,
        'sc_conv_w': nrm(ks[15], (DEPTH, SC_CONV, SC_WIDTH), f32) * SC_CONV ** -0.5,
        'w_sc_out': nrm(ks[16], (DEPTH, SC_WIDTH, D_MODEL), f32) * SC_WIDTH ** -0.5,
        'w_o': nrm(ks[17], (DEPTH, D_MODEL, D_MODEL), f32) * D_MODEL ** -0.5,
        'ln2_w': 1.0 + 0.02 * nrm(ks[18], (DEPTH, D_MODEL), f32),
        'w_rg': nrm(ks[19], (DEPTH, D_MODEL, N_EGROUPS), f32) * D_MODEL ** -0.5,
        'b_rg': 0.01 * nrm(ks[20], (DEPTH, N_EGROUPS), f32),
        'w_re': nrm(ks[21], (DEPTH, D_MODEL, N_EXPERTS), f32) * D_MODEL ** -0.5,
        'b_re': 0.01 * nrm(ks[22], (DEPTH, N_EXPERTS), f32),
        'w_gate': nrm(ks[23], (DEPTH, N_EGROUPS, EXPERTS_PER_GROUP, D_MODEL, D_EXPERT), f32) * D_MODEL ** -0.5,
        'w_up': nrm(ks[24], (DEPTH, N_EGROUPS, EXPERTS_PER_GROUP, D_MODEL, D_EXPERT), f32) * D_MODEL ** -0.5,
        'w_down': nrm(ks[25], (DEPTH, N_EGROUPS, EXPERTS_PER_GROUP, D_EXPERT, D_MODEL), f32) * D_EXPERT ** -0.5,
        'final_norm_w': 1.0 + 0.02 * nrm(ks[26], (D_MODEL,), f32),
    }


def reference(x_prompt, x_sample, state_ssm, state_ssd_conv, state_short_conv, meta_tokens, ln1_w, w_in,
              ssd_conv_w, ssd_conv_b, ssd_dt_bias, ssd_A_log, ssd_D, ssd_norm_w, w_ssd_out, sc_conv_w,
              w_sc_out, w_o, ln2_w, w_rg, b_rg, w_re, b_re, w_gate, w_up, w_down, final_norm_w):
    dtype = x_prompt.dtype
    bp, sp = x_prompt.shape[0], x_prompt.shape[1]
    bs, ss = x_sample.shape[0], x_sample.shape[1]
    xp = jnp.concatenate([jnp.broadcast_to(meta_tokens.astype(dtype)[None], (bp, N_META, D_MODEL)), x_prompt], axis=1)
    xs = x_sample
    seg_prompt = ((N_META, N_META), (sp, SSD_CHUNK))
    seg_sample = ((ss, ss),)
    ssm_p, cv_p, sc_p, ssm_s, cv_s, sc_s = [], [], [], [], [], []
    for l in range(DEPTH):
        mix_w = (w_in[l], ssd_conv_w[l], ssd_conv_b[l], ssd_dt_bias[l], ssd_A_log[l], ssd_D[l],
                 ssd_norm_w[l], w_ssd_out[l], sc_conv_w[l], w_sc_out[l], w_o[l])
        moe_w = (w_rg[l], b_rg[l], w_re[l], b_re[l], w_gate[l], w_up[l], w_down[l])
        m, h1, c1, s1 = _token_mixers(_rmsnorm(xp, ln1_w[l]),
                                      jnp.zeros((bp, SSD_HEADS, SSD_HEAD_DIM, SSD_STATE), jnp.float32),
                                      jnp.zeros((bp, SSD_CONV - 1, SSD_CONV_DIM), dtype),
                                      jnp.zeros((bp, SC_CONV - 1, SC_WIDTH), dtype),
                                      seg_prompt, *mix_w)
        xp = xp + m
        xp = xp + _hier_moe(_rmsnorm(xp, ln2_w[l]), *moe_w)
        ssm_p.append(h1)
        cv_p.append(c1)
        sc_p.append(s1)
        m, h2, c2, s2 = _token_mixers(_rmsnorm(xs, ln1_w[l]), state_ssm[l], state_ssd_conv[l],
                                      state_short_conv[l], seg_sample, *mix_w)
        xs = xs + m
        xs = xs + _hier_moe(_rmsnorm(xs, ln2_w[l]), *moe_w)
        ssm_s.append(h2)
        cv_s.append(c2)
        sc_s.append(s2)
    y_prompt = _rmsnorm(xp, final_norm_w)[:, N_META:]
    y_sample = _rmsnorm(xs, final_norm_w)
    return (y_prompt, y_sample, jnp.stack(ssm_p, 0), jnp.stack(cv_p, 0), jnp.stack(sc_p, 0),
            jnp.stack(ssm_s, 0), jnp.stack(cv_s, 0), jnp.stack(sc_s, 0))
```

```python
import functools

import jax
import jax.numpy as jnp
from jax import lax
from jax.experimental import pallas as pl
from jax.experimental.pallas import tpu as pltpu

F32 = jnp.float32
BF16 = jnp.bfloat16

D_MODEL = 1024
N_META = 16
SSD_INNER = 2048
SSD_HEAD_DIM = 64
SSD_HEADS = 32
SSD_GROUPS = 4
SSD_HPG = 8
SSD_STATE = 128
SSD_CONV = 4
SSD_CONV_DIM = 3072
SC_WIDTH = 1024
SC_CONV = 3
N_EGROUPS = 4
EXPERTS_PER_GROUP = 8
N_EXPERTS = 32
D_EXPERT = 256
EPS = 1e-6

LANES = 128
SUBLANES = 8
GROUP_COLS = SSD_HPG * SSD_HEAD_DIM
A_COLS = SSD_INNER + SSD_CONV_DIM
S_COLS = 3 * SC_WIDTH
G_COLS = 2 * D_MODEL
NEG_BIG = -1e30
VMEM_LIMIT = 56 * 1024 * 1024


def _nt_dot(a, b):
    return lax.dot_general(a, b, (((1,), (1,)), ((), ())), preferred_element_type=F32)


def _dot01(m01_bf16, x):
    hi = x.astype(BF16)
    r1 = x - hi.astype(F32)
    mid = r1.astype(BF16)
    lo = (r1 - mid.astype(F32)).astype(BF16)
    out = jnp.dot(m01_bf16, hi, preferred_element_type=F32)
    out = out + jnp.dot(m01_bf16, mid, preferred_element_type=F32)
    return out + jnp.dot(m01_bf16, lo, preferred_element_type=F32)


def _softplus(x):
    return jnp.maximum(x, 0.0) + jnp.log1p(jnp.exp(-jnp.abs(x)))


def _sigmoid(x):
    return 1.0 / (1.0 + jnp.exp(-x))


def _inproj_kernel(nA, nS, x_ref, lnw_ref, w_ref, wdt_ref, oa_ref, os_ref, og_ref, odt_ref, h_s):
    j = pl.program_id(1)

    @pl.when(j == 0)
    def _():
        x = x_ref[...]
        ms = jnp.mean(x * x, axis=-1, keepdims=True)
        h = (x * lax.rsqrt(ms + EPS) * lnw_ref[...]).astype(BF16)
        h_s[...] = h
        odt_ref[...] = jnp.dot(h, wdt_ref[...], preferred_element_type=F32)

    r = jnp.dot(h_s[...], w_ref[...], preferred_element_type=F32).astype(BF16)

    @pl.when(j < nA)
    def _():
        oa_ref[...] = r

    @pl.when(jnp.logical_and(j >= nA, j < nA + nS))
    def _():
        os_ref[...] = r

    @pl.when(j >= nA + nS)
    def _():
        og_ref[...] = r


def _inproj(x, lnw, wcat, wdt, tm, tn=1024):
    t = x.shape[0]
    nA, nS, nG = A_COLS // tn, S_COLS // tn, G_COLS // tn
    nj = nA + nS + nG
    return pl.pallas_call(
        functools.partial(_inproj_kernel, nA, nS),
        grid=(t // tm, nj),
        in_specs=[
            pl.BlockSpec((tm, D_MODEL), lambda i, j: (i, 0)),
            pl.BlockSpec((1, D_MODEL), lambda i, j: (0, 0)),
            pl.BlockSpec((D_MODEL, tn), lambda i, j: (0, j)),
            pl.BlockSpec((D_MODEL, LANES), lambda i, j: (0, 0)),
        ],
        out_specs=[
            pl.BlockSpec((tm, tn), lambda i, j: (i, jnp.minimum(j, nA - 1))),
            pl.BlockSpec((tm, tn), lambda i, j: (i, jnp.clip(j - nA, 0, nS - 1))),
            pl.BlockSpec((tm, tn), lambda i, j: (i, jnp.clip(j - nA - nS, 0, nG - 1))),
            pl.BlockSpec((tm, LANES), lambda i, j: (i, 0)),
        ],
        out_shape=[
            jax.ShapeDtypeStruct((t, A_COLS), BF16),
            jax.ShapeDtypeStruct((t, S_COLS), BF16),
            jax.ShapeDtypeStruct((t, G_COLS), BF16),
            jax.ShapeDtypeStruct((t, LANES), F32),
        ],
        scratch_shapes=[pltpu.VMEM((tm, D_MODEL), BF16)],
        compiler_params=pltpu.CompilerParams(
            dimension_semantics=("arbitrary", "arbitrary"), vmem_limit_bytes=VMEM_LIMIT),
        name="inproj",
    )(x, lnw, wcat, wdt)


def _seqmix_kernel(Q, NS, pad_rows,
                   a_ref, s_ref, dt_ref, hist_ref, schist_ref, st0_ref,
                   cw_ref, cb_ref, scw_ref, dtb_ref, alog_ref, drow_ref, nw_ref,
                   u_ref, v_ref, st_ref, cs_ref, scs_ref,
                   xpad, scpad, xbc_s, y_s, xst_s):
    L = Q // NS
    lg = L.bit_length() - 1
    c = pl.program_id(1)

    @pl.when(c == 0)
    def _():
        xpad[:, 0:SUBLANES, :] = hist_ref[...]
        scpad[:, 0:SUBLANES, :] = schist_ref[...]
        st_ref[...] = st0_ref[...]

    CW = 512
    for cc in range(0, SSD_CONV_DIM, CW):
        raw_all = a_ref[:, SSD_INNER + cc:SSD_INNER + cc + CW].astype(F32)
        w3 = cw_ref[3:4, cc:cc + CW]
        bias = cb_ref[:, cc:cc + CW]
        for s in range(NS):
            raw = raw_all[s * L:(s + 1) * L]
            xpad[s, SUBLANES:SUBLANES + L, cc:cc + CW] = raw
            acc = raw * w3 + bias
            for k in range(SSD_CONV - 1):
                acc = acc + xpad[s, 5 + k:5 + k + L, cc:cc + CW] * cw_ref[k:k + 1, cc:cc + CW]
            xbc_s[s * L:(s + 1) * L, cc:cc + CW] = acc * _sigmoid(acc)
    for s in range(NS):
        cs_ref[s] = xpad[s, L + 5:L + 8, :]
        xpad[s, 0:SUBLANES, :] = xpad[s, L:L + SUBLANES, :]

    for cc in range(0, SC_WIDTH, CW):
        scb = s_ref[:, cc:cc + CW].astype(F32)
        ch_all = (s_ref[:, SC_WIDTH + cc:SC_WIDTH + cc + CW].astype(F32)
                  * s_ref[:, 2 * SC_WIDTH + cc:2 * SC_WIDTH + cc + CW].astype(F32))
        for s in range(NS):
            ch = ch_all[s * L:(s + 1) * L]
            scpad[s, SUBLANES:SUBLANES + L, cc:cc + CW] = ch
            acc = ch * scw_ref[2:3, cc:cc + CW]
            for k in range(SC_CONV - 1):
                acc = acc + scpad[s, 6 + k:6 + k + L, cc:cc + CW] * scw_ref[k:k + 1, cc:cc + CW]
            v_ref[s * L:(s + 1) * L, cc:cc + CW] = (scb[s * L:(s + 1) * L] * acc).astype(BF16)
    for s in range(NS):
        scs_ref[s] = scpad[s, L + 6:L + 8, :]
        scpad[s, 0:SUBLANES, :] = scpad[s, L:L + SUBLANES, :]

    def padrows(x):
        if Q == LANES:
            return x
        return jnp.concatenate([x, jnp.zeros((LANES - Q, x.shape[1]), x.dtype)], axis=0)

    li = lax.broadcasted_iota(jnp.int32, (Q, Q), 0)
    si = lax.broadcasted_iota(jnp.int32, (Q, Q), 1)
    same = (li >> lg) == (si >> lg)
    causal = jnp.logical_and(same, si <= li)
    tril01 = jnp.where(causal, 1.0, 0.0).astype(BF16)
    same01 = jnp.where(same, 1.0, 0.0).astype(BF16)

    dt = _softplus(dt_ref[...] + dtb_ref[...])
    if pad_rows:
        ri = lax.broadcasted_iota(jnp.int32, (Q, LANES), 0)
        dt = jnp.where(ri >= pad_rows, dt, 0.0)
    da = dt * (-jnp.exp(alog_ref[...]))
    acum = _dot01(tril01, da)
    tot = _dot01(same01, da)
    acum_t = padrows(acum).T
    tot_t = padrows(tot).T
    dt_t = padrows(dt).T
    w_t = jnp.exp(tot_t - acum_t) * dt_t

    rowseq = lax.broadcasted_iota(jnp.int32, (Q, GROUP_COLS), 0) >> lg
    rowseq_p = lax.broadcasted_iota(jnp.int32, (LANES, LANES), 0) >> lg

    for g in range(SSD_GROUPS):
        b_g = xbc_s[:, SSD_INNER + g * SSD_STATE:SSD_INNER + (g + 1) * SSD_STATE]
        c_g = xbc_s[:, SSD_INNER + GROUP_COLS + g * SSD_STATE:SSD_INNER + GROUP_COLS + (g + 1) * SSD_STATE]
        b_gb = b_g.astype(BF16)
        c_gb = c_g.astype(BF16)
        cbm = _nt_dot(c_gb, b_gb)
        yo = None
        for s in range(NS):
            h_s = st_ref[s, g * GROUP_COLS:(g + 1) * GROUP_COLS, :].astype(BF16)
            yo_s = _nt_dot(c_gb, h_s)
            yo = yo_s if yo is None else jnp.where(rowseq == s, yo_s, yo)
        for r in range(SSD_HPG):
            h = g * SSD_HPG + r
            cols = slice(h * SSD_HEAD_DIM, (h + 1) * SSD_HEAD_DIM)
            colb = jnp.broadcast_to(acum[:, h:h + 1], (Q, Q))
            rowb = jnp.broadcast_to(acum_t[h:h + 1, 0:Q], (Q, Q))
            dec = jnp.exp(jnp.where(causal, colb - rowb, NEG_BIG))
            wts = (cbm * dec * jnp.broadcast_to(dt_t[h:h + 1, 0:Q], (Q, Q))).astype(BF16)
            x_h = xbc_s[:, cols]
            yd = jnp.dot(wts, x_h.astype(BF16), preferred_element_type=F32)
            ecol = jnp.exp(jnp.broadcast_to(acum[:, h:h + 1], (Q, SSD_HEAD_DIM)))
            y_s[:, cols] = yd + yo[:, r * SSD_HEAD_DIM:(r + 1) * SSD_HEAD_DIM] * ecol + drow_ref[:, cols] * x_h

    for jb in range(SSD_INNER // LANES):
        xst_s[jb * LANES:(jb + 1) * LANES, :] = padrows(xbc_s[:, jb * LANES:(jb + 1) * LANES]).T
    for s in range(NS):
        da_b = jnp.exp(jnp.broadcast_to(tot_t[:, s * L:s * L + 1], (LANES, LANES)))
        for g in range(SSD_GROUPS):
            b_p = padrows(xbc_s[:, SSD_INNER + g * SSD_STATE:SSD_INNER + (g + 1) * SSD_STATE])
            if NS > 1:
                b_p = jnp.where(rowseq_p == s, b_p, 0.0)
            pieces = []
            for r in range(SSD_HPG):
                h = g * SSD_HPG + r
                pieces.append(xst_s[h * SSD_HEAD_DIM:(h + 1) * SSD_HEAD_DIM, :] * w_t[h:h + 1, :])
            xw_t = jnp.concatenate(pieces, axis=0).astype(BF16)
            upd = jnp.dot(xw_t, b_p.astype(BF16), preferred_element_type=F32)
            for r in range(SSD_HPG):
                h = g * SSD_HPG + r
                rows = slice(h * SSD_HEAD_DIM, (h + 1) * SSD_HEAD_DIM)
                dec_h = jnp.broadcast_to(da_b[h:h + 1, :], (SSD_HEAD_DIM, SSD_STATE))
                st_ref[s, rows, :] = dec_h * st_ref[s, rows, :] + upd[r * SSD_HEAD_DIM:(r + 1) * SSD_HEAD_DIM, :]

    for g in range(SSD_GROUPS):
        cols = slice(g * GROUP_COLS, (g + 1) * GROUP_COLS)
        z = a_ref[:, cols].astype(F32)
        ug = y_s[:, cols] * (z * _sigmoid(z))
        ms = jnp.mean(ug * ug, axis=-1, keepdims=True)
        u_ref[:, cols] = (ug * lax.rsqrt(ms + EPS) * nw_ref[:, cols]).astype(BF16)


def _seqmix(pa, ps, pdt, hist, schist, st0, prm, *, nb, nc, q, ns, pad_rows, init_bcast):
    L = q // ns
    cw, cb, scw, dtb, alog, drow, nw = prm

    def tok(b, c):
        return (b * nc + c, 0)

    def init3(b, c):
        return (0 if init_bcast else b, 0, 0)

    def const2(b, c):
        return (0, 0)

    def per_b(b, c):
        return (b, 0, 0)

    n_seq = nb * ns
    return pl.pallas_call(
        functools.partial(_seqmix_kernel, q, ns, pad_rows),
        grid=(nb, nc),
        in_specs=[
            pl.BlockSpec((q, A_COLS), tok),
            pl.BlockSpec((q, S_COLS), tok),
            pl.BlockSpec((q, LANES), tok),
            pl.BlockSpec((ns, SUBLANES, SSD_CONV_DIM), init3),
            pl.BlockSpec((ns, SUBLANES, SC_WIDTH), init3),
            pl.BlockSpec((ns, SSD_INNER, SSD_STATE), init3),
            pl.BlockSpec((SSD_CONV, SSD_CONV_DIM), const2),
            pl.BlockSpec((1, SSD_CONV_DIM), const2),
            pl.BlockSpec((SC_CONV, SC_WIDTH), const2),
            pl.BlockSpec((1, LANES), const2),
            pl.BlockSpec((1, LANES), const2),
            pl.BlockSpec((1, SSD_INNER), const2),
            pl.BlockSpec((1, SSD_INNER), const2),
        ],
        out_specs=[
            pl.BlockSpec((q, SSD_INNER), tok),
            pl.BlockSpec((q, SC_WIDTH), tok),
            pl.BlockSpec((ns, SSD_INNER, SSD_STATE), per_b),
            pl.BlockSpec((ns, SSD_CONV - 1, SSD_CONV_DIM), per_b),
            pl.BlockSpec((ns, SC_CONV - 1, SC_WIDTH), per_b),
        ],
        out_shape=[
            jax.ShapeDtypeStruct((nb * nc * q, SSD_INNER), BF16),
            jax.ShapeDtypeStruct((nb * nc * q, SC_WIDTH), BF16),
            jax.ShapeDtypeStruct((n_seq, SSD_INNER, SSD_STATE), F32),
            jax.ShapeDtypeStruct((n_seq, SSD_CONV - 1, SSD_CONV_DIM), F32),
            jax.ShapeDtypeStruct((n_seq, SC_CONV - 1, SC_WIDTH), F32),
        ],
        scratch_shapes=[
            pltpu.VMEM((ns, SUBLANES + L, SSD_CONV_DIM), F32),
            pltpu.VMEM((ns, SUBLANES + L, SC_WIDTH), F32),
            pltpu.VMEM((q, SSD_CONV_DIM), F32),
            pltpu.VMEM((q, SSD_INNER), F32),
            pltpu.VMEM((SSD_INNER, LANES), F32),
        ],
        compiler_params=pltpu.CompilerParams(
            dimension_semantics=("arbitrary", "arbitrary"), vmem_limit_bytes=VMEM_LIMIT),
        name="seqmix",
    )(pa, ps, pdt, hist, schist, st0, cw, cb, scw, dtb, alog, drow, nw)


def _merge_kernel(u_ref, v_ref, g_ref, x_ref, wa_ref, ws_ref, wo_ref, ln2_ref, wr_ref, br_ref,
                  x1_ref, h2_ref, gate_ref):
    y_ssd = jnp.dot(u_ref[...], wa_ref[...], preferred_element_type=F32)
    y_sc = jnp.dot(v_ref[...], ws_ref[...], preferred_element_type=F32)
    g1 = g_ref[:, 0:D_MODEL].astype(F32)
    g2 = g_ref[:, D_MODEL:2 * D_MODEL].astype(F32)
    merged = (_sigmoid(g1) * y_ssd + _sigmoid(g2) * y_sc).astype(BF16)
    x1 = x_ref[...] + jnp.dot(merged, wo_ref[...], preferred_element_type=F32)
    x1_ref[...] = x1
    ms = jnp.mean(x1 * x1, axis=-1, keepdims=True)
    h2 = (x1 * lax.rsqrt(ms + EPS) * ln2_ref[...]).astype(BF16)
    h2_ref[...] = h2

    logits = jnp.dot(h2, wr_ref[...], preferred_element_type=F32) + br_ref[...]
    lane = lax.broadcasted_iota(jnp.int32, logits.shape, 1)
    big = jnp.int32(1 << 20)
    gl = jnp.where(lane < N_EGROUPS, logits, NEG_BIG)
    gmax = jnp.max(gl, axis=-1, keepdims=True)
    g_sel = jnp.min(jnp.where(gl == gmax, lane, big), axis=-1, keepdims=True)
    gsum = jnp.sum(jnp.exp(gl - gmax), axis=-1, keepdims=True)
    g_prob = 1.0 / gsum
    lo = N_EGROUPS + EXPERTS_PER_GROUP * g_sel
    emask = jnp.logical_and(lane >= lo, lane < lo + EXPERTS_PER_GROUP)
    el = jnp.where(emask, logits, NEG_BIG)
    m1 = jnp.max(el, axis=-1, keepdims=True)
    e = jnp.where(emask, jnp.exp(el - m1), -1.0)
    i1 = jnp.min(jnp.where(e == 1.0, lane, big), axis=-1, keepdims=True)
    e_rest = jnp.where(lane == i1, -1.0, e)
    e2 = jnp.max(e_rest, axis=-1, keepdims=True)
    i2 = jnp.min(jnp.where(e_rest == e2, lane, big), axis=-1, keepdims=True)
    denom = 1.0 + e2
    w1 = g_prob / denom
    w2 = g_prob * e2 / denom
    gate_ref[...] = jnp.where(lane == i1, w1, 0.0) + jnp.where(lane == i2, w2, 0.0)


def _merge(u, v, g, x, wa, ws, wo, ln2, wr, br, tm):
    t = x.shape[0]
    row = lambda i: (i, 0)
    const = lambda i: (0, 0)
    return pl.pallas_call(
        _merge_kernel,
        grid=(t // tm,),
        in_specs=[
            pl.BlockSpec((tm, SSD_INNER), row),
            pl.BlockSpec((tm, SC_WIDTH), row),
            pl.BlockSpec((tm, G_COLS), row),
            pl.BlockSpec((tm, D_MODEL), row),
            pl.BlockSpec((SSD_INNER, D_MODEL), const),
            pl.BlockSpec((SC_WIDTH, D_MODEL), const),
            pl.BlockSpec((D_MODEL, D_MODEL), const),
            pl.BlockSpec((1, D_MODEL), const),
            pl.BlockSpec((D_MODEL, LANES), const),
            pl.BlockSpec((1, LANES), const),
        ],
        out_specs=[
            pl.BlockSpec((tm, D_MODEL), row),
            pl.BlockSpec((tm, D_MODEL), row),
            pl.BlockSpec((tm, LANES), row),
        ],
        out_shape=[
            jax.ShapeDtypeStruct((t, D_MODEL), F32),
            jax.ShapeDtypeStruct((t, D_MODEL), BF16),
            jax.ShapeDtypeStruct((t, LANES), F32),
        ],
        compiler_params=pltpu.CompilerParams(
            dimension_semantics=("arbitrary",), vmem_limit_bytes=VMEM_LIMIT),
        name="merge",
    )(u, v, g, x, wa, ws, wo, ln2, wr, br)


def _moe_kernel(h2_ref, gate_ref, x1_ref, wg_ref, wu_ref, wd_ref, fw_ref, y_ref, acc_s):
    e = pl.program_id(1)

    @pl.when(e == 0)
    def _():
        acc_s[...] = jnp.zeros_like(acc_s)

    h2 = h2_ref[...]
    a = jnp.dot(h2, wg_ref[...], preferred_element_type=F32)
    up = jnp.dot(h2, wu_ref[...], preferred_element_type=F32)
    lane = lax.broadcasted_iota(jnp.int32, gate_ref.shape, 1)
    ge = jnp.sum(jnp.where(lane == N_EGROUPS + e, gate_ref[...], 0.0), axis=-1, keepdims=True)
    act = (a * _sigmoid(a) * up * ge).astype(BF16)
    acc_s[...] += jnp.dot(act, wd_ref[...], preferred_element_type=F32)

    @pl.when(e == N_EXPERTS - 1)
    def _():
        x2 = x1_ref[...] + acc_s[...]
        ms = jnp.mean(x2 * x2, axis=-1, keepdims=True)
        y_ref[...] = x2 * lax.rsqrt(ms + EPS) * fw_ref[...]


def _moe(h2, gate, x1, wg, wu, wd, fw, tm):
    t = x1.shape[0]
    row = lambda i, e: (i, 0)
    return pl.pallas_call(
        _moe_kernel,
        grid=(t // tm, N_EXPERTS),
        in_specs=[
            pl.BlockSpec((tm, D_MODEL), row),
            pl.BlockSpec((tm, LANES), row),
            pl.BlockSpec((tm, D_MODEL), row),
            pl.BlockSpec((None, D_MODEL, D_EXPERT), lambda i, e: (e, 0, 0)),
            pl.BlockSpec((None, D_MODEL, D_EXPERT), lambda i, e: (e, 0, 0)),
            pl.BlockSpec((None, D_EXPERT, D_MODEL), lambda i, e: (e, 0, 0)),
            pl.BlockSpec((1, D_MODEL), lambda i, e: (0, 0)),
        ],
        out_specs=pl.BlockSpec((tm, D_MODEL), row),
        out_shape=jax.ShapeDtypeStruct((t, D_MODEL), F32),
        scratch_shapes=[pltpu.VMEM((tm, D_MODEL), F32)],
        compiler_params=pltpu.CompilerParams(
            dimension_semantics=("arbitrary", "arbitrary"), vmem_limit_bytes=VMEM_LIMIT),
        name="moe",
    )(h2, gate, x1, wg, wu, wd, fw)


def _pad_lanes(v, n=LANES):
    v = v.reshape(1, -1).astype(F32)
    return jnp.pad(v, ((0, 0), (0, n - v.shape[1])))


def _pad_hist(h):
    return jnp.pad(h, ((0, 0), (SUBLANES - h.shape[1], 0), (0, 0)))


def kernel(x_prompt, x_sample, state_ssm, state_ssd_conv, state_short_conv, meta_tokens, ln1_w, w_in,
           ssd_conv_w, ssd_conv_b, ssd_dt_bias, ssd_A_log, ssd_D, ssd_norm_w, w_ssd_out, sc_conv_w,
           w_sc_out, w_o, ln2_w, w_rg, b_rg, w_re, b_re, w_gate, w_up, w_down, final_norm_w):
    assert ln1_w.shape[0] == 1, "single-layer trunk"
    bp, sp, _ = x_prompt.shape
    bs, ss, _ = x_sample.shape
    l = 0

    a0, a1 = 0, A_COLS
    d0, d1 = A_COLS, A_COLS + SSD_HEADS
    wi = w_in[l]
    wcat = jnp.concatenate([wi[:, a0:a1], wi[:, d1:]], axis=1).astype(BF16)
    wdt = jnp.pad(wi[:, d0:d1], ((0, 0), (0, LANES - SSD_HEADS))).astype(BF16)
    lnw = ln1_w[l].reshape(1, D_MODEL)
    prm = (ssd_conv_w[l], ssd_conv_b[l].reshape(1, -1), sc_conv_w[l],
           _pad_lanes(ssd_dt_bias[l]), _pad_lanes(ssd_A_log[l]),
           jnp.repeat(ssd_D[l].astype(F32), SSD_HEAD_DIM).reshape(1, SSD_INNER),
           ssd_norm_w[l].reshape(1, SSD_INNER))
    wa = w_ssd_out[l].astype(BF16)
    ws = w_sc_out[l].astype(BF16)
    wo = w_o[l].astype(BF16)
    ln2 = ln2_w[l].reshape(1, D_MODEL)
    wr = jnp.pad(jnp.concatenate([w_rg[l], w_re[l]], axis=1),
                 ((0, 0), (0, LANES - N_EGROUPS - N_EXPERTS))).astype(BF16)
    br = _pad_lanes(jnp.concatenate([b_rg[l], b_re[l]]))
    wg = w_gate[l].reshape(N_EXPERTS, D_MODEL, D_EXPERT).astype(BF16)
    wu = w_up[l].reshape(N_EXPERTS, D_MODEL, D_EXPERT).astype(BF16)
    wd = w_down[l].reshape(N_EXPERTS, D_EXPERT, D_MODEL).astype(BF16)
    fw = final_norm_w.reshape(1, D_MODEL)

    q = LANES
    xm = jnp.pad(meta_tokens.astype(F32), ((q - N_META, 0), (0, 0)))
    ma, msc, _, mdt = _inproj(xm, lnw, wcat, wdt, tm=q)
    zeros_hist = jnp.zeros((1, SUBLANES, SSD_CONV_DIM), F32)
    zeros_sch = jnp.zeros((1, SUBLANES, SC_WIDTH), F32)
    zeros_st = jnp.zeros((1, SSD_INNER, SSD_STATE), F32)
    _, _, m_st, m_cs, m_scs = _seqmix(ma, msc, mdt, zeros_hist, zeros_sch, zeros_st, prm,
                                      nb=1, nc=1, q=q, ns=1, pad_rows=q - N_META, init_bcast=True)

    xp = x_prompt.reshape(bp * sp, D_MODEL)
    pa, psc, pg, pdt = _inproj(xp, lnw, wcat, wdt, tm=512)
    pu, pv, p_st, p_cs, p_scs = _seqmix(pa, psc, pdt, _pad_hist(m_cs), _pad_hist(m_scs), m_st, prm,
                                        nb=bp, nc=sp // q, q=q, ns=1, pad_rows=0, init_bcast=True)
    px1, ph2, pgate = _merge(pu, pv, pg, xp, wa, ws, wo, ln2, wr, br, tm=512)
    y_prompt = _moe(ph2, pgate, px1, wg, wu, wd, fw, tm=1024).reshape(bp, sp, D_MODEL)

    ns = 8
    xs = x_sample.reshape(bs * ss, D_MODEL)
    sa, ssc, sg, sdt = _inproj(xs, lnw, wcat, wdt, tm=512)
    su, sv, s_st, s_cs, s_scs = _seqmix(sa, ssc, sdt, _pad_hist(state_ssd_conv[l]),
                                        _pad_hist(state_short_conv[l]),
                                        state_ssm[l].reshape(bs, SSD_INNER, SSD_STATE), prm,
                                        nb=bs // ns, nc=1, q=ns * ss, ns=ns, pad_rows=0, init_bcast=False)
    sx1, sh2, sgate = _merge(su, sv, sg, xs, wa, ws, wo, ln2, wr, br, tm=512)
    y_sample = _moe(sh2, sgate, sx1, wg, wu, wd, fw, tm=1024).reshape(bs, ss, D_MODEL)

    hshape = (SSD_HEADS, SSD_HEAD_DIM, SSD_STATE)
    return (y_prompt, y_sample,
            p_st.reshape(1, bp, *hshape), p_cs[None], p_scs[None],
            s_st.reshape(1, bs, *hshape), s_cs[None], s_scs[None])
```

```python
import functools

import jax
import jax.numpy as jnp
from jax import lax
from jax.experimental import pallas as pl
from jax.experimental.pallas import tpu as pltpu

F32 = jnp.float32
BF16 = jnp.bfloat16

D_MODEL = 1024
N_META = 16
SSD_INNER = 2048
SSD_HEAD_DIM = 64
SSD_HEADS = 32
SSD_GROUPS = 4
SSD_HPG = 8
SSD_STATE = 128
SSD_CONV = 4
SSD_CONV_DIM = 3072
SC_WIDTH = 1024
SC_CONV = 3
N_EGROUPS = 4
EXPERTS_PER_GROUP = 8
N_EXPERTS = 32
D_EXPERT = 256
EPS = 1e-6

LANES = 128
SUBLANES = 8
GROUP_COLS = SSD_HPG * SSD_HEAD_DIM
A_COLS = SSD_INNER + SSD_CONV_DIM
S_COLS = 3 * SC_WIDTH
G_COLS = 2 * D_MODEL
XG_COLS = D_MODEL + LANES
MOE_TILE = 256
NEG_BIG = -1e30
VMEM_LIMIT = 56 * 1024 * 1024


def _nt_dot(a, b):
    return lax.dot_general(a, b, (((1,), (1,)), ((), ())), preferred_element_type=F32)


def _dot01(m01_bf16, x):
    hi = x.astype(BF16)
    r1 = x - hi.astype(F32)
    mid = r1.astype(BF16)
    lo = (r1 - mid.astype(F32)).astype(BF16)
    out = jnp.dot(m01_bf16, hi, preferred_element_type=F32)
    out = out + jnp.dot(m01_bf16, mid, preferred_element_type=F32)
    return out + jnp.dot(m01_bf16, lo, preferred_element_type=F32)


def _softplus(x):
    return jnp.maximum(x, 0.0) + jnp.log1p(jnp.exp(-jnp.abs(x)))


def _sigmoid(x):
    return 1.0 / (1.0 + jnp.exp(-x))


def _inproj_kernel(nA, nS, x_ref, lnw_ref, w_ref, wdt_ref, oa_ref, os_ref, og_ref, odt_ref, h_s):
    j = pl.program_id(1)

    @pl.when(j == 0)
    def _():
        x = x_ref[...]
        ms = jnp.mean(x * x, axis=-1, keepdims=True)
        h = (x * lax.rsqrt(ms + EPS) * lnw_ref[...]).astype(BF16)
        h_s[...] = h
        odt_ref[...] = jnp.dot(h, wdt_ref[...], preferred_element_type=F32)

    r = jnp.dot(h_s[...], w_ref[...], preferred_element_type=F32).astype(BF16)

    @pl.when(j < nA)
    def _():
        oa_ref[...] = r

    @pl.when(jnp.logical_and(j >= nA, j < nA + nS))
    def _():
        os_ref[...] = r

    @pl.when(j >= nA + nS)
    def _():
        og_ref[...] = r


def _inproj(x, lnw, wcat, wdt, tm, tn=1024):
    t = x.shape[0]
    nA, nS, nG = A_COLS // tn, S_COLS // tn, G_COLS // tn
    nj = nA + nS + nG
    return pl.pallas_call(
        functools.partial(_inproj_kernel, nA, nS),
        grid=(t // tm, nj),
        in_specs=[
            pl.BlockSpec((tm, D_MODEL), lambda i, j: (i, 0)),
            pl.BlockSpec((1, D_MODEL), lambda i, j: (0, 0)),
            pl.BlockSpec((D_MODEL, tn), lambda i, j: (0, j)),
            pl.BlockSpec((D_MODEL, LANES), lambda i, j: (0, 0)),
        ],
        out_specs=[
            pl.BlockSpec((tm, tn), lambda i, j: (i, jnp.minimum(j, nA - 1))),
            pl.BlockSpec((tm, tn), lambda i, j: (i, jnp.clip(j - nA, 0, nS - 1))),
            pl.BlockSpec((tm, tn), lambda i, j: (i, jnp.clip(j - nA - nS, 0, nG - 1))),
            pl.BlockSpec((tm, LANES), lambda i, j: (i, 0)),
        ],
        out_shape=[
            jax.ShapeDtypeStruct((t, A_COLS), BF16),
            jax.ShapeDtypeStruct((t, S_COLS), BF16),
            jax.ShapeDtypeStruct((t, G_COLS), BF16),
            jax.ShapeDtypeStruct((t, LANES), F32),
        ],
        scratch_shapes=[pltpu.VMEM((tm, D_MODEL), BF16)],
        compiler_params=pltpu.CompilerParams(
            dimension_semantics=("arbitrary", "arbitrary"), vmem_limit_bytes=VMEM_LIMIT),
        name="inproj",
    )(x, lnw, wcat, wdt)


def _seqmix_kernel(Q, NS, pad_rows,
                   a_ref, s_ref, dt_ref, hist_ref, schist_ref, st0_ref,
                   cw_ref, cb_ref, scw_ref, dtb_ref, alog_ref, drow_ref, nw_ref,
                   u_ref, v_ref, st_ref, cs_ref, scs_ref,
                   xpad, scpad, xbc_s, y_s, xst_s):
    L = Q // NS
    lg = L.bit_length() - 1
    c = pl.program_id(1)

    @pl.when(c == 0)
    def _():
        xpad[:, 0:SUBLANES, :] = hist_ref[...]
        scpad[:, 0:SUBLANES, :] = schist_ref[...]
        st_ref[...] = st0_ref[...]

    CW = 512
    for cc in range(0, SSD_CONV_DIM, CW):
        raw_all = a_ref[:, SSD_INNER + cc:SSD_INNER + cc + CW].astype(F32)
        w3 = cw_ref[3:4, cc:cc + CW]
        bias = cb_ref[:, cc:cc + CW]
        for s in range(NS):
            raw = raw_all[s * L:(s + 1) * L]
            xpad[s, SUBLANES:SUBLANES + L, cc:cc + CW] = raw
            acc = raw * w3 + bias
            for k in range(SSD_CONV - 1):
                acc = acc + xpad[s, 5 + k:5 + k + L, cc:cc + CW] * cw_ref[k:k + 1, cc:cc + CW]
            xbc_s[s * L:(s + 1) * L, cc:cc + CW] = acc * _sigmoid(acc)
    for s in range(NS):
        cs_ref[s] = xpad[s, L + 5:L + 8, :]
        xpad[s, 0:SUBLANES, :] = xpad[s, L:L + SUBLANES, :]

    for cc in range(0, SC_WIDTH, CW):
        scb = s_ref[:, cc:cc + CW].astype(F32)
        ch_all = (s_ref[:, SC_WIDTH + cc:SC_WIDTH + cc + CW].astype(F32)
                  * s_ref[:, 2 * SC_WIDTH + cc:2 * SC_WIDTH + cc + CW].astype(F32))
        for s in range(NS):
            ch = ch_all[s * L:(s + 1) * L]
            scpad[s, SUBLANES:SUBLANES + L, cc:cc + CW] = ch
            acc = ch * scw_ref[2:3, cc:cc + CW]
            for k in range(SC_CONV - 1):
                acc = acc + scpad[s, 6 + k:6 + k + L, cc:cc + CW] * scw_ref[k:k + 1, cc:cc + CW]
            v_ref[s * L:(s + 1) * L, cc:cc + CW] = (scb[s * L:(s + 1) * L] * acc).astype(BF16)
    for s in range(NS):
        scs_ref[s] = scpad[s, L + 6:L + 8, :]
        scpad[s, 0:SUBLANES, :] = scpad[s, L:L + SUBLANES, :]

    def padrows(x):
        if Q == LANES:
            return x
        return jnp.concatenate([x, jnp.zeros((LANES - Q, x.shape[1]), x.dtype)], axis=0)

    li = lax.broadcasted_iota(jnp.int32, (Q, Q), 0)
    si = lax.broadcasted_iota(jnp.int32, (Q, Q), 1)
    same = (li >> lg) == (si >> lg)
    causal = jnp.logical_and(same, si <= li)
    tril01 = jnp.where(causal, 1.0, 0.0).astype(BF16)
    same01 = jnp.where(same, 1.0, 0.0).astype(BF16)

    dt = _softplus(dt_ref[...] + dtb_ref[...])
    if pad_rows:
        ri = lax.broadcasted_iota(jnp.int32, (Q, LANES), 0)
        dt = jnp.where(ri >= pad_rows, dt, 0.0)
    da = dt * (-jnp.exp(alog_ref[...]))
    acum = _dot01(tril01, da)
    tot = _dot01(same01, da)
    acum_t = padrows(acum).T
    tot_t = padrows(tot).T
    dt_t = padrows(dt).T
    w_t = jnp.exp(tot_t - acum_t) * dt_t

    rowseq = lax.broadcasted_iota(jnp.int32, (Q, GROUP_COLS), 0) >> lg
    rowseq_p = lax.broadcasted_iota(jnp.int32, (LANES, LANES), 0) >> lg

    for g in range(SSD_GROUPS):
        b_g = xbc_s[:, SSD_INNER + g * SSD_STATE:SSD_INNER + (g + 1) * SSD_STATE]
        c_g = xbc_s[:, SSD_INNER + GROUP_COLS + g * SSD_STATE:SSD_INNER + GROUP_COLS + (g + 1) * SSD_STATE]
        b_gb = b_g.astype(BF16)
        c_gb = c_g.astype(BF16)
        cbm = _nt_dot(c_gb, b_gb)
        yo = None
        for s in range(NS):
            h_s = st_ref[s, g * GROUP_COLS:(g + 1) * GROUP_COLS, :].astype(BF16)
            yo_s = _nt_dot(c_gb, h_s)
            yo = yo_s if yo is None else jnp.where(rowseq == s, yo_s, yo)
        for r in range(SSD_HPG):
            h = g * SSD_HPG + r
            cols = slice(h * SSD_HEAD_DIM, (h + 1) * SSD_HEAD_DIM)
            colb = jnp.broadcast_to(acum[:, h:h + 1], (Q, Q))
            rowb = jnp.broadcast_to(acum_t[h:h + 1, 0:Q], (Q, Q))
            dec = jnp.exp(jnp.where(causal, colb - rowb, NEG_BIG))
            wts = (cbm * dec * jnp.broadcast_to(dt_t[h:h + 1, 0:Q], (Q, Q))).astype(BF16)
            x_h = xbc_s[:, cols]
            yd = jnp.dot(wts, x_h.astype(BF16), preferred_element_type=F32)
            ecol = jnp.exp(jnp.broadcast_to(acum[:, h:h + 1], (Q, SSD_HEAD_DIM)))
            y_s[:, cols] = yd + yo[:, r * SSD_HEAD_DIM:(r + 1) * SSD_HEAD_DIM] * ecol + drow_ref[:, cols] * x_h

    for jb in range(SSD_INNER // LANES):
        xst_s[jb * LANES:(jb + 1) * LANES, :] = padrows(xbc_s[:, jb * LANES:(jb + 1) * LANES]).T
    for s in range(NS):
        da_b = jnp.exp(jnp.broadcast_to(tot_t[:, s * L:s * L + 1], (LANES, LANES)))
        for g in range(SSD_GROUPS):
            b_p = padrows(xbc_s[:, SSD_INNER + g * SSD_STATE:SSD_INNER + (g + 1) * SSD_STATE])
            if NS > 1:
                b_p = jnp.where(rowseq_p == s, b_p, 0.0)
            pieces = []
            for r in range(SSD_HPG):
                h = g * SSD_HPG + r
                pieces.append(xst_s[h * SSD_HEAD_DIM:(h + 1) * SSD_HEAD_DIM, :] * w_t[h:h + 1, :])
            xw_t = jnp.concatenate(pieces, axis=0).astype(BF16)
            upd = jnp.dot(xw_t, b_p.astype(BF16), preferred_element_type=F32)
            for r in range(SSD_HPG):
                h = g * SSD_HPG + r
                rows = slice(h * SSD_HEAD_DIM, (h + 1) * SSD_HEAD_DIM)
                dec_h = jnp.broadcast_to(da_b[h:h + 1, :], (SSD_HEAD_DIM, SSD_STATE))
                st_ref[s, rows, :] = dec_h * st_ref[s, rows, :] + upd[r * SSD_HEAD_DIM:(r + 1) * SSD_HEAD_DIM, :]

    for g in range(SSD_GROUPS):
        cols = slice(g * GROUP_COLS, (g + 1) * GROUP_COLS)
        z = a_ref[:, cols].astype(F32)
        ug = y_s[:, cols] * (z * _sigmoid(z))
        ms = jnp.mean(ug * ug, axis=-1, keepdims=True)
        u_ref[:, cols] = (ug * lax.rsqrt(ms + EPS) * nw_ref[:, cols]).astype(BF16)


def _seqmix(pa, ps, pdt, hist, schist, st0, prm, *, nb, nc, q, ns, pad_rows, init_bcast):
    L = q // ns
    cw, cb, scw, dtb, alog, drow, nw = prm

    def tok(b, c):
        return (b * nc + c, 0)

    def init3(b, c):
        return (0 if init_bcast else b, 0, 0)

    def const2(b, c):
        return (0, 0)

    def per_b(b, c):
        return (b, 0, 0)

    n_seq = nb * ns
    return pl.pallas_call(
        functools.partial(_seqmix_kernel, q, ns, pad_rows),
        grid=(nb, nc),
        in_specs=[
            pl.BlockSpec((q, A_COLS), tok),
            pl.BlockSpec((q, S_COLS), tok),
            pl.BlockSpec((q, LANES), tok),
            pl.BlockSpec((ns, SUBLANES, SSD_CONV_DIM), init3),
            pl.BlockSpec((ns, SUBLANES, SC_WIDTH), init3),
            pl.BlockSpec((ns, SSD_INNER, SSD_STATE), init3),
            pl.BlockSpec((SSD_CONV, SSD_CONV_DIM), const2),
            pl.BlockSpec((1, SSD_CONV_DIM), const2),
            pl.BlockSpec((SC_CONV, SC_WIDTH), const2),
            pl.BlockSpec((1, LANES), const2),
            pl.BlockSpec((1, LANES), const2),
            pl.BlockSpec((1, SSD_INNER), const2),
            pl.BlockSpec((1, SSD_INNER), const2),
        ],
        out_specs=[
            pl.BlockSpec((q, SSD_INNER), tok),
            pl.BlockSpec((q, SC_WIDTH), tok),
            pl.BlockSpec((ns, SSD_INNER, SSD_STATE), per_b),
            pl.BlockSpec((ns, SSD_CONV - 1, SSD_CONV_DIM), per_b),
            pl.BlockSpec((ns, SC_CONV - 1, SC_WIDTH), per_b),
        ],
        out_shape=[
            jax.ShapeDtypeStruct((nb * nc * q, SSD_INNER), BF16),
            jax.ShapeDtypeStruct((nb * nc * q, SC_WIDTH), BF16),
            jax.ShapeDtypeStruct((n_seq, SSD_INNER, SSD_STATE), F32),
            jax.ShapeDtypeStruct((n_seq, SSD_CONV - 1, SSD_CONV_DIM), F32),
            jax.ShapeDtypeStruct((n_seq, SC_CONV - 1, SC_WIDTH), F32),
        ],
        scratch_shapes=[
            pltpu.VMEM((ns, SUBLANES + L, SSD_CONV_DIM), F32),
            pltpu.VMEM((ns, SUBLANES + L, SC_WIDTH), F32),
            pltpu.VMEM((q, SSD_CONV_DIM), F32),
            pltpu.VMEM((q, SSD_INNER), F32),
            pltpu.VMEM((SSD_INNER, LANES), F32),
        ],
        compiler_params=pltpu.CompilerParams(
            dimension_semantics=("arbitrary", "arbitrary"), vmem_limit_bytes=VMEM_LIMIT),
        name="seqmix",
    )(pa, ps, pdt, hist, schist, st0, cw, cb, scw, dtb, alog, drow, nw)


def _merge_kernel(u_ref, v_ref, g_ref, x_ref, wa_ref, ws_ref, wo_ref, ln2_ref, wr_ref, br_ref,
                  xg_ref, cnt_s):
    i = pl.program_id(0)

    @pl.when(i == 0)
    def _():
        cnt_s[...] = jnp.zeros_like(cnt_s)

    y_ssd = jnp.dot(u_ref[...], wa_ref[...], preferred_element_type=F32)
    y_sc = jnp.dot(v_ref[...], ws_ref[...], preferred_element_type=F32)
    g1 = g_ref[:, 0:D_MODEL].astype(F32)
    g2 = g_ref[:, D_MODEL:2 * D_MODEL].astype(F32)
    merged = (_sigmoid(g1) * y_ssd + _sigmoid(g2) * y_sc).astype(BF16)
    x1 = x_ref[...] + jnp.dot(merged, wo_ref[...], preferred_element_type=F32)
    xg_ref[:, 0:D_MODEL] = x1
    ms = jnp.mean(x1 * x1, axis=-1, keepdims=True)
    h2 = (x1 * lax.rsqrt(ms + EPS) * ln2_ref[...]).astype(BF16)

    logits = jnp.dot(h2, wr_ref[...], preferred_element_type=F32) + br_ref[...]
    lane = lax.broadcasted_iota(jnp.int32, logits.shape, 1)
    big = jnp.int32(1 << 20)
    gl = jnp.where(lane < N_EGROUPS, logits, NEG_BIG)
    gmax = jnp.max(gl, axis=-1, keepdims=True)
    g_sel = jnp.min(jnp.where(gl == gmax, lane, big), axis=-1, keepdims=True)
    gsum = jnp.sum(jnp.exp(gl - gmax), axis=-1, keepdims=True)
    g_prob = 1.0 / gsum
    lo = N_EGROUPS + EXPERTS_PER_GROUP * g_sel
    emask = jnp.logical_and(lane >= lo, lane < lo + EXPERTS_PER_GROUP)
    el = jnp.where(emask, logits, NEG_BIG)
    m1 = jnp.max(el, axis=-1, keepdims=True)
    e = jnp.where(emask, jnp.exp(el - m1), -1.0)
    i1 = jnp.min(jnp.where(e == 1.0, lane, big), axis=-1, keepdims=True)
    e_rest = jnp.where(lane == i1, -1.0, e)
    e2 = jnp.max(e_rest, axis=-1, keepdims=True)
    i2 = jnp.min(jnp.where(e_rest == e2, lane, big), axis=-1, keepdims=True)
    denom = 1.0 + e2
    w1 = g_prob / denom
    w2 = g_prob * e2 / denom
    gates = jnp.where(lane == i1, w1, 0.0) + jnp.where(lane == i2, w2, 0.0)

    tm = logits.shape[0]
    onehot = jnp.where(lane == g_sel, 1.0, 0.0)
    ri = lax.broadcasted_iota(jnp.int32, (tm, tm), 0)
    ci = lax.broadcasted_iota(jnp.int32, (tm, tm), 1)
    tril01 = jnp.where(ci <= ri, 1.0, 0.0).astype(BF16)
    incl = jnp.dot(tril01, onehot.astype(BF16), preferred_element_type=F32) + cnt_s[...]
    rank = jnp.sum(jnp.where(lane == g_sel, incl - 1.0, 0.0), axis=-1, keepdims=True)
    cnt_s[...] = incl[tm - 1:tm, :]
    xg_ref[:, D_MODEL:D_MODEL + LANES] = jnp.where(
        lane == 0, g_sel.astype(F32), jnp.where(lane == 1, rank, gates))


def _merge(u, v, g, x, wa, ws, wo, ln2, wr, br, tm):
    t = x.shape[0]
    row = lambda i: (i, 0)
    const = lambda i: (0, 0)
    return pl.pallas_call(
        _merge_kernel,
        grid=(t // tm,),
        in_specs=[
            pl.BlockSpec((tm, SSD_INNER), row),
            pl.BlockSpec((tm, SC_WIDTH), row),
            pl.BlockSpec((tm, G_COLS), row),
            pl.BlockSpec((tm, D_MODEL), row),
            pl.BlockSpec((SSD_INNER, D_MODEL), const),
            pl.BlockSpec((SC_WIDTH, D_MODEL), const),
            pl.BlockSpec((D_MODEL, D_MODEL), const),
            pl.BlockSpec((1, D_MODEL), const),
            pl.BlockSpec((D_MODEL, LANES), const),
            pl.BlockSpec((1, LANES), const),
        ],
        out_specs=pl.BlockSpec((tm, XG_COLS), row),
        out_shape=jax.ShapeDtypeStruct((t, XG_COLS), F32),
        scratch_shapes=[pltpu.VMEM((1, LANES), F32)],
        compiler_params=pltpu.CompilerParams(
            dimension_semantics=("arbitrary",), vmem_limit_bytes=VMEM_LIMIT),
        name="merge",
    )(u, v, g, x, wa, ws, wo, ln2, wr, br)


def _moe_kernel(tmg, inv_ref, tg_ref, nv_ref, xg_hbm, wg_ref, wu_ref, wd_ref, ln2_ref, fw_ref,
                y_hbm, xbuf, ybuf, sem_in, sem_out):
    i = pl.program_id(0)
    n = pl.num_programs(0)
    slot = i % 2

    def row_in(tok, r, slt):
        return pltpu.make_async_copy(xg_hbm.at[pl.ds(tok, 1)], xbuf.at[slt, pl.ds(r, 1)], sem_in.at[slt])

    def row_out(tok, r, slt):
        return pltpu.make_async_copy(ybuf.at[slt, pl.ds(r, 1)], y_hbm.at[pl.ds(tok, 1)], sem_out.at[slt])

    def gather(tile, slt):
        def body(r, carry):
            row_in(inv_ref[tile * tmg + r], r, slt).start()
            return carry
        lax.fori_loop(0, tmg, body, 0)

    def wait_gather(slt):
        pltpu.make_async_copy(xg_hbm.at[pl.ds(0, tmg)], xbuf.at[slt], sem_in.at[slt]).wait()

    def scatter(tile, slt):
        def body(r, carry):
            row_out(inv_ref[tile * tmg + r], r, slt).start()
            return carry
        lax.fori_loop(0, nv_ref[tile], body, 0)

    def wait_scatter(tile, slt):
        def body(r, carry):
            row_out(0, r, slt).wait()
            return carry
        lax.fori_loop(0, nv_ref[tile], body, 0)

    @pl.when(i == 0)
    def _():
        gather(0, 0)

    nxt = jnp.minimum(i + 1, n - 1)

    @pl.when(jnp.logical_and(i + 1 < n, nv_ref[nxt] > 0))
    def _():
        gather(nxt, 1 - slot)

    @pl.when(i >= 2)
    def _():
        wait_scatter(jnp.maximum(i - 2, 0), slot)

    @pl.when(jnp.logical_or(nv_ref[i] > 0, i == 0))
    def _():
        wait_gather(slot)
        xg = xbuf[slot]
        x1 = xg[:, 0:D_MODEL]
        gate = xg[:, D_MODEL:D_MODEL + LANES]
        ms = jnp.mean(x1 * x1, axis=-1, keepdims=True)
        h2 = (x1 * lax.rsqrt(ms + EPS) * ln2_ref[...]).astype(BF16)
        a = jnp.dot(h2, wg_ref[...], preferred_element_type=F32)
        up = jnp.dot(h2, wu_ref[...], preferred_element_type=F32)
        act = a * _sigmoid(a) * up
        lane = lax.broadcasted_iota(jnp.int32, gate.shape, 1)
        first = N_EGROUPS + EXPERTS_PER_GROUP * tg_ref[i]
        pieces = []
        for e in range(EXPERTS_PER_GROUP):
            ge = jnp.sum(jnp.where(lane == first + e, gate, 0.0), axis=-1, keepdims=True)
            pieces.append((act[:, e * D_EXPERT:(e + 1) * D_EXPERT] * ge).astype(BF16))
        mo = jnp.dot(jnp.concatenate(pieces, axis=1), wd_ref[...], preferred_element_type=F32)
        x2 = x1 + mo
        ms2 = jnp.mean(x2 * x2, axis=-1, keepdims=True)
        ybuf[slot] = x2 * lax.rsqrt(ms2 + EPS) * fw_ref[...]
        scatter(i, slot)

    @pl.when(i == n - 1)
    def _():
        @pl.when(i >= 1)
        def _():
            wait_scatter(jnp.maximum(i - 1, 0), 1 - slot)
        wait_scatter(i, slot)


def _moe(xg, inv, tile_group, n_valid, wg, wu, wd, ln2, fw, tmg):
    t = xg.shape[0]
    n_tiles = tile_group.shape[0]
    gcols = EXPERTS_PER_GROUP * D_EXPERT
    const = lambda i, inv, tg, nv: (0, 0)
    by_group = lambda i, inv, tg, nv: (tg[i], 0, 0)
    return pl.pallas_call(
        functools.partial(_moe_kernel, tmg),
        grid_spec=pltpu.PrefetchScalarGridSpec(
            num_scalar_prefetch=3,
            grid=(n_tiles,),
            in_specs=[
                pl.BlockSpec(memory_space=pltpu.HBM),
                pl.BlockSpec((None, D_MODEL, gcols), by_group),
                pl.BlockSpec((None, D_MODEL, gcols), by_group),
                pl.BlockSpec((None, gcols, D_MODEL), by_group),
                pl.BlockSpec((1, D_MODEL), const),
                pl.BlockSpec((1, D_MODEL), const),
            ],
            out_specs=pl.BlockSpec(memory_space=pltpu.HBM),
            scratch_shapes=[
                pltpu.VMEM((2, tmg, XG_COLS), F32),
                pltpu.VMEM((2, tmg, D_MODEL), F32),
                pltpu.SemaphoreType.DMA((2,)),
                pltpu.SemaphoreType.DMA((2,)),
            ],
        ),
        out_shape=jax.ShapeDtypeStruct((t, D_MODEL), F32),
        compiler_params=pltpu.CompilerParams(
            dimension_semantics=("arbitrary",), vmem_limit_bytes=VMEM_LIMIT),
        name="moe",
    )(inv, tile_group, n_valid, xg, wg, wu, wd, ln2, fw)


def _route(xg, tmg):
    t = xg.shape[0]
    g = xg[:, D_MODEL].astype(jnp.int32)
    rank = xg[:, D_MODEL + 1].astype(jnp.int32)
    counts = jnp.sum(g[:, None] == jnp.arange(N_EGROUPS, dtype=jnp.int32)[None, :], axis=0, dtype=jnp.int32)
    tiles_per = (counts + tmg - 1) // tmg
    tile_end = jnp.cumsum(tiles_per)
    tile_base = tile_end - tiles_per
    n_tiles = t // tmg + N_EGROUPS
    pos = tile_base[g] * tmg + rank
    inv = jnp.zeros((n_tiles * tmg,), jnp.int32).at[pos].set(jnp.arange(t, dtype=jnp.int32))
    ti = jnp.arange(n_tiles, dtype=jnp.int32)
    tg = jnp.minimum(jnp.sum(ti[:, None] >= tile_end[None, :], axis=1, dtype=jnp.int32), N_EGROUPS - 1)
    nv = jnp.clip(counts[tg] - (ti - tile_base[tg]) * tmg, 0, tmg)
    nv = jnp.where(ti < tile_end[-1], nv, 0).astype(jnp.int32)
    return inv, tg, nv


def _pad_lanes(v, n=LANES):
    v = v.reshape(1, -1).astype(F32)
    return jnp.pad(v, ((0, 0), (0, n - v.shape[1])))


def _pad_hist(h):
    return jnp.pad(h, ((0, 0), (SUBLANES - h.shape[1], 0), (0, 0)))


def kernel(x_prompt, x_sample, state_ssm, state_ssd_conv, state_short_conv, meta_tokens, ln1_w, w_in,
           ssd_conv_w, ssd_conv_b, ssd_dt_bias, ssd_A_log, ssd_D, ssd_norm_w, w_ssd_out, sc_conv_w,
           w_sc_out, w_o, ln2_w, w_rg, b_rg, w_re, b_re, w_gate, w_up, w_down, final_norm_w):
    assert ln1_w.shape[0] == 1, "single-layer trunk"
    bp, sp, _ = x_prompt.shape
    bs, ss, _ = x_sample.shape
    l = 0

    a0, a1 = 0, A_COLS
    d0, d1 = A_COLS, A_COLS + SSD_HEADS
    wi = w_in[l]
    wcat = jnp.concatenate([wi[:, a0:a1], wi[:, d1:]], axis=1).astype(BF16)
    wdt = jnp.pad(wi[:, d0:d1], ((0, 0), (0, LANES - SSD_HEADS))).astype(BF16)
    lnw = ln1_w[l].reshape(1, D_MODEL)
    prm = (ssd_conv_w[l], ssd_conv_b[l].reshape(1, -1), sc_conv_w[l],
           _pad_lanes(ssd_dt_bias[l]), _pad_lanes(ssd_A_log[l]),
           jnp.repeat(ssd_D[l].astype(F32), SSD_HEAD_DIM).reshape(1, SSD_INNER),
           ssd_norm_w[l].reshape(1, SSD_INNER))
    wa = w_ssd_out[l].astype(BF16)
    ws = w_sc_out[l].astype(BF16)
    wo = w_o[l].astype(BF16)
    ln2 = ln2_w[l].reshape(1, D_MODEL)
    wr = jnp.pad(jnp.concatenate([w_rg[l], w_re[l]], axis=1),
                 ((0, 0), (0, LANES - N_EGROUPS - N_EXPERTS))).astype(BF16)
    br = _pad_lanes(jnp.concatenate([b_rg[l], b_re[l]]))
    gcols = EXPERTS_PER_GROUP * D_EXPERT
    wg = jnp.transpose(w_gate[l], (0, 2, 1, 3)).reshape(N_EGROUPS, D_MODEL, gcols).astype(BF16)
    wu = jnp.transpose(w_up[l], (0, 2, 1, 3)).reshape(N_EGROUPS, D_MODEL, gcols).astype(BF16)
    wd = w_down[l].reshape(N_EGROUPS, gcols, D_MODEL).astype(BF16)
    fw = final_norm_w.reshape(1, D_MODEL)

    def mlp_tail(u, v, g, x):
        xg = _merge(u, v, g, x, wa, ws, wo, ln2, wr, br, tm=512)
        inv, tg, nv = _route(xg, MOE_TILE)
        return _moe(xg, inv, tg, nv, wg, wu, wd, ln2, fw, MOE_TILE)

    q = LANES
    xm = jnp.pad(meta_tokens.astype(F32), ((q - N_META, 0), (0, 0)))
    ma, msc, _, mdt = _inproj(xm, lnw, wcat, wdt, tm=q)
    zeros_hist = jnp.zeros((1, SUBLANES, SSD_CONV_DIM), F32)
    zeros_sch = jnp.zeros((1, SUBLANES, SC_WIDTH), F32)
    zeros_st = jnp.zeros((1, SSD_INNER, SSD_STATE), F32)
    _, _, m_st, m_cs, m_scs = _seqmix(ma, msc, mdt, zeros_hist, zeros_sch, zeros_st, prm,
                                      nb=1, nc=1, q=q, ns=1, pad_rows=q - N_META, init_bcast=True)

    xp = x_prompt.reshape(bp * sp, D_MODEL)
    pa, psc, pg, pdt = _inproj(xp, lnw, wcat, wdt, tm=1024)
    pu, pv, p_st, p_cs, p_scs = _seqmix(pa, psc, pdt, _pad_hist(m_cs), _pad_hist(m_scs), m_st, prm,
                                        nb=bp, nc=sp // q, q=q, ns=1, pad_rows=0, init_bcast=True)
    y_prompt = mlp_tail(pu, pv, pg, xp).reshape(bp, sp, D_MODEL)

    ns = 8
    xs = x_sample.reshape(bs * ss, D_MODEL)
    sa, ssc, sg, sdt = _inproj(xs, lnw, wcat, wdt, tm=512)
    su, sv, s_st, s_cs, s_scs = _seqmix(sa, ssc, sdt, _pad_hist(state_ssd_conv[l]),
                                        _pad_hist(state_short_conv[l]),
                                        state_ssm[l].reshape(bs, SSD_INNER, SSD_STATE), prm,
                                        nb=bs // ns, nc=1, q=ns * ss, ns=ns, pad_rows=0, init_bcast=False)
    y_sample = mlp_tail(su, sv, sg, xs).reshape(bs, ss, D_MODEL)

    hshape = (SSD_HEADS, SSD_HEAD_DIM, SSD_STATE)
    return (y_prompt, y_sample,
            p_st.reshape(1, bp, *hshape), p_cs[None], p_scs[None],
            s_st.reshape(1, bs, *hshape), s_cs[None], s_scs[None])
```

```python
import functools

import jax
import jax.numpy as jnp
from jax import lax
from jax.experimental import pallas as pl
from jax.experimental.pallas import tpu as pltpu

F32 = jnp.float32
BF16 = jnp.bfloat16

D_MODEL = 1024
N_META = 16
SSD_INNER = 2048
SSD_HEAD_DIM = 64
SSD_HEADS = 32
SSD_GROUPS = 4
SSD_HPG = 8
SSD_STATE = 128
SSD_CONV = 4
SSD_CONV_DIM = 3072
SC_WIDTH = 1024
SC_CONV = 3
N_EGROUPS = 4
EXPERTS_PER_GROUP = 8
N_EXPERTS = 32
D_EXPERT = 256
EPS = 1e-6

LANES = 128
SUBLANES = 8
GROUP_COLS = SSD_HPG * SSD_HEAD_DIM
A_COLS = SSD_INNER + SSD_CONV_DIM
S_COLS = 3 * SC_WIDTH
G_COLS = 2 * D_MODEL
XG_COLS = D_MODEL + LANES
MOE_TILE = 256
NEG_BIG = -1e30
VMEM_LIMIT = 56 * 1024 * 1024


def _nt_dot(a, b):
    return lax.dot_general(a, b, (((1,), (1,)), ((), ())), preferred_element_type=F32)


def _dot01(m01_bf16, x):
    hi = x.astype(BF16)
    r1 = x - hi.astype(F32)
    mid = r1.astype(BF16)
    lo = (r1 - mid.astype(F32)).astype(BF16)
    out = jnp.dot(m01_bf16, hi, preferred_element_type=F32)
    out = out + jnp.dot(m01_bf16, mid, preferred_element_type=F32)
    return out + jnp.dot(m01_bf16, lo, preferred_element_type=F32)


def _softplus(x):
    return jnp.maximum(x, 0.0) + jnp.log1p(jnp.exp(-jnp.abs(x)))


def _sigmoid(x):
    return 1.0 / (1.0 + jnp.exp(-x))


def _inproj_kernel(nA, nS, x_ref, lnw_ref, w_ref, wdt_ref, oa_ref, os_ref, og_ref, odt_ref, h_s):
    j = pl.program_id(1)

    @pl.when(j == 0)
    def _():
        x = x_ref[...]
        ms = jnp.mean(x * x, axis=-1, keepdims=True)
        h = (x * lax.rsqrt(ms + EPS) * lnw_ref[...]).astype(BF16)
        h_s[...] = h
        odt_ref[...] = jnp.dot(h, wdt_ref[...], preferred_element_type=F32)

    r = jnp.dot(h_s[...], w_ref[...], preferred_element_type=F32).astype(BF16)

    @pl.when(j < nA)
    def _():
        oa_ref[...] = r

    @pl.when(jnp.logical_and(j >= nA, j < nA + nS))
    def _():
        os_ref[...] = r

    @pl.when(j >= nA + nS)
    def _():
        og_ref[...] = r


def _inproj(x, lnw, wcat, wdt, tm, tn=1024):
    t = x.shape[0]
    nA, nS, nG = A_COLS // tn, S_COLS // tn, G_COLS // tn
    nj = nA + nS + nG
    return pl.pallas_call(
        functools.partial(_inproj_kernel, nA, nS),
        grid=(t // tm, nj),
        in_specs=[
            pl.BlockSpec((tm, D_MODEL), lambda i, j: (i, 0)),
            pl.BlockSpec((1, D_MODEL), lambda i, j: (0, 0)),
            pl.BlockSpec((D_MODEL, tn), lambda i, j: (0, j)),
            pl.BlockSpec((D_MODEL, LANES), lambda i, j: (0, 0)),
        ],
        out_specs=[
            pl.BlockSpec((tm, tn), lambda i, j: (i, jnp.minimum(j, nA - 1))),
            pl.BlockSpec((tm, tn), lambda i, j: (i, jnp.clip(j - nA, 0, nS - 1))),
            pl.BlockSpec((tm, tn), lambda i, j: (i, jnp.clip(j - nA - nS, 0, nG - 1))),
            pl.BlockSpec((tm, LANES), lambda i, j: (i, 0)),
        ],
        out_shape=[
            jax.ShapeDtypeStruct((t, A_COLS), BF16),
            jax.ShapeDtypeStruct((t, S_COLS), BF16),
            jax.ShapeDtypeStruct((t, G_COLS), BF16),
            jax.ShapeDtypeStruct((t, LANES), F32),
        ],
        scratch_shapes=[pltpu.VMEM((tm, D_MODEL), BF16)],
        compiler_params=pltpu.CompilerParams(
            dimension_semantics=("arbitrary", "arbitrary"), vmem_limit_bytes=VMEM_LIMIT),
        name="inproj",
    )(x, lnw, wcat, wdt)


def _seqmix_kernel(Q, NS, pad_rows,
                   a_ref, s_ref, dt_ref, hist_ref, schist_ref, st0_ref,
                   cw_ref, cb_ref, scw_ref, dtb_ref, alog_ref, drow_ref, nw_ref,
                   u_ref, v_ref, st_ref, cs_ref, scs_ref,
                   xpad, scpad, xbc_s, y_s, xst_s):
    L = Q // NS
    lg = L.bit_length() - 1
    c = pl.program_id(1)

    @pl.when(c == 0)
    def _():
        xpad[:, 0:SUBLANES, :] = hist_ref[...]
        scpad[:, 0:SUBLANES, :] = schist_ref[...]
        st_ref[...] = st0_ref[...]

    CW = 512
    for cc in range(0, SSD_CONV_DIM, CW):
        raw_all = a_ref[:, SSD_INNER + cc:SSD_INNER + cc + CW].astype(F32)
        w3 = cw_ref[3:4, cc:cc + CW]
        bias = cb_ref[:, cc:cc + CW]
        for s in range(NS):
            raw = raw_all[s * L:(s + 1) * L]
            xpad[s, SUBLANES:SUBLANES + L, cc:cc + CW] = raw
            acc = raw * w3 + bias
            for k in range(SSD_CONV - 1):
                acc = acc + xpad[s, 5 + k:5 + k + L, cc:cc + CW] * cw_ref[k:k + 1, cc:cc + CW]
            xbc_s[s * L:(s + 1) * L, cc:cc + CW] = acc * _sigmoid(acc)
    for s in range(NS):
        cs_ref[s] = xpad[s, L + 5:L + 8, :]
        xpad[s, 0:SUBLANES, :] = xpad[s, L:L + SUBLANES, :]

    for cc in range(0, SC_WIDTH, CW):
        scb = s_ref[:, cc:cc + CW].astype(F32)
        ch_all = (s_ref[:, SC_WIDTH + cc:SC_WIDTH + cc + CW].astype(F32)
                  * s_ref[:, 2 * SC_WIDTH + cc:2 * SC_WIDTH + cc + CW].astype(F32))
        for s in range(NS):
            ch = ch_all[s * L:(s + 1) * L]
            scpad[s, SUBLANES:SUBLANES + L, cc:cc + CW] = ch
            acc = ch * scw_ref[2:3, cc:cc + CW]
            for k in range(SC_CONV - 1):
                acc = acc + scpad[s, 6 + k:6 + k + L, cc:cc + CW] * scw_ref[k:k + 1, cc:cc + CW]
            v_ref[s * L:(s + 1) * L, cc:cc + CW] = (scb[s * L:(s + 1) * L] * acc).astype(BF16)
    for s in range(NS):
        scs_ref[s] = scpad[s, L + 6:L + 8, :]
        scpad[s, 0:SUBLANES, :] = scpad[s, L:L + SUBLANES, :]

    def padrows(x):
        if Q == LANES:
            return x
        return jnp.concatenate([x, jnp.zeros((LANES - Q, x.shape[1]), x.dtype)], axis=0)

    li = lax.broadcasted_iota(jnp.int32, (Q, Q), 0)
    si = lax.broadcasted_iota(jnp.int32, (Q, Q), 1)
    same = (li >> lg) == (si >> lg)
    causal = jnp.logical_and(same, si <= li)
    tril01 = jnp.where(causal, 1.0, 0.0).astype(BF16)
    same01 = jnp.where(same, 1.0, 0.0).astype(BF16)

    dt = _softplus(dt_ref[...] + dtb_ref[...])
    if pad_rows:
        ri = lax.broadcasted_iota(jnp.int32, (Q, LANES), 0)
        dt = jnp.where(ri >= pad_rows, dt, 0.0)
    da = dt * (-jnp.exp(alog_ref[...]))
    acum = _dot01(tril01, da)
    tot = _dot01(same01, da)
    acum_t = padrows(acum).T
    tot_t = padrows(tot).T
    dt_t = padrows(dt).T
    w_t = jnp.exp(tot_t - acum_t) * dt_t

    left_head = lax.broadcasted_iota(jnp.int32, (Q, LANES), 1) < SSD_HEAD_DIM
    rowseq = lax.broadcasted_iota(jnp.int32, (Q, GROUP_COLS), 0) >> lg
    rowseq_p = lax.broadcasted_iota(jnp.int32, (LANES, LANES), 0) >> lg

    for g in range(SSD_GROUPS):
        b_g = xbc_s[:, SSD_INNER + g * SSD_STATE:SSD_INNER + (g + 1) * SSD_STATE]
        c_g = xbc_s[:, SSD_INNER + GROUP_COLS + g * SSD_STATE:SSD_INNER + GROUP_COLS + (g + 1) * SSD_STATE]
        b_gb = b_g.astype(BF16)
        c_gb = c_g.astype(BF16)
        cbm = _nt_dot(c_gb, b_gb)
        yo = None
        for s in range(NS):
            h_s = st_ref[s, g * GROUP_COLS:(g + 1) * GROUP_COLS, :].astype(BF16)
            yo_s = _nt_dot(c_gb, h_s)
            yo = yo_s if yo is None else jnp.where(rowseq == s, yo_s, yo)
        for rp in range(SSD_HPG // 2):
            h0 = g * SSD_HPG + 2 * rp
            cols = slice(h0 * SSD_HEAD_DIM, (h0 + 2) * SSD_HEAD_DIM)
            colbs, w_pair = [], []
            for h in (h0, h0 + 1):
                colb = jnp.broadcast_to(acum[:, h:h + 1], (Q, LANES))
                rowb = jnp.broadcast_to(acum_t[h:h + 1, 0:Q], (Q, Q))
                dec = jnp.exp(jnp.where(causal, colb[:, 0:Q] - rowb, NEG_BIG))
                w_pair.append((cbm * dec * jnp.broadcast_to(dt_t[h:h + 1, 0:Q], (Q, Q))).astype(BF16))
                colbs.append(colb)
            x_p = xbc_s[:, cols]
            rhs = jnp.concatenate([jnp.where(left_head, x_p, 0.0).astype(BF16),
                                   jnp.where(left_head, 0.0, x_p).astype(BF16)], axis=0)
            yd = jnp.dot(jnp.concatenate(w_pair, axis=1), rhs, preferred_element_type=F32)
            ecol = jnp.exp(jnp.where(left_head, colbs[0], colbs[1]))
            y_s[:, cols] = yd + yo[:, rp * LANES:(rp + 1) * LANES] * ecol + drow_ref[:, cols] * x_p

    for jb in range(SSD_INNER // LANES):
        xst_s[jb * LANES:(jb + 1) * LANES, :] = padrows(xbc_s[:, jb * LANES:(jb + 1) * LANES]).T
    for s in range(NS):
        da_b = jnp.exp(jnp.broadcast_to(tot_t[:, s * L:s * L + 1], (LANES, LANES)))
        for g in range(SSD_GROUPS):
            b_p = padrows(xbc_s[:, SSD_INNER + g * SSD_STATE:SSD_INNER + (g + 1) * SSD_STATE])
            if NS > 1:
                b_p = jnp.where(rowseq_p == s, b_p, 0.0)
            pieces = []
            for r in range(SSD_HPG):
                h = g * SSD_HPG + r
                pieces.append(xst_s[h * SSD_HEAD_DIM:(h + 1) * SSD_HEAD_DIM, :] * w_t[h:h + 1, :])
            xw_t = jnp.concatenate(pieces, axis=0).astype(BF16)
            upd = jnp.dot(xw_t, b_p.astype(BF16), preferred_element_type=F32)
            for r in range(SSD_HPG):
                h = g * SSD_HPG + r
                rows = slice(h * SSD_HEAD_DIM, (h + 1) * SSD_HEAD_DIM)
                dec_h = jnp.broadcast_to(da_b[h:h + 1, :], (SSD_HEAD_DIM, SSD_STATE))
                st_ref[s, rows, :] = dec_h * st_ref[s, rows, :] + upd[r * SSD_HEAD_DIM:(r + 1) * SSD_HEAD_DIM, :]

    for g in range(SSD_GROUPS):
        cols = slice(g * GROUP_COLS, (g + 1) * GROUP_COLS)
        z = a_ref[:, cols].astype(F32)
        ug = y_s[:, cols] * (z * _sigmoid(z))
        ms = jnp.mean(ug * ug, axis=-1, keepdims=True)
        u_ref[:, cols] = (ug * lax.rsqrt(ms + EPS) * nw_ref[:, cols]).astype(BF16)


def _seqmix(pa, ps, pdt, hist, schist, st0, prm, *, nb, nc, q, ns, pad_rows, init_bcast):
    L = q // ns
    cw, cb, scw, dtb, alog, drow, nw = prm

    def tok(b, c):
        return (b * nc + c, 0)

    def init3(b, c):
        return (0 if init_bcast else b, 0, 0)

    def const2(b, c):
        return (0, 0)

    def per_b(b, c):
        return (b, 0, 0)

    n_seq = nb * ns
    return pl.pallas_call(
        functools.partial(_seqmix_kernel, q, ns, pad_rows),
        grid=(nb, nc),
        in_specs=[
            pl.BlockSpec((q, A_COLS), tok),
            pl.BlockSpec((q, S_COLS), tok),
            pl.BlockSpec((q, LANES), tok),
            pl.BlockSpec((ns, SUBLANES, SSD_CONV_DIM), init3),
            pl.BlockSpec((ns, SUBLANES, SC_WIDTH), init3),
            pl.BlockSpec((ns, SSD_INNER, SSD_STATE), init3),
            pl.BlockSpec((SSD_CONV, SSD_CONV_DIM), const2),
            pl.BlockSpec((1, SSD_CONV_DIM), const2),
            pl.BlockSpec((SC_CONV, SC_WIDTH), const2),
            pl.BlockSpec((1, LANES), const2),
            pl.BlockSpec((1, LANES), const2),
            pl.BlockSpec((1, SSD_INNER), const2),
            pl.BlockSpec((1, SSD_INNER), const2),
        ],
        out_specs=[
            pl.BlockSpec((q, SSD_INNER), tok),
            pl.BlockSpec((q, SC_WIDTH), tok),
            pl.BlockSpec((ns, SSD_INNER, SSD_STATE), per_b),
            pl.BlockSpec((ns, SSD_CONV - 1, SSD_CONV_DIM), per_b),
            pl.BlockSpec((ns, SC_CONV - 1, SC_WIDTH), per_b),
        ],
        out_shape=[
            jax.ShapeDtypeStruct((nb * nc * q, SSD_INNER), BF16),
            jax.ShapeDtypeStruct((nb * nc * q, SC_WIDTH), BF16),
            jax.ShapeDtypeStruct((n_seq, SSD_INNER, SSD_STATE), F32),
            jax.ShapeDtypeStruct((n_seq, SSD_CONV - 1, SSD_CONV_DIM), F32),
            jax.ShapeDtypeStruct((n_seq, SC_CONV - 1, SC_WIDTH), F32),
        ],
        scratch_shapes=[
            pltpu.VMEM((ns, SUBLANES + L, SSD_CONV_DIM), F32),
            pltpu.VMEM((ns, SUBLANES + L, SC_WIDTH), F32),
            pltpu.VMEM((q, SSD_CONV_DIM), F32),
            pltpu.VMEM((q, SSD_INNER), F32),
            pltpu.VMEM((SSD_INNER, LANES), F32),
        ],
        compiler_params=pltpu.CompilerParams(
            dimension_semantics=("arbitrary", "arbitrary"), vmem_limit_bytes=VMEM_LIMIT),
        name="seqmix",
    )(pa, ps, pdt, hist, schist, st0, cw, cb, scw, dtb, alog, drow, nw)


def _merge_kernel(u_ref, v_ref, g_ref, x_ref, wa_ref, ws_ref, wo_ref, ln2_ref, wr_ref, br_ref,
                  xg_ref, cnt_s):
    i = pl.program_id(0)

    @pl.when(i == 0)
    def _():
        cnt_s[...] = jnp.zeros_like(cnt_s)

    y_ssd = jnp.dot(u_ref[...], wa_ref[...], preferred_element_type=F32)
    y_sc = jnp.dot(v_ref[...], ws_ref[...], preferred_element_type=F32)
    g1 = g_ref[:, 0:D_MODEL].astype(F32)
    g2 = g_ref[:, D_MODEL:2 * D_MODEL].astype(F32)
    merged = (_sigmoid(g1) * y_ssd + _sigmoid(g2) * y_sc).astype(BF16)
    x1 = x_ref[...] + jnp.dot(merged, wo_ref[...], preferred_element_type=F32)
    xg_ref[:, 0:D_MODEL] = x1
    ms = jnp.mean(x1 * x1, axis=-1, keepdims=True)
    h2 = (x1 * lax.rsqrt(ms + EPS) * ln2_ref[...]).astype(BF16)

    logits = jnp.dot(h2, wr_ref[...], preferred_element_type=F32) + br_ref[...]
    lane = lax.broadcasted_iota(jnp.int32, logits.shape, 1)
    big = jnp.int32(1 << 20)
    gl = jnp.where(lane < N_EGROUPS, logits, NEG_BIG)
    gmax = jnp.max(gl, axis=-1, keepdims=True)
    g_sel = jnp.min(jnp.where(gl == gmax, lane, big), axis=-1, keepdims=True)
    gsum = jnp.sum(jnp.exp(gl - gmax), axis=-1, keepdims=True)
    g_prob = 1.0 / gsum
    lo = N_EGROUPS + EXPERTS_PER_GROUP * g_sel
    emask = jnp.logical_and(lane >= lo, lane < lo + EXPERTS_PER_GROUP)
    el = jnp.where(emask, logits, NEG_BIG)
    m1 = jnp.max(el, axis=-1, keepdims=True)
    e = jnp.where(emask, jnp.exp(el - m1), -1.0)
    i1 = jnp.min(jnp.where(e == 1.0, lane, big), axis=-1, keepdims=True)
    e_rest = jnp.where(lane == i1, -1.0, e)
    e2 = jnp.max(e_rest, axis=-1, keepdims=True)
    i2 = jnp.min(jnp.where(e_rest == e2, lane, big), axis=-1, keepdims=True)
    denom = 1.0 + e2
    w1 = g_prob / denom
    w2 = g_prob * e2 / denom
    gates = jnp.where(lane == i1, w1, 0.0) + jnp.where(lane == i2, w2, 0.0)

    tm = logits.shape[0]
    onehot = jnp.where(lane == g_sel, 1.0, 0.0)
    ri = lax.broadcasted_iota(jnp.int32, (tm, tm), 0)
    ci = lax.broadcasted_iota(jnp.int32, (tm, tm), 1)
    tril01 = jnp.where(ci <= ri, 1.0, 0.0).astype(BF16)
    incl = jnp.dot(tril01, onehot.astype(BF16), preferred_element_type=F32) + cnt_s[...]
    rank = jnp.sum(jnp.where(lane == g_sel, incl - 1.0, 0.0), axis=-1, keepdims=True)
    cnt_s[...] = incl[tm - 1:tm, :]
    xg_ref[:, D_MODEL:D_MODEL + LANES] = jnp.where(
        lane == 0, g_sel.astype(F32), jnp.where(lane == 1, rank, gates))


def _merge(u, v, g, x, wa, ws, wo, ln2, wr, br, tm):
    t = x.shape[0]
    row = lambda i: (i, 0)
    const = lambda i: (0, 0)
    return pl.pallas_call(
        _merge_kernel,
        grid=(t // tm,),
        in_specs=[
            pl.BlockSpec((tm, SSD_INNER), row),
            pl.BlockSpec((tm, SC_WIDTH), row),
            pl.BlockSpec((tm, G_COLS), row),
            pl.BlockSpec((tm, D_MODEL), row),
            pl.BlockSpec((SSD_INNER, D_MODEL), const),
            pl.BlockSpec((SC_WIDTH, D_MODEL), const),
            pl.BlockSpec((D_MODEL, D_MODEL), const),
            pl.BlockSpec((1, D_MODEL), const),
            pl.BlockSpec((D_MODEL, LANES), const),
            pl.BlockSpec((1, LANES), const),
        ],
        out_specs=pl.BlockSpec((tm, XG_COLS), row),
        out_shape=jax.ShapeDtypeStruct((t, XG_COLS), F32),
        scratch_shapes=[pltpu.VMEM((1, LANES), F32)],
        compiler_params=pltpu.CompilerParams(
            dimension_semantics=("arbitrary",), vmem_limit_bytes=VMEM_LIMIT),
        name="merge",
    )(u, v, g, x, wa, ws, wo, ln2, wr, br)


def _moe_kernel(tmg, inv_ref, tg_ref, nv_ref, xg_hbm, wg_ref, wu_ref, wd_ref, ln2_ref, fw_ref,
                y_hbm, xbuf, ybuf, sem_in, sem_out):
    i = pl.program_id(0)
    n = pl.num_programs(0)
    slot = i % 2

    def row_in(tok, r, slt):
        return pltpu.make_async_copy(xg_hbm.at[pl.ds(tok, 1)], xbuf.at[slt, pl.ds(r, 1)], sem_in.at[slt])

    def row_out(tok, r, slt):
        return pltpu.make_async_copy(ybuf.at[slt, pl.ds(r, 1)], y_hbm.at[pl.ds(tok, 1)], sem_out.at[slt])

    unroll = 8

    def gather(tile, slt):
        def body(r, carry):
            row_in(inv_ref[tile * tmg + r], r, slt).start()
            return carry
        lax.fori_loop(0, tmg, body, 0, unroll=unroll)

    def wait_gather(slt):
        pltpu.make_async_copy(xg_hbm.at[pl.ds(0, tmg)], xbuf.at[slt], sem_in.at[slt]).wait()

    def scatter(tile, slt):
        def body(r, carry):
            row_out(inv_ref[tile * tmg + r], r, slt).start()
            return carry
        full = nv_ref[tile] == tmg

        @pl.when(full)
        def _():
            for r in range(tmg):
                body(r, 0)

        @pl.when(jnp.logical_not(full))
        def _():
            lax.fori_loop(0, nv_ref[tile], body, 0)

    def wait_scatter(tile, slt):
        def body(r, carry):
            row_out(0, r, slt).wait()
            return carry
        full = nv_ref[tile] == tmg

        @pl.when(full)
        def _():
            pltpu.make_async_copy(ybuf.at[slt], y_hbm.at[pl.ds(0, tmg)], sem_out.at[slt]).wait()

        @pl.when(jnp.logical_not(full))
        def _():
            lax.fori_loop(0, nv_ref[tile], body, 0)

    @pl.when(i == 0)
    def _():
        gather(0, 0)

    nxt = jnp.minimum(i + 1, n - 1)

    @pl.when(i >= 2)
    def _():
        wait_scatter(jnp.maximum(i - 2, 0), slot)

    @pl.when(nv_ref[i] == 0)
    def _():
        gather(nxt, 1 - slot)
        wait_gather(slot)

    @pl.when(nv_ref[i] > 0)
    def _():
        wait_gather(slot)
        x1 = xbuf[slot, :, 0:D_MODEL]
        for r in range(tmg // 2):
            row_in(inv_ref[nxt * tmg + r], r, 1 - slot).start()
        gate = xbuf[slot, :, D_MODEL:D_MODEL + LANES]
        for r in range(tmg // 2, tmg):
            row_in(inv_ref[nxt * tmg + r], r, 1 - slot).start()
        x1_res = xbuf[slot, :, 0:D_MODEL]
        ms = jnp.mean(x1 * x1, axis=-1, keepdims=True)
        h2 = (x1 * lax.rsqrt(ms + EPS) * ln2_ref[...]).astype(BF16)
        a = jnp.dot(h2, wg_ref[...], preferred_element_type=F32)
        up = jnp.dot(h2, wu_ref[...], preferred_element_type=F32)
        act = a * _sigmoid(a) * up
        lane = lax.broadcasted_iota(jnp.int32, gate.shape, 1)
        first = N_EGROUPS + EXPERTS_PER_GROUP * tg_ref[i]
        pieces = []
        for e in range(EXPERTS_PER_GROUP):
            ge = jnp.sum(jnp.where(lane == first + e, gate, 0.0), axis=-1, keepdims=True)
            pieces.append((act[:, e * D_EXPERT:(e + 1) * D_EXPERT] * ge).astype(BF16))
        mo = jnp.dot(jnp.concatenate(pieces, axis=1), wd_ref[...], preferred_element_type=F32)
        x2 = x1_res + mo
        ms2 = jnp.mean(x2 * x2, axis=-1, keepdims=True)
        ybuf[slot] = x2 * lax.rsqrt(ms2 + EPS) * fw_ref[...]
        scatter(i, slot)

    @pl.when(i == n - 1)
    def _():
        wait_gather(1 - slot)

        @pl.when(i >= 1)
        def _():
            wait_scatter(jnp.maximum(i - 1, 0), 1 - slot)
        wait_scatter(i, slot)


def _moe(xg, inv, tile_group, n_valid, wg, wu, wd, ln2, fw, tmg):
    t = xg.shape[0]
    n_tiles = tile_group.shape[0]
    gcols = EXPERTS_PER_GROUP * D_EXPERT
    const = lambda i, inv, tg, nv: (0, 0)
    by_group = lambda i, inv, tg, nv: (tg[i], 0, 0)
    return pl.pallas_call(
        functools.partial(_moe_kernel, tmg),
        grid_spec=pltpu.PrefetchScalarGridSpec(
            num_scalar_prefetch=3,
            grid=(n_tiles,),
            in_specs=[
                pl.BlockSpec(memory_space=pltpu.HBM),
                pl.BlockSpec((None, D_MODEL, gcols), by_group),
                pl.BlockSpec((None, D_MODEL, gcols), by_group),
                pl.BlockSpec((None, gcols, D_MODEL), by_group),
                pl.BlockSpec((1, D_MODEL), const),
                pl.BlockSpec((1, D_MODEL), const),
            ],
            out_specs=pl.BlockSpec(memory_space=pltpu.HBM),
            scratch_shapes=[
                pltpu.VMEM((2, tmg, XG_COLS), F32),
                pltpu.VMEM((2, tmg, D_MODEL), F32),
                pltpu.SemaphoreType.DMA((2,)),
                pltpu.SemaphoreType.DMA((2,)),
            ],
        ),
        out_shape=jax.ShapeDtypeStruct((t, D_MODEL), F32),
        compiler_params=pltpu.CompilerParams(
            dimension_semantics=("arbitrary",), vmem_limit_bytes=VMEM_LIMIT),
        name="moe",
    )(inv, tile_group, n_valid, xg, wg, wu, wd, ln2, fw)


def _route(xg, tmg):
    t = xg.shape[0]
    g = xg[:, D_MODEL].astype(jnp.int32)
    rank = xg[:, D_MODEL + 1].astype(jnp.int32)
    counts = jnp.sum(g[:, None] == jnp.arange(N_EGROUPS, dtype=jnp.int32)[None, :], axis=0, dtype=jnp.int32)
    tiles_per = (counts + tmg - 1) // tmg
    tile_end = jnp.cumsum(tiles_per)
    tile_base = tile_end - tiles_per
    n_tiles = t // tmg + N_EGROUPS
    pos = tile_base[g] * tmg + rank
    inv = jnp.zeros((n_tiles * tmg,), jnp.int32).at[pos].set(jnp.arange(t, dtype=jnp.int32))
    ti = jnp.arange(n_tiles, dtype=jnp.int32)
    tg = jnp.minimum(jnp.sum(ti[:, None] >= tile_end[None, :], axis=1, dtype=jnp.int32), N_EGROUPS - 1)
    nv = jnp.clip(counts[tg] - (ti - tile_base[tg]) * tmg, 0, tmg)
    nv = jnp.where(ti < tile_end[-1], nv, 0).astype(jnp.int32)
    return inv, tg, nv


def _pad_lanes(v, n=LANES):
    v = v.reshape(1, -1).astype(F32)
    return jnp.pad(v, ((0, 0), (0, n - v.shape[1])))


def _pad_hist(h):
    return jnp.pad(h, ((0, 0), (SUBLANES - h.shape[1], 0), (0, 0)))


def kernel(x_prompt, x_sample, state_ssm, state_ssd_conv, state_short_conv, meta_tokens, ln1_w, w_in,
           ssd_conv_w, ssd_conv_b, ssd_dt_bias, ssd_A_log, ssd_D, ssd_norm_w, w_ssd_out, sc_conv_w,
           w_sc_out, w_o, ln2_w, w_rg, b_rg, w_re, b_re, w_gate, w_up, w_down, final_norm_w):
    assert ln1_w.shape[0] == 1, "single-layer trunk"
    bp, sp, _ = x_prompt.shape
    bs, ss, _ = x_sample.shape
    l = 0

    a0, a1 = 0, A_COLS
    d0, d1 = A_COLS, A_COLS + SSD_HEADS
    wi = w_in[l]
    wcat = jnp.concatenate([wi[:, a0:a1], wi[:, d1:]], axis=1).astype(BF16)
    wdt = jnp.pad(wi[:, d0:d1], ((0, 0), (0, LANES - SSD_HEADS))).astype(BF16)
    lnw = ln1_w[l].reshape(1, D_MODEL)
    prm = (ssd_conv_w[l], ssd_conv_b[l].reshape(1, -1), sc_conv_w[l],
           _pad_lanes(ssd_dt_bias[l]), _pad_lanes(ssd_A_log[l]),
           jnp.repeat(ssd_D[l].astype(F32), SSD_HEAD_DIM).reshape(1, SSD_INNER),
           ssd_norm_w[l].reshape(1, SSD_INNER))
    wa = w_ssd_out[l].astype(BF16)
    ws = w_sc_out[l].astype(BF16)
    wo = w_o[l].astype(BF16)
    ln2 = ln2_w[l].reshape(1, D_MODEL)
    wr = jnp.pad(jnp.concatenate([w_rg[l], w_re[l]], axis=1),
                 ((0, 0), (0, LANES - N_EGROUPS - N_EXPERTS))).astype(BF16)
    br = _pad_lanes(jnp.concatenate([b_rg[l], b_re[l]]))
    gcols = EXPERTS_PER_GROUP * D_EXPERT
    wg = jnp.transpose(w_gate[l], (0, 2, 1, 3)).reshape(N_EGROUPS, D_MODEL, gcols).astype(BF16)
    wu = jnp.transpose(w_up[l], (0, 2, 1, 3)).reshape(N_EGROUPS, D_MODEL, gcols).astype(BF16)
    wd = w_down[l].reshape(N_EGROUPS, gcols, D_MODEL).astype(BF16)
    fw = final_norm_w.reshape(1, D_MODEL)

    def mlp_tail(u, v, g, x):
        xg = _merge(u, v, g, x, wa, ws, wo, ln2, wr, br, tm=512)
        inv, tg, nv = _route(xg, MOE_TILE)
        return _moe(xg, inv, tg, nv, wg, wu, wd, ln2, fw, MOE_TILE)

    q = LANES
    xm = jnp.pad(meta_tokens.astype(F32), ((q - N_META, 0), (0, 0)))
    ma, msc, _, mdt = _inproj(xm, lnw, wcat, wdt, tm=q)
    zeros_hist = jnp.zeros((1, SUBLANES, SSD_CONV_DIM), F32)
    zeros_sch = jnp.zeros((1, SUBLANES, SC_WIDTH), F32)
    zeros_st = jnp.zeros((1, SSD_INNER, SSD_STATE), F32)
    _, _, m_st, m_cs, m_scs = _seqmix(ma, msc, mdt, zeros_hist, zeros_sch, zeros_st, prm,
                                      nb=1, nc=1, q=q, ns=1, pad_rows=q - N_META, init_bcast=True)

    xp = x_prompt.reshape(bp * sp, D_MODEL)
    pa, psc, pg, pdt = _inproj(xp, lnw, wcat, wdt, tm=1024)
    pu, pv, p_st, p_cs, p_scs = _seqmix(pa, psc, pdt, _pad_hist(m_cs), _pad_hist(m_scs), m_st, prm,
                                        nb=bp, nc=sp // q, q=q, ns=1, pad_rows=0, init_bcast=True)
    y_prompt = mlp_tail(pu, pv, pg, xp).reshape(bp, sp, D_MODEL)

    ns = 8
    xs = x_sample.reshape(bs * ss, D_MODEL)
    sa, ssc, sg, sdt = _inproj(xs, lnw, wcat, wdt, tm=512)
    su, sv, s_st, s_cs, s_scs = _seqmix(sa, ssc, sdt, _pad_hist(state_ssd_conv[l]),
                                        _pad_hist(state_short_conv[l]),
                                        state_ssm[l].reshape(bs, SSD_INNER, SSD_STATE), prm,
                                        nb=bs // ns, nc=1, q=ns * ss, ns=ns, pad_rows=0, init_bcast=False)
    y_sample = mlp_tail(su, sv, sg, xs).reshape(bs, ss, D_MODEL)

    hshape = (SSD_HEADS, SSD_HEAD_DIM, SSD_STATE)
    return (y_prompt, y_sample,
            p_st.reshape(1, bp, *hshape), p_cs[None], p_scs[None],
            s_st.reshape(1, bs, *hshape), s_cs[None], s_scs[None])
```

```python
import functools

import jax
import jax.numpy as jnp
import numpy as np
from jax import lax
from jax.experimental import pallas as pl
from jax.experimental.pallas import tpu as pltpu

F32 = jnp.float32
BF16 = jnp.bfloat16

D_MODEL = 1024
N_META = 16
SSD_INNER = 2048
SSD_HEAD_DIM = 64
SSD_HEADS = 32
SSD_GROUPS = 4
SSD_HPG = 8
SSD_STATE = 128
SSD_CONV = 4
SSD_CONV_DIM = 3072
SC_WIDTH = 1024
SC_CONV = 3
N_EGROUPS = 4
EXPERTS_PER_GROUP = 8
N_EXPERTS = 32
D_EXPERT = 256
EPS = 1e-6

LANES = 128
SUBLANES = 8
GROUP_COLS = SSD_HPG * SSD_HEAD_DIM
A_COLS = SSD_INNER + SSD_CONV_DIM
S_COLS = 3 * SC_WIDTH
G_COLS = 2 * D_MODEL
XG_COLS = D_MODEL + LANES
MOE_TILE = 256
RANK_BLOCK = 512
NEG_BIG = -1e30
LOG2E = 1.4426950408889634
VMEM_LIMIT = 56 * 1024 * 1024


def _nt_dot(a, b):
    return lax.dot_general(a, b, (((1,), (1,)), ((), ())), preferred_element_type=F32)


def _dot01(m01_bf16, x):
    hi = x.astype(BF16)
    r1 = x - hi.astype(F32)
    mid = r1.astype(BF16)
    lo = (r1 - mid.astype(F32)).astype(BF16)
    out = jnp.dot(m01_bf16, hi, preferred_element_type=F32)
    out = out + jnp.dot(m01_bf16, mid, preferred_element_type=F32)
    return out + jnp.dot(m01_bf16, lo, preferred_element_type=F32)


def _softplus(x):
    return jnp.maximum(x, 0.0) + jnp.log1p(jnp.exp(-jnp.abs(x)))


def _sigmoid(x):
    return 1.0 / (1.0 + jnp.exp(-x))


def _inproj_kernel(nA, nS, x_ref, lnw_ref, wa_ref, wsg_ref, wdt_ref, oa_ref, os_ref, og_ref, odt_ref, h_s):
    j = pl.program_id(1)

    @pl.when(j == 0)
    def _():
        x = x_ref[...]
        ms = jnp.mean(x * x, axis=-1, keepdims=True)
        h = (x * lax.rsqrt(ms + EPS) * lnw_ref[...]).astype(BF16)
        h_s[...] = h
        odt_ref[...] = jnp.dot(h, wdt_ref[...], preferred_element_type=F32)

    def proj(w_ref):
        return jnp.dot(h_s[...], w_ref[...], preferred_element_type=F32).astype(BF16)

    @pl.when(j < nA)
    def _():
        oa_ref[...] = proj(wa_ref)

    @pl.when(jnp.logical_and(j >= nA, j < nA + nS))
    def _():
        os_ref[...] = proj(wsg_ref)

    @pl.when(j >= nA + nS)
    def _():
        og_ref[...] = proj(wsg_ref)


def _inproj(x, lnw, wa, wsg, wdt, tm, tn=512):
    t = x.shape[0]
    nA, nS, nG = A_COLS // tn, S_COLS // tn, G_COLS // tn
    nj = nA + nS + nG
    return pl.pallas_call(
        functools.partial(_inproj_kernel, nA, nS),
        grid=(t // tm, nj),
        in_specs=[
            pl.BlockSpec((tm, D_MODEL), lambda i, j: (i, 0)),
            pl.BlockSpec((1, D_MODEL), lambda i, j: (0, 0)),
            pl.BlockSpec((D_MODEL, tn), lambda i, j: (0, jnp.minimum(j, nA - 1))),
            pl.BlockSpec((D_MODEL, tn), lambda i, j: (0, jnp.clip(j - nA, 0, nS + nG - 1))),
            pl.BlockSpec((D_MODEL, LANES), lambda i, j: (0, 0)),
        ],
        out_specs=[
            pl.BlockSpec((tm, tn), lambda i, j: (i, jnp.minimum(j, nA - 1))),
            pl.BlockSpec((tm, tn), lambda i, j: (i, jnp.clip(j - nA, 0, nS - 1))),
            pl.BlockSpec((tm, tn), lambda i, j: (i, jnp.clip(j - nA - nS, 0, nG - 1))),
            pl.BlockSpec((tm, LANES), lambda i, j: (i, 0)),
        ],
        out_shape=[
            jax.ShapeDtypeStruct((t, A_COLS), BF16),
            jax.ShapeDtypeStruct((t, S_COLS), BF16),
            jax.ShapeDtypeStruct((t, G_COLS), BF16),
            jax.ShapeDtypeStruct((t, LANES), F32),
        ],
        scratch_shapes=[pltpu.VMEM((tm, D_MODEL), BF16)],
        compiler_params=pltpu.CompilerParams(
            dimension_semantics=("arbitrary", "arbitrary"), vmem_limit_bytes=VMEM_LIMIT),
        name="inproj",
    )(x, lnw, wa, wsg, wdt)


def halo_rows(ns):
    return max(2 * SUBLANES, ns * SUBLANES)


def _shift_matrices(q, ns, taps):
    L = q // ns
    hb = halo_rows(ns)
    r = np.arange(q)[:, None]
    col = np.arange(3 * hb + q)[None, :]
    seq, t = r // L, r % L
    mats = []
    for k in range(taps - 1):
        d = taps - 1 - k
        from_cur = (col >= 3 * hb) & (col - 3 * hb == r - d) & (t >= d)
        from_hist = (col < 3 * hb) & (col % hb == seq * SUBLANES + SUBLANES - d + t) & (t < d)
        mats.append(from_cur | from_hist)
    return jnp.asarray(np.stack(mats), dtype=BF16)


def _seqmix_kernel(Q, NS, pad_rows,
                   a_ref, s_ref, dt_ref, hist_ref, schist_ref, st0_ref, shift_ref,
                   cw_ref, cb_ref, scw_ref, dtb_ref, alog_ref, drow_ref, nw_ref,
                   u_ref, v_ref, st_ref, cs_ref, scs_ref,
                   halo, scpad, xbc_s, y_s, xst_s):
    L = Q // NS
    lg = L.bit_length() - 1
    c = pl.program_id(1)

    @pl.when(c == 0)
    def _():
        for s in range(NS):
            halo[s * SUBLANES:(s + 1) * SUBLANES, :] = hist_ref[s]
        scpad[:, 0:SUBLANES, :] = schist_ref[...]
        st_ref[...] = st0_ref[...]

    CW = 256
    hb = halo_rows(NS)
    for cc in range(0, SSD_CONV_DIM, CW):
        raw_b = a_ref[:, SSD_INNER + cc:SSD_INNER + cc + CW]
        hl = halo[:, cc:cc + CW]
        if hb > hl.shape[0]:
            hl = jnp.concatenate([hl, jnp.zeros((hb - hl.shape[0], CW), F32)], axis=0)
        h_hi = hl.astype(BF16)
        h_r = hl - h_hi.astype(F32)
        h_mid = h_r.astype(BF16)
        h_lo = (h_r - h_mid.astype(F32)).astype(BF16)
        ext = jnp.concatenate([h_hi, h_mid, h_lo, raw_b], axis=0)
        raw = raw_b.astype(F32)
        acc = raw * cw_ref[3:4, cc:cc + CW] + cb_ref[:, cc:cc + CW]
        for k in range(SSD_CONV - 1):
            acc = acc + jnp.dot(shift_ref[k], ext, preferred_element_type=F32) * cw_ref[k:k + 1, cc:cc + CW]
        xbc_s[:, cc:cc + CW] = acc * _sigmoid(acc)
        for s in range(NS):
            halo[s * SUBLANES:(s + 1) * SUBLANES, cc:cc + CW] = raw[(s + 1) * L - SUBLANES:(s + 1) * L]
    for s in range(NS):
        cs_ref[s] = halo[(s + 1) * SUBLANES - (SSD_CONV - 1):(s + 1) * SUBLANES, :]

    for cc in range(0, SC_WIDTH, CW):
        scb = s_ref[:, cc:cc + CW].astype(F32)
        ch_all = (s_ref[:, SC_WIDTH + cc:SC_WIDTH + cc + CW].astype(F32)
                  * s_ref[:, 2 * SC_WIDTH + cc:2 * SC_WIDTH + cc + CW].astype(F32))
        for s in range(NS):
            ch = ch_all[s * L:(s + 1) * L]
            scpad[s, SUBLANES:SUBLANES + L, cc:cc + CW] = ch
            acc = ch * scw_ref[2:3, cc:cc + CW]
            for k in range(SC_CONV - 1):
                acc = acc + scpad[s, 6 + k:6 + k + L, cc:cc + CW] * scw_ref[k:k + 1, cc:cc + CW]
            v_ref[s * L:(s + 1) * L, cc:cc + CW] = (scb[s * L:(s + 1) * L] * acc).astype(BF16)
    for s in range(NS):
        scs_ref[s] = scpad[s, L + 6:L + 8, :]
        scpad[s, 0:SUBLANES, :] = scpad[s, L:L + SUBLANES, :]

    def padrows(x):
        if Q == LANES:
            return x
        return jnp.concatenate([x, jnp.zeros((LANES - Q, x.shape[1]), x.dtype)], axis=0)

    li = lax.broadcasted_iota(jnp.int32, (Q, Q), 0)
    si = lax.broadcasted_iota(jnp.int32, (Q, Q), 1)
    same = (li >> lg) == (si >> lg)
    causal = jnp.logical_and(same, si <= li)
    tril01 = jnp.where(causal, 1.0, 0.0).astype(BF16)
    same01 = jnp.where(same, 1.0, 0.0).astype(BF16)

    dt = _softplus(dt_ref[...] + dtb_ref[...])
    if pad_rows:
        ri = lax.broadcasted_iota(jnp.int32, (Q, LANES), 0)
        dt = jnp.where(ri >= pad_rows, dt, 0.0)
    da = dt * (-jnp.exp(alog_ref[...]))
    acum = _dot01(tril01, da)
    tot = _dot01(same01, da)
    acum_t = padrows(acum).T
    tot_t = padrows(tot).T
    dt_t = padrows(dt).T
    w_t = jnp.exp(tot_t - acum_t) * dt_t
    acum2 = acum * LOG2E
    rowq_t = acum_t * LOG2E - jnp.log2(dt_t)

    left_head = lax.broadcasted_iota(jnp.int32, (Q, LANES), 1) < SSD_HEAD_DIM
    rowseq = lax.broadcasted_iota(jnp.int32, (Q, GROUP_COLS), 0) >> lg
    rowseq_p = lax.broadcasted_iota(jnp.int32, (LANES, LANES), 0) >> lg

    for g in range(SSD_GROUPS):
        b_g = xbc_s[:, SSD_INNER + g * SSD_STATE:SSD_INNER + (g + 1) * SSD_STATE]
        c_g = xbc_s[:, SSD_INNER + GROUP_COLS + g * SSD_STATE:SSD_INNER + GROUP_COLS + (g + 1) * SSD_STATE]
        b_gb = b_g.astype(BF16)
        c_gb = c_g.astype(BF16)
        cbm = _nt_dot(c_gb, b_gb)
        yo = None
        for s in range(NS):
            h_s = st_ref[s, g * GROUP_COLS:(g + 1) * GROUP_COLS, :].astype(BF16)
            yo_s = _nt_dot(c_gb, h_s)
            yo = yo_s if yo is None else jnp.where(rowseq == s, yo_s, yo)
        for rp in range(SSD_HPG // 2):
            h0 = g * SSD_HPG + 2 * rp
            cols = slice(h0 * SSD_HEAD_DIM, (h0 + 2) * SSD_HEAD_DIM)
            colbs, w_pair = [], []
            for h in (h0, h0 + 1):
                colb = jnp.broadcast_to(acum2[:, h:h + 1], (Q, LANES))
                rowb = jnp.broadcast_to(rowq_t[h:h + 1, 0:Q], (Q, Q))
                dec_dt = jnp.exp2(jnp.where(causal, colb[:, 0:Q] - rowb, NEG_BIG))
                w_pair.append((cbm * dec_dt).astype(BF16))
                colbs.append(colb)
            x_p = xbc_s[:, cols]
            rhs = jnp.concatenate([jnp.where(left_head, x_p, 0.0).astype(BF16),
                                   jnp.where(left_head, 0.0, x_p).astype(BF16)], axis=0)
            yd = jnp.dot(jnp.concatenate(w_pair, axis=1), rhs, preferred_element_type=F32)
            ecol = jnp.exp2(jnp.where(left_head, colbs[0], colbs[1]))
            y_s[:, cols] = yd + yo[:, rp * LANES:(rp + 1) * LANES] * ecol + drow_ref[:, cols] * x_p

    for jb in range(SSD_INNER // LANES):
        xst_s[jb * LANES:(jb + 1) * LANES, :] = padrows(xbc_s[:, jb * LANES:(jb + 1) * LANES]).T
    for s in range(NS):
        da_b = jnp.exp(jnp.broadcast_to(tot_t[:, s * L:s * L + 1], (LANES, LANES)))
        for g in range(SSD_GROUPS):
            b_p = padrows(xbc_s[:, SSD_INNER + g * SSD_STATE:SSD_INNER + (g + 1) * SSD_STATE])
            if NS > 1:
                b_p = jnp.where(rowseq_p == s, b_p, 0.0)
            pieces = []
            for r in range(SSD_HPG):
                h = g * SSD_HPG + r
                pieces.append(xst_s[h * SSD_HEAD_DIM:(h + 1) * SSD_HEAD_DIM, :] * w_t[h:h + 1, :])
            xw_t = jnp.concatenate(pieces, axis=0).astype(BF16)
            upd = jnp.dot(xw_t, b_p.astype(BF16), preferred_element_type=F32)
            for r in range(SSD_HPG):
                h = g * SSD_HPG + r
                rows = slice(h * SSD_HEAD_DIM, (h + 1) * SSD_HEAD_DIM)
                dec_h = jnp.broadcast_to(da_b[h:h + 1, :], (SSD_HEAD_DIM, SSD_STATE))
                st_ref[s, rows, :] = dec_h * st_ref[s, rows, :] + upd[r * SSD_HEAD_DIM:(r + 1) * SSD_HEAD_DIM, :]

    for g in range(SSD_GROUPS):
        cols = slice(g * GROUP_COLS, (g + 1) * GROUP_COLS)
        z = a_ref[:, cols].astype(F32)
        ug = y_s[:, cols] * (z * _sigmoid(z))
        ms = jnp.mean(ug * ug, axis=-1, keepdims=True)
        u_ref[:, cols] = (ug * lax.rsqrt(ms + EPS) * nw_ref[:, cols]).astype(BF16)


def _seqmix(pa, ps, pdt, hist, schist, st0, prm, *, nb, nc, q, ns, pad_rows, init_bcast):
    L = q // ns
    cw, cb, scw, dtb, alog, drow, nw = prm

    def tok(b, c):
        return (b * nc + c, 0)

    def init3(b, c):
        return (0 if init_bcast else b, 0, 0)

    def const2(b, c):
        return (0, 0)

    def per_b(b, c):
        return (b, 0, 0)

    n_seq = nb * ns
    shifts = _shift_matrices(q, ns, SSD_CONV)
    return pl.pallas_call(
        functools.partial(_seqmix_kernel, q, ns, pad_rows),
        grid=(nb, nc),
        in_specs=[
            pl.BlockSpec((q, A_COLS), tok),
            pl.BlockSpec((q, S_COLS), tok),
            pl.BlockSpec((q, LANES), tok),
            pl.BlockSpec((ns, SUBLANES, SSD_CONV_DIM), init3),
            pl.BlockSpec((ns, SUBLANES, SC_WIDTH), init3),
            pl.BlockSpec((ns, SSD_INNER, SSD_STATE), init3),
            pl.BlockSpec(shifts.shape, lambda b, c: (0, 0, 0)),
            pl.BlockSpec((SSD_CONV, SSD_CONV_DIM), const2),
            pl.BlockSpec((1, SSD_CONV_DIM), const2),
            pl.BlockSpec((SC_CONV, SC_WIDTH), const2),
            pl.BlockSpec((1, LANES), const2),
            pl.BlockSpec((1, LANES), const2),
            pl.BlockSpec((1, SSD_INNER), const2),
            pl.BlockSpec((1, SSD_INNER), const2),
        ],
        out_specs=[
            pl.BlockSpec((q, SSD_INNER), tok),
            pl.BlockSpec((q, SC_WIDTH), tok),
            pl.BlockSpec((ns, SSD_INNER, SSD_STATE), per_b),
            pl.BlockSpec((ns, SSD_CONV - 1, SSD_CONV_DIM), per_b),
            pl.BlockSpec((ns, SC_CONV - 1, SC_WIDTH), per_b),
        ],
        out_shape=[
            jax.ShapeDtypeStruct((nb * nc * q, SSD_INNER), BF16),
            jax.ShapeDtypeStruct((nb * nc * q, SC_WIDTH), BF16),
            jax.ShapeDtypeStruct((n_seq, SSD_INNER, SSD_STATE), F32),
            jax.ShapeDtypeStruct((n_seq, SSD_CONV - 1, SSD_CONV_DIM), F32),
            jax.ShapeDtypeStruct((n_seq, SC_CONV - 1, SC_WIDTH), F32),
        ],
        scratch_shapes=[
            pltpu.VMEM((ns * SUBLANES, SSD_CONV_DIM), F32),
            pltpu.VMEM((ns, SUBLANES + L, SC_WIDTH), F32),
            pltpu.VMEM((q, SSD_CONV_DIM), F32),
            pltpu.VMEM((q, SSD_INNER), F32),
            pltpu.VMEM((SSD_INNER, LANES), F32),
        ],
        compiler_params=pltpu.CompilerParams(
            dimension_semantics=("arbitrary", "arbitrary"), vmem_limit_bytes=VMEM_LIMIT),
        name="seqmix",
    )(pa, ps, pdt, hist, schist, st0, shifts, cw, cb, scw, dtb, alog, drow, nw)


def _merge_kernel(u_ref, v_ref, g_ref, x_ref, wa_ref, ws_ref, wo_ref, ln2_ref, wr_ref, br_ref,
                  xg_ref, cnt_s):
    i = pl.program_id(0)

    @pl.when(i == 0)
    def _():
        cnt_s[...] = jnp.zeros_like(cnt_s)

    y_ssd = jnp.dot(u_ref[...], wa_ref[...], preferred_element_type=F32)
    y_sc = jnp.dot(v_ref[...], ws_ref[...], preferred_element_type=F32)
    g1 = g_ref[:, 0:D_MODEL].astype(F32)
    g2 = g_ref[:, D_MODEL:2 * D_MODEL].astype(F32)
    merged = (_sigmoid(g1) * y_ssd + _sigmoid(g2) * y_sc).astype(BF16)
    x1 = x_ref[...] + jnp.dot(merged, wo_ref[...], preferred_element_type=F32)
    xg_ref[:, 0:D_MODEL] = x1
    ms = jnp.mean(x1 * x1, axis=-1, keepdims=True)
    h2 = (x1 * lax.rsqrt(ms + EPS) * ln2_ref[...]).astype(BF16)

    logits = jnp.dot(h2, wr_ref[...], preferred_element_type=F32) + br_ref[...]
    lane = lax.broadcasted_iota(jnp.int32, logits.shape, 1)
    big = jnp.int32(1 << 20)
    gl = jnp.where(lane < N_EGROUPS, logits, NEG_BIG)
    gmax = jnp.max(gl, axis=-1, keepdims=True)
    g_sel = jnp.min(jnp.where(gl == gmax, lane, big), axis=-1, keepdims=True)
    gsum = jnp.sum(jnp.exp(gl - gmax), axis=-1, keepdims=True)
    g_prob = 1.0 / gsum
    lo = N_EGROUPS + EXPERTS_PER_GROUP * g_sel
    emask = jnp.logical_and(lane >= lo, lane < lo + EXPERTS_PER_GROUP)
    el = jnp.where(emask, logits, NEG_BIG)
    m1 = jnp.max(el, axis=-1, keepdims=True)
    e = jnp.where(emask, jnp.exp(el - m1), -1.0)
    i1 = jnp.min(jnp.where(e == 1.0, lane, big), axis=-1, keepdims=True)
    e_rest = jnp.where(lane == i1, -1.0, e)
    e2 = jnp.max(e_rest, axis=-1, keepdims=True)
    i2 = jnp.min(jnp.where(e_rest == e2, lane, big), axis=-1, keepdims=True)
    denom = 1.0 + e2
    w1 = g_prob / denom
    w2 = g_prob * e2 / denom
    gates = jnp.where(lane == i1, w1, 0.0) + jnp.where(lane == i2, w2, 0.0)

    tm = logits.shape[0]
    rb = min(tm, RANK_BLOCK)
    onehot = jnp.where(lane == g_sel, 1.0, 0.0).astype(BF16)
    ri = lax.broadcasted_iota(jnp.int32, (rb, rb), 0)
    ci = lax.broadcasted_iota(jnp.int32, (rb, rb), 1)
    tril01 = jnp.where(ci <= ri, 1.0, 0.0).astype(BF16)
    carry = cnt_s[...]
    incl_blocks = []
    for r0 in range(0, tm, rb):
        incl_b = jnp.dot(tril01, onehot[r0:r0 + rb], preferred_element_type=F32) + carry
        carry = incl_b[rb - 1:rb, :]
        incl_blocks.append(incl_b)
    cnt_s[...] = carry
    incl = incl_blocks[0] if len(incl_blocks) == 1 else jnp.concatenate(incl_blocks, axis=0)
    rank = jnp.sum(jnp.where(lane == g_sel, incl - 1.0, 0.0), axis=-1, keepdims=True)
    xg_ref[:, D_MODEL:D_MODEL + LANES] = jnp.where(
        lane == 0, g_sel.astype(F32), jnp.where(lane == 1, rank, gates))


def _merge(u, v, g, x, wa, ws, wo, ln2, wr, br, tm):
    t = x.shape[0]
    row = lambda i: (i, 0)
    const = lambda i: (0, 0)
    return pl.pallas_call(
        _merge_kernel,
        grid=(t // tm,),
        in_specs=[
            pl.BlockSpec((tm, SSD_INNER), row),
            pl.BlockSpec((tm, SC_WIDTH), row),
            pl.BlockSpec((tm, G_COLS), row),
            pl.BlockSpec((tm, D_MODEL), row),
            pl.BlockSpec((SSD_INNER, D_MODEL), const, pipeline_mode=pl.Buffered(1)),
            pl.BlockSpec((SC_WIDTH, D_MODEL), const, pipeline_mode=pl.Buffered(1)),
            pl.BlockSpec((D_MODEL, D_MODEL), const, pipeline_mode=pl.Buffered(1)),
            pl.BlockSpec((1, D_MODEL), const),
            pl.BlockSpec((D_MODEL, LANES), const),
            pl.BlockSpec((1, LANES), const),
        ],
        out_specs=pl.BlockSpec((tm, XG_COLS), row),
        out_shape=jax.ShapeDtypeStruct((t, XG_COLS), F32),
        scratch_shapes=[pltpu.VMEM((1, LANES), F32)],
        compiler_params=pltpu.CompilerParams(
            dimension_semantics=("arbitrary",), vmem_limit_bytes=VMEM_LIMIT),
        name="merge",
    )(u, v, g, x, wa, ws, wo, ln2, wr, br)


def _moe_kernel(tmg, inv_ref, tg_ref, nv_ref, xg_hbm, wg_ref, wu_ref, wd_ref, ln2_ref, fw_ref,
                y_hbm, xbuf, ybuf, sem_in, sem_out):
    i = pl.program_id(0)
    n = pl.num_programs(0)
    slot = i % 2

    def row_in(tok, r, slt):
        return pltpu.make_async_copy(xg_hbm.at[pl.ds(tok, 1)], xbuf.at[slt, pl.ds(r, 1)], sem_in.at[slt])

    def row_out(tok, r, slt):
        return pltpu.make_async_copy(ybuf.at[slt, pl.ds(r, 1)], y_hbm.at[pl.ds(tok, 1)], sem_out.at[slt])

    unroll = 8

    def gather(tile, slt):
        def body(r, carry):
            row_in(inv_ref[tile * tmg + r], r, slt).start()
            return carry
        lax.fori_loop(0, tmg, body, 0, unroll=unroll)

    def wait_gather(slt):
        pltpu.make_async_copy(xg_hbm.at[pl.ds(0, tmg)], xbuf.at[slt], sem_in.at[slt]).wait()

    def scatter(tile, slt):
        def body(r, carry):
            row_out(inv_ref[tile * tmg + r], r, slt).start()
            return carry
        full = nv_ref[tile] == tmg

        @pl.when(full)
        def _():
            for r in range(tmg):
                body(r, 0)

        @pl.when(jnp.logical_not(full))
        def _():
            lax.fori_loop(0, nv_ref[tile], body, 0)

    def wait_scatter(tile, slt):
        def body(r, carry):
            row_out(0, r, slt).wait()
            return carry
        full = nv_ref[tile] == tmg

        @pl.when(full)
        def _():
            pltpu.make_async_copy(ybuf.at[slt], y_hbm.at[pl.ds(0, tmg)], sem_out.at[slt]).wait()

        @pl.when(jnp.logical_not(full))
        def _():
            lax.fori_loop(0, nv_ref[tile], body, 0)

    @pl.when(i == 0)
    def _():
        gather(0, 0)

    nxt = jnp.minimum(i + 1, n - 1)

    @pl.when(i >= 2)
    def _():
        wait_scatter(jnp.maximum(i - 2, 0), slot)

    @pl.when(nv_ref[i] == 0)
    def _():
        gather(nxt, 1 - slot)
        wait_gather(slot)

    @pl.when(nv_ref[i] > 0)
    def _():
        wait_gather(slot)
        x1 = xbuf[slot, :, 0:D_MODEL]
        for r in range(tmg // 2):
            row_in(inv_ref[nxt * tmg + r], r, 1 - slot).start()
        gate = xbuf[slot, :, D_MODEL:D_MODEL + LANES]
        for r in range(tmg // 2, tmg):
            row_in(inv_ref[nxt * tmg + r], r, 1 - slot).start()
        x1_res = xbuf[slot, :, 0:D_MODEL]
        ms = jnp.mean(x1 * x1, axis=-1, keepdims=True)
        h2 = (x1 * lax.rsqrt(ms + EPS) * ln2_ref[...]).astype(BF16)
        a = jnp.dot(h2, wg_ref[...], preferred_element_type=F32)
        up = jnp.dot(h2, wu_ref[...], preferred_element_type=F32)
        act = a * _sigmoid(a) * up
        lane = lax.broadcasted_iota(jnp.int32, gate.shape, 1)
        first = N_EGROUPS + EXPERTS_PER_GROUP * tg_ref[i]
        pieces = []
        for e in range(EXPERTS_PER_GROUP):
            ge = jnp.sum(jnp.where(lane == first + e, gate, 0.0), axis=-1, keepdims=True)
            pieces.append((act[:, e * D_EXPERT:(e + 1) * D_EXPERT] * ge).astype(BF16))
        mo = jnp.dot(jnp.concatenate(pieces, axis=1), wd_ref[...], preferred_element_type=F32)
        x2 = x1_res + mo
        ms2 = jnp.mean(x2 * x2, axis=-1, keepdims=True)
        ybuf[slot] = x2 * lax.rsqrt(ms2 + EPS) * fw_ref[...]
        scatter(i, slot)

    @pl.when(i == n - 1)
    def _():
        wait_gather(1 - slot)

        @pl.when(i >= 1)
        def _():
            wait_scatter(jnp.maximum(i - 1, 0), 1 - slot)
        wait_scatter(i, slot)


def _moe(xg, inv, tile_group, n_valid, wg, wu, wd, ln2, fw, tmg):
    t = xg.shape[0]
    n_tiles = tile_group.shape[0]
    gcols = EXPERTS_PER_GROUP * D_EXPERT
    const = lambda i, inv, tg, nv: (0, 0)
    by_group = lambda i, inv, tg, nv: (tg[i], 0, 0)
    return pl.pallas_call(
        functools.partial(_moe_kernel, tmg),
        grid_spec=pltpu.PrefetchScalarGridSpec(
            num_scalar_prefetch=3,
            grid=(n_tiles,),
            in_specs=[
                pl.BlockSpec(memory_space=pltpu.HBM),
                pl.BlockSpec((None, D_MODEL, gcols), by_group),
                pl.BlockSpec((None, D_MODEL, gcols), by_group),
                pl.BlockSpec((None, gcols, D_MODEL), by_group),
                pl.BlockSpec((1, D_MODEL), const),
                pl.BlockSpec((1, D_MODEL), const),
            ],
            out_specs=pl.BlockSpec(memory_space=pltpu.HBM),
            scratch_shapes=[
                pltpu.VMEM((2, tmg, XG_COLS), F32),
                pltpu.VMEM((2, tmg, D_MODEL), F32),
                pltpu.SemaphoreType.DMA((2,)),
                pltpu.SemaphoreType.DMA((2,)),
            ],
        ),
        out_shape=jax.ShapeDtypeStruct((t, D_MODEL), F32),
        compiler_params=pltpu.CompilerParams(
            dimension_semantics=("arbitrary",), vmem_limit_bytes=VMEM_LIMIT),
        name="moe",
    )(inv, tile_group, n_valid, xg, wg, wu, wd, ln2, fw)


def _route(xg, tmg):
    t = xg.shape[0]
    g = xg[:, D_MODEL].astype(jnp.int32)
    rank = xg[:, D_MODEL + 1].astype(jnp.int32)
    counts = jnp.sum(g[:, None] == jnp.arange(N_EGROUPS, dtype=jnp.int32)[None, :], axis=0, dtype=jnp.int32)
    tiles_per = (counts + tmg - 1) // tmg
    tile_end = jnp.cumsum(tiles_per)
    tile_base = tile_end - tiles_per
    n_tiles = t // tmg + N_EGROUPS
    pos = tile_base[g] * tmg + rank
    inv = jnp.zeros((n_tiles * tmg,), jnp.int32).at[pos].set(
        jnp.arange(t, dtype=jnp.int32), unique_indices=True, mode="promise_in_bounds")
    ti = jnp.arange(n_tiles, dtype=jnp.int32)
    tg = jnp.minimum(jnp.sum(ti[:, None] >= tile_end[None, :], axis=1, dtype=jnp.int32), N_EGROUPS - 1)
    nv = jnp.clip(counts[tg] - (ti - tile_base[tg]) * tmg, 0, tmg)
    nv = jnp.where(ti < tile_end[-1], nv, 0).astype(jnp.int32)
    return inv, tg, nv


def _pad_lanes(v, n=LANES):
    v = v.reshape(1, -1).astype(F32)
    return jnp.pad(v, ((0, 0), (0, n - v.shape[1])))


def _pad_hist(h):
    return jnp.pad(h, ((0, 0), (SUBLANES - h.shape[1], 0), (0, 0)))


def kernel(x_prompt, x_sample, state_ssm, state_ssd_conv, state_short_conv, meta_tokens, ln1_w, w_in,
           ssd_conv_w, ssd_conv_b, ssd_dt_bias, ssd_A_log, ssd_D, ssd_norm_w, w_ssd_out, sc_conv_w,
           w_sc_out, w_o, ln2_w, w_rg, b_rg, w_re, b_re, w_gate, w_up, w_down, final_norm_w):
    assert ln1_w.shape[0] == 1, "single-layer trunk"
    bp, sp, _ = x_prompt.shape
    bs, ss, _ = x_sample.shape
    l = 0

    a0, a1 = 0, A_COLS
    d0, d1 = A_COLS, A_COLS + SSD_HEADS
    wi = w_in[l]
    w_a = wi[:, a0:a1].astype(BF16)
    w_sg = wi[:, d1:].astype(BF16)
    wdt = jnp.pad(wi[:, d0:d1], ((0, 0), (0, LANES - SSD_HEADS))).astype(BF16)
    lnw = ln1_w[l].reshape(1, D_MODEL)
    prm = (ssd_conv_w[l], ssd_conv_b[l].reshape(1, -1), sc_conv_w[l],
           _pad_lanes(ssd_dt_bias[l]), _pad_lanes(ssd_A_log[l]),
           jnp.repeat(ssd_D[l].astype(F32), SSD_HEAD_DIM).reshape(1, SSD_INNER),
           ssd_norm_w[l].reshape(1, SSD_INNER))
    wa = w_ssd_out[l].astype(BF16)
    ws = w_sc_out[l].astype(BF16)
    wo = w_o[l].astype(BF16)
    ln2 = ln2_w[l].reshape(1, D_MODEL)
    wr = jnp.pad(jnp.concatenate([w_rg[l], w_re[l]], axis=1),
                 ((0, 0), (0, LANES - N_EGROUPS - N_EXPERTS))).astype(BF16)
    br = _pad_lanes(jnp.concatenate([b_rg[l], b_re[l]]))
    gcols = EXPERTS_PER_GROUP * D_EXPERT
    wg = jnp.transpose(w_gate[l], (0, 2, 1, 3)).reshape(N_EGROUPS, D_MODEL, gcols).astype(BF16)
    wu = jnp.transpose(w_up[l], (0, 2, 1, 3)).reshape(N_EGROUPS, D_MODEL, gcols).astype(BF16)
    wd = w_down[l].reshape(N_EGROUPS, gcols, D_MODEL).astype(BF16)
    fw = final_norm_w.reshape(1, D_MODEL)

    def mlp_tail(u, v, g, x):
        xg = _merge(u, v, g, x, wa, ws, wo, ln2, wr, br, tm=1024)
        inv, tg, nv = _route(xg, MOE_TILE)
        return _moe(xg, inv, tg, nv, wg, wu, wd, ln2, fw, MOE_TILE)

    q = LANES
    xm = jnp.pad(meta_tokens.astype(F32), ((q - N_META, 0), (0, 0)))
    ma, msc, _, mdt = _inproj(xm, lnw, w_a, w_sg, wdt, tm=q)
    zeros_hist = jnp.zeros((1, SUBLANES, SSD_CONV_DIM), F32)
    zeros_sch = jnp.zeros((1, SUBLANES, SC_WIDTH), F32)
    zeros_st = jnp.zeros((1, SSD_INNER, SSD_STATE), F32)
    _, _, m_st, m_cs, m_scs = _seqmix(ma, msc, mdt, zeros_hist, zeros_sch, zeros_st, prm,
                                      nb=1, nc=1, q=q, ns=1, pad_rows=q - N_META, init_bcast=True)

    xp = x_prompt.reshape(bp * sp, D_MODEL)
    pa, psc, pg, pdt = _inproj(xp, lnw, w_a, w_sg, wdt, tm=2048)
    pu, pv, p_st, p_cs, p_scs = _seqmix(pa, psc, pdt, _pad_hist(m_cs), _pad_hist(m_scs), m_st, prm,
                                        nb=bp, nc=sp // q, q=q, ns=1, pad_rows=0, init_bcast=True)
    y_prompt = mlp_tail(pu, pv, pg, xp).reshape(bp, sp, D_MODEL)

    ns = 8
    xs = x_sample.reshape(bs * ss, D_MODEL)
    sa, ssc, sg, sdt = _inproj(xs, lnw, w_a, w_sg, wdt, tm=1024)
    su, sv, s_st, s_cs, s_scs = _seqmix(sa, ssc, sdt, _pad_hist(state_ssd_conv[l]),
                                        _pad_hist(state_short_conv[l]),
                                        state_ssm[l].reshape(bs, SSD_INNER, SSD_STATE), prm,
                                        nb=bs // ns, nc=1, q=ns * ss, ns=ns, pad_rows=0, init_bcast=False)
    y_sample = mlp_tail(su, sv, sg, xs).reshape(bs, ss, D_MODEL)

    hshape = (SSD_HEADS, SSD_HEAD_DIM, SSD_STATE)
    return (y_prompt, y_sample,
            p_st.reshape(1, bp, *hshape), p_cs[None], p_scs[None],
            s_st.reshape(1, bs, *hshape), s_cs[None], s_scs[None])
```

```python
import functools

import jax
import jax.numpy as jnp
import numpy as np
from jax import lax
from jax.experimental import pallas as pl
from jax.experimental.pallas import tpu as pltpu

F32 = jnp.float32
BF16 = jnp.bfloat16

D_MODEL = 1024
N_META = 16
SSD_INNER = 2048
SSD_HEAD_DIM = 64
SSD_HEADS = 32
SSD_GROUPS = 4
SSD_HPG = 8
SSD_STATE = 128
SSD_CONV = 4
SSD_CONV_DIM = 3072
SC_WIDTH = 1024
SC_CONV = 3
N_EGROUPS = 4
EXPERTS_PER_GROUP = 8
N_EXPERTS = 32
D_EXPERT = 256
EPS = 1e-6

LANES = 128
SUBLANES = 8
GROUP_COLS = SSD_HPG * SSD_HEAD_DIM
A_COLS = SSD_INNER + SSD_CONV_DIM
S_COLS = 3 * SC_WIDTH
G_COLS = 2 * D_MODEL
XG_COLS = D_MODEL + LANES
MOE_TILE = 256
MOE_TILE_SMALL = 128
RANK_BLOCK = 512
NEG_BIG = -1e30
LOG2E = 1.4426950408889634
VMEM_LIMIT = 56 * 1024 * 1024


def _nt_dot(a, b):
    return lax.dot_general(a, b, (((1,), (1,)), ((), ())), preferred_element_type=F32)


def _dot01(m01_bf16, x):
    hi = x.astype(BF16)
    r1 = x - hi.astype(F32)
    mid = r1.astype(BF16)
    lo = (r1 - mid.astype(F32)).astype(BF16)
    out = jnp.dot(m01_bf16, hi, preferred_element_type=F32)
    out = out + jnp.dot(m01_bf16, mid, preferred_element_type=F32)
    return out + jnp.dot(m01_bf16, lo, preferred_element_type=F32)


def _softplus(x):
    return jnp.maximum(x, 0.0) + jnp.log1p(jnp.exp(-jnp.abs(x)))


def _sigmoid(x):
    return 1.0 / (1.0 + jnp.exp(-x))


def _inproj_kernel(nA, nS, x_ref, lnw_ref, wa_ref, wsg_ref, wdt_ref, oa_ref, os_ref, og_ref, odt_ref, h_s):
    j = pl.program_id(1)

    @pl.when(j == 0)
    def _():
        x = x_ref[...]
        ms = jnp.mean(x * x, axis=-1, keepdims=True)
        h = (x * lax.rsqrt(ms + EPS) * lnw_ref[...]).astype(BF16)
        h_s[...] = h
        odt_ref[...] = jnp.dot(h, wdt_ref[...], preferred_element_type=F32)

    def proj(w_ref):
        return jnp.dot(h_s[...], w_ref[...].astype(BF16), preferred_element_type=F32).astype(BF16)

    @pl.when(j < nA)
    def _():
        oa_ref[...] = proj(wa_ref)

    @pl.when(jnp.logical_and(j >= nA, j < nA + nS))
    def _():
        os_ref[...] = proj(wsg_ref)

    @pl.when(j >= nA + nS)
    def _():
        og_ref[...] = proj(wsg_ref)


def _inproj(x, lnw, wa, wsg, wdt, tm, tn=512):
    t = x.shape[0]
    nA, nS, nG = A_COLS // tn, S_COLS // tn, G_COLS // tn
    nj = nA + nS + nG
    return pl.pallas_call(
        functools.partial(_inproj_kernel, nA, nS),
        grid=(t // tm, nj),
        in_specs=[
            pl.BlockSpec((tm, D_MODEL), lambda i, j: (i, 0)),
            pl.BlockSpec((1, D_MODEL), lambda i, j: (0, 0)),
            pl.BlockSpec((D_MODEL, tn), lambda i, j: (0, jnp.minimum(j, nA - 1))),
            pl.BlockSpec((D_MODEL, tn), lambda i, j: (0, jnp.clip(j - nA, 0, nS + nG - 1))),
            pl.BlockSpec((D_MODEL, LANES), lambda i, j: (0, 0)),
        ],
        out_specs=[
            pl.BlockSpec((tm, tn), lambda i, j: (i, jnp.minimum(j, nA - 1))),
            pl.BlockSpec((tm, tn), lambda i, j: (i, jnp.clip(j - nA, 0, nS - 1))),
            pl.BlockSpec((tm, tn), lambda i, j: (i, jnp.clip(j - nA - nS, 0, nG - 1))),
            pl.BlockSpec((tm, LANES), lambda i, j: (i, 0)),
        ],
        out_shape=[
            jax.ShapeDtypeStruct((t, A_COLS), BF16),
            jax.ShapeDtypeStruct((t, S_COLS), BF16),
            jax.ShapeDtypeStruct((t, G_COLS), BF16),
            jax.ShapeDtypeStruct((t, LANES), F32),
        ],
        scratch_shapes=[pltpu.VMEM((tm, D_MODEL), BF16)],
        compiler_params=pltpu.CompilerParams(
            dimension_semantics=("arbitrary", "arbitrary"), vmem_limit_bytes=VMEM_LIMIT),
        name="inproj",
    )(x, lnw, wa, wsg, wdt)


def halo_rows(ns):
    return max(2 * SUBLANES, ns * SUBLANES)


def _shift_matrices(q, ns, taps):
    L = q // ns
    hb = halo_rows(ns)
    r = np.arange(q)[:, None]
    col = np.arange(3 * hb + q)[None, :]
    seq, t = r // L, r % L
    mats = []
    for k in range(taps - 1):
        d = taps - 1 - k
        from_cur = (col >= 3 * hb) & (col - 3 * hb == r - d) & (t >= d)
        from_hist = (col < 3 * hb) & (col % hb == seq * SUBLANES + SUBLANES - d + t) & (t < d)
        mats.append(from_cur | from_hist)
    return jnp.asarray(np.stack(mats), dtype=BF16)


def _seqmix_kernel(Q, NS, pad_rows,
                   a_ref, s_ref, dt_ref, hist_ref, schist_ref, st0_ref, shift_ref,
                   cw_ref, cb_ref, scw_ref, dtb_ref, alog_ref, drow_ref, nw_ref,
                   u_ref, v_ref, st_ref, cs_ref, scs_ref,
                   halo, scpad, xbc_s, y_s, xst_s):
    L = Q // NS
    lg = L.bit_length() - 1
    c = pl.program_id(1)

    @pl.when(c == 0)
    def _():
        for s in range(NS):
            halo[s * SUBLANES:(s + 1) * SUBLANES, :] = hist_ref[s]
        scpad[:, 0:SUBLANES, :] = schist_ref[...]
        st_ref[...] = st0_ref[...]

    CW = 256
    hb = halo_rows(NS)
    for cc in range(0, SSD_CONV_DIM, CW):
        raw_b = a_ref[:, SSD_INNER + cc:SSD_INNER + cc + CW]
        hl = halo[:, cc:cc + CW]
        if hb > hl.shape[0]:
            hl = jnp.concatenate([hl, jnp.zeros((hb - hl.shape[0], CW), F32)], axis=0)
        h_hi = hl.astype(BF16)
        h_r = hl - h_hi.astype(F32)
        h_mid = h_r.astype(BF16)
        h_lo = (h_r - h_mid.astype(F32)).astype(BF16)
        ext = jnp.concatenate([h_hi, h_mid, h_lo, raw_b], axis=0)
        raw = raw_b.astype(F32)
        acc = raw * cw_ref[3:4, cc:cc + CW] + cb_ref[:, cc:cc + CW]
        for k in range(SSD_CONV - 1):
            acc = acc + jnp.dot(shift_ref[k], ext, preferred_element_type=F32) * cw_ref[k:k + 1, cc:cc + CW]
        xbc_s[:, cc:cc + CW] = acc * _sigmoid(acc)
        for s in range(NS):
            halo[s * SUBLANES:(s + 1) * SUBLANES, cc:cc + CW] = raw[(s + 1) * L - SUBLANES:(s + 1) * L]
    for s in range(NS):
        cs_ref[s] = halo[(s + 1) * SUBLANES - (SSD_CONV - 1):(s + 1) * SUBLANES, :]

    for cc in range(0, SC_WIDTH, CW):
        scb = s_ref[:, cc:cc + CW].astype(F32)
        ch_all = (s_ref[:, SC_WIDTH + cc:SC_WIDTH + cc + CW].astype(F32)
                  * s_ref[:, 2 * SC_WIDTH + cc:2 * SC_WIDTH + cc + CW].astype(F32))
        for s in range(NS):
            ch = ch_all[s * L:(s + 1) * L]
            scpad[s, SUBLANES:SUBLANES + L, cc:cc + CW] = ch
            acc = ch * scw_ref[2:3, cc:cc + CW]
            for k in range(SC_CONV - 1):
                acc = acc + scpad[s, 6 + k:6 + k + L, cc:cc + CW] * scw_ref[k:k + 1, cc:cc + CW]
            v_ref[s * L:(s + 1) * L, cc:cc + CW] = (scb[s * L:(s + 1) * L] * acc).astype(BF16)
    for s in range(NS):
        scs_ref[s] = scpad[s, L + 6:L + 8, :]
        scpad[s, 0:SUBLANES, :] = scpad[s, L:L + SUBLANES, :]

    def padrows(x):
        if Q == LANES:
            return x
        return jnp.concatenate([x, jnp.zeros((LANES - Q, x.shape[1]), x.dtype)], axis=0)

    li = lax.broadcasted_iota(jnp.int32, (Q, Q), 0)
    si = lax.broadcasted_iota(jnp.int32, (Q, Q), 1)
    same = (li >> lg) == (si >> lg)
    causal = jnp.logical_and(same, si <= li)
    tril01 = jnp.where(causal, 1.0, 0.0).astype(BF16)
    same01 = jnp.where(same, 1.0, 0.0).astype(BF16)

    dt = _softplus(dt_ref[...] + dtb_ref[...])
    if pad_rows:
        ri = lax.broadcasted_iota(jnp.int32, (Q, LANES), 0)
        dt = jnp.where(ri >= pad_rows, dt, 0.0)
    da = dt * (-jnp.exp(alog_ref[...]))
    acum = _dot01(tril01, da)
    tot = _dot01(same01, da)
    acum_t = padrows(acum).T
    tot_t = padrows(tot).T
    dt_t = padrows(dt).T
    w_t = jnp.exp(tot_t - acum_t) * dt_t
    acum2 = acum * LOG2E
    rowq_t = acum_t * LOG2E - jnp.log2(dt_t)

    left_head = lax.broadcasted_iota(jnp.int32, (Q, LANES), 1) < SSD_HEAD_DIM
    rowseq = lax.broadcasted_iota(jnp.int32, (Q, GROUP_COLS), 0) >> lg
    rowseq_p = lax.broadcasted_iota(jnp.int32, (LANES, LANES), 0) >> lg

    for g in range(SSD_GROUPS):
        b_g = xbc_s[:, SSD_INNER + g * SSD_STATE:SSD_INNER + (g + 1) * SSD_STATE]
        c_g = xbc_s[:, SSD_INNER + GROUP_COLS + g * SSD_STATE:SSD_INNER + GROUP_COLS + (g + 1) * SSD_STATE]
        b_gb = b_g.astype(BF16)
        c_gb = c_g.astype(BF16)
        cbm = _nt_dot(c_gb, b_gb)
        yo = None
        for s in range(NS):
            h_s = st_ref[s, g * GROUP_COLS:(g + 1) * GROUP_COLS, :].astype(BF16)
            yo_s = _nt_dot(c_gb, h_s)
            yo = yo_s if yo is None else jnp.where(rowseq == s, yo_s, yo)
        for rp in range(SSD_HPG // 2):
            h0 = g * SSD_HPG + 2 * rp
            cols = slice(h0 * SSD_HEAD_DIM, (h0 + 2) * SSD_HEAD_DIM)
            colbs, w_pair = [], []
            for h in (h0, h0 + 1):
                colb = jnp.broadcast_to(acum2[:, h:h + 1], (Q, LANES))
                rowb = jnp.broadcast_to(rowq_t[h:h + 1, 0:Q], (Q, Q))
                dec_dt = jnp.exp2(jnp.where(causal, colb[:, 0:Q] - rowb, NEG_BIG))
                w_pair.append((cbm * dec_dt).astype(BF16))
                colbs.append(colb)
            x_p = xbc_s[:, cols]
            rhs = jnp.concatenate([jnp.where(left_head, x_p, 0.0).astype(BF16),
                                   jnp.where(left_head, 0.0, x_p).astype(BF16)], axis=0)
            yd = jnp.dot(jnp.concatenate(w_pair, axis=1), rhs, preferred_element_type=F32)
            ecol = jnp.exp2(jnp.where(left_head, colbs[0], colbs[1]))
            y_s[:, cols] = yd + yo[:, rp * LANES:(rp + 1) * LANES] * ecol + drow_ref[:, cols] * x_p

    for jb in range(SSD_INNER // LANES):
        xst_s[jb * LANES:(jb + 1) * LANES, :] = padrows(xbc_s[:, jb * LANES:(jb + 1) * LANES]).T
    for s in range(NS):
        da_b = jnp.exp(jnp.broadcast_to(tot_t[:, s * L:s * L + 1], (LANES, LANES)))
        for g in range(SSD_GROUPS):
            b_p = padrows(xbc_s[:, SSD_INNER + g * SSD_STATE:SSD_INNER + (g + 1) * SSD_STATE])
            if NS > 1:
                b_p = jnp.where(rowseq_p == s, b_p, 0.0)
            pieces = []
            for r in range(SSD_HPG):
                h = g * SSD_HPG + r
                pieces.append(xst_s[h * SSD_HEAD_DIM:(h + 1) * SSD_HEAD_DIM, :] * w_t[h:h + 1, :])
            xw_t = jnp.concatenate(pieces, axis=0).astype(BF16)
            upd = jnp.dot(xw_t, b_p.astype(BF16), preferred_element_type=F32)
            for r in range(SSD_HPG):
                h = g * SSD_HPG + r
                rows = slice(h * SSD_HEAD_DIM, (h + 1) * SSD_HEAD_DIM)
                dec_h = jnp.broadcast_to(da_b[h:h + 1, :], (SSD_HEAD_DIM, SSD_STATE))
                st_ref[s, rows, :] = dec_h * st_ref[s, rows, :] + upd[r * SSD_HEAD_DIM:(r + 1) * SSD_HEAD_DIM, :]

    for g in range(SSD_GROUPS):
        cols = slice(g * GROUP_COLS, (g + 1) * GROUP_COLS)
        z = a_ref[:, cols].astype(F32)
        ug = y_s[:, cols] * (z * _sigmoid(z))
        ms = jnp.mean(ug * ug, axis=-1, keepdims=True)
        u_ref[:, cols] = (ug * lax.rsqrt(ms + EPS) * nw_ref[:, cols]).astype(BF16)


def _seqmix(pa, ps, pdt, hist, schist, st0, prm, *, nb, nc, q, ns, pad_rows, init_bcast):
    L = q // ns
    cw, cb, scw, dtb, alog, drow, nw = prm

    def tok(b, c):
        return (b * nc + c, 0)

    def init3(b, c):
        return (0 if init_bcast else b, 0, 0)

    def const2(b, c):
        return (0, 0)

    def per_b(b, c):
        return (b, 0, 0)

    n_seq = nb * ns
    shifts = _shift_matrices(q, ns, SSD_CONV)
    return pl.pallas_call(
        functools.partial(_seqmix_kernel, q, ns, pad_rows),
        grid=(nb, nc),
        in_specs=[
            pl.BlockSpec((q, A_COLS), tok),
            pl.BlockSpec((q, S_COLS), tok),
            pl.BlockSpec((q, LANES), tok),
            pl.BlockSpec((ns, SUBLANES, SSD_CONV_DIM), init3),
            pl.BlockSpec((ns, SUBLANES, SC_WIDTH), init3),
            pl.BlockSpec((ns, SSD_INNER, SSD_STATE), init3),
            pl.BlockSpec(shifts.shape, lambda b, c: (0, 0, 0)),
            pl.BlockSpec((SSD_CONV, SSD_CONV_DIM), const2),
            pl.BlockSpec((1, SSD_CONV_DIM), const2),
            pl.BlockSpec((SC_CONV, SC_WIDTH), const2),
            pl.BlockSpec((1, LANES), const2),
            pl.BlockSpec((1, LANES), const2),
            pl.BlockSpec((1, SSD_INNER), const2),
            pl.BlockSpec((1, SSD_INNER), const2),
        ],
        out_specs=[
            pl.BlockSpec((q, SSD_INNER), tok),
            pl.BlockSpec((q, SC_WIDTH), tok),
            pl.BlockSpec((ns, SSD_INNER, SSD_STATE), per_b),
            pl.BlockSpec((ns, SSD_CONV - 1, SSD_CONV_DIM), per_b),
            pl.BlockSpec((ns, SC_CONV - 1, SC_WIDTH), per_b),
        ],
        out_shape=[
            jax.ShapeDtypeStruct((nb * nc * q, SSD_INNER), BF16),
            jax.ShapeDtypeStruct((nb * nc * q, SC_WIDTH), BF16),
            jax.ShapeDtypeStruct((n_seq, SSD_INNER, SSD_STATE), F32),
            jax.ShapeDtypeStruct((n_seq, SSD_CONV - 1, SSD_CONV_DIM), F32),
            jax.ShapeDtypeStruct((n_seq, SC_CONV - 1, SC_WIDTH), F32),
        ],
        scratch_shapes=[
            pltpu.VMEM((ns * SUBLANES, SSD_CONV_DIM), F32),
            pltpu.VMEM((ns, SUBLANES + L, SC_WIDTH), F32),
            pltpu.VMEM((q, SSD_CONV_DIM), F32),
            pltpu.VMEM((q, SSD_INNER), F32),
            pltpu.VMEM((SSD_INNER, LANES), F32),
        ],
        compiler_params=pltpu.CompilerParams(
            dimension_semantics=("arbitrary", "arbitrary"), vmem_limit_bytes=VMEM_LIMIT),
        name="seqmix",
    )(pa, ps, pdt, hist, schist, st0, shifts, cw, cb, scw, dtb, alog, drow, nw)


def _merge_kernel(u_ref, v_ref, g_ref, x_ref, wa_ref, ws_ref, wo_ref, ln2_ref, wr_ref, br_ref,
                  xg_ref, cnt_s):
    i = pl.program_id(0)

    @pl.when(i == 0)
    def _():
        cnt_s[...] = jnp.zeros_like(cnt_s)

    y_ssd = jnp.dot(u_ref[...], wa_ref[...], preferred_element_type=F32)
    y_sc = jnp.dot(v_ref[...], ws_ref[...], preferred_element_type=F32)
    g1 = g_ref[:, 0:D_MODEL].astype(F32)
    g2 = g_ref[:, D_MODEL:2 * D_MODEL].astype(F32)
    merged = (_sigmoid(g1) * y_ssd + _sigmoid(g2) * y_sc).astype(BF16)
    x1 = x_ref[...] + jnp.dot(merged, wo_ref[...], preferred_element_type=F32)
    xg_ref[:, 0:D_MODEL] = x1
    ms = jnp.mean(x1 * x1, axis=-1, keepdims=True)
    h2 = (x1 * lax.rsqrt(ms + EPS) * ln2_ref[...]).astype(BF16)

    logits = jnp.dot(h2, wr_ref[...], preferred_element_type=F32) + br_ref[...]
    lane = lax.broadcasted_iota(jnp.int32, logits.shape, 1)
    big = jnp.int32(1 << 20)
    gl = jnp.where(lane < N_EGROUPS, logits, NEG_BIG)
    gmax = jnp.max(gl, axis=-1, keepdims=True)
    g_sel = jnp.min(jnp.where(gl == gmax, lane, big), axis=-1, keepdims=True)
    gsum = jnp.sum(jnp.exp(gl - gmax), axis=-1, keepdims=True)
    g_prob = 1.0 / gsum
    lo = N_EGROUPS + EXPERTS_PER_GROUP * g_sel
    emask = jnp.logical_and(lane >= lo, lane < lo + EXPERTS_PER_GROUP)
    el = jnp.where(emask, logits, NEG_BIG)
    m1 = jnp.max(el, axis=-1, keepdims=True)
    e = jnp.where(emask, jnp.exp(el - m1), -1.0)
    i1 = jnp.min(jnp.where(e == 1.0, lane, big), axis=-1, keepdims=True)
    e_rest = jnp.where(lane == i1, -1.0, e)
    e2 = jnp.max(e_rest, axis=-1, keepdims=True)
    i2 = jnp.min(jnp.where(e_rest == e2, lane, big), axis=-1, keepdims=True)
    denom = 1.0 + e2
    w1 = g_prob / denom
    w2 = g_prob * e2 / denom
    gates = jnp.where(lane == i1, w1, 0.0) + jnp.where(lane == i2, w2, 0.0)

    tm = logits.shape[0]
    rb = min(tm, RANK_BLOCK)
    onehot = jnp.where(lane == g_sel, 1.0, 0.0).astype(BF16)
    ri = lax.broadcasted_iota(jnp.int32, (rb, rb), 0)
    ci = lax.broadcasted_iota(jnp.int32, (rb, rb), 1)
    tril01 = jnp.where(ci <= ri, 1.0, 0.0).astype(BF16)
    carry = cnt_s[...]
    incl_blocks = []
    for r0 in range(0, tm, rb):
        incl_b = jnp.dot(tril01, onehot[r0:r0 + rb], preferred_element_type=F32) + carry
        carry = incl_b[rb - 1:rb, :]
        incl_blocks.append(incl_b)
    cnt_s[...] = carry
    incl = incl_blocks[0] if len(incl_blocks) == 1 else jnp.concatenate(incl_blocks, axis=0)
    rank = jnp.sum(jnp.where(lane == g_sel, incl - 1.0, 0.0), axis=-1, keepdims=True)
    xg_ref[:, D_MODEL:D_MODEL + LANES] = jnp.where(
        lane == 0, g_sel.astype(F32), jnp.where(lane == 1, rank, gates))


def _merge(u, v, g, x, wa, ws, wo, ln2, wr, br, tm):
    t = x.shape[0]
    row = lambda i: (i, 0)
    const = lambda i: (0, 0)
    return pl.pallas_call(
        _merge_kernel,
        grid=(t // tm,),
        in_specs=[
            pl.BlockSpec((tm, SSD_INNER), row),
            pl.BlockSpec((tm, SC_WIDTH), row),
            pl.BlockSpec((tm, G_COLS), row),
            pl.BlockSpec((tm, D_MODEL), row),
            pl.BlockSpec((SSD_INNER, D_MODEL), const, pipeline_mode=pl.Buffered(1)),
            pl.BlockSpec((SC_WIDTH, D_MODEL), const, pipeline_mode=pl.Buffered(1)),
            pl.BlockSpec((D_MODEL, D_MODEL), const, pipeline_mode=pl.Buffered(1)),
            pl.BlockSpec((1, D_MODEL), const),
            pl.BlockSpec((D_MODEL, LANES), const),
            pl.BlockSpec((1, LANES), const),
        ],
        out_specs=pl.BlockSpec((tm, XG_COLS), row),
        out_shape=jax.ShapeDtypeStruct((t, XG_COLS), F32),
        scratch_shapes=[pltpu.VMEM((1, LANES), F32)],
        compiler_params=pltpu.CompilerParams(
            dimension_semantics=("arbitrary",), vmem_limit_bytes=VMEM_LIMIT),
        name="merge",
    )(u, v, g, x, wa, ws, wo, ln2, wr, br)


def _moe_kernel(tmg, inv_ref, tg_ref, nv_ref, xg_hbm, wg_ref, wu_ref, wd_ref, ln2_ref, fw_ref,
                y_hbm, xbuf, ybuf, sem_in, sem_out):
    i = pl.program_id(0)
    n = pl.num_programs(0)
    slot = i % 2

    def row_in(tok, r, slt):
        return pltpu.make_async_copy(xg_hbm.at[pl.ds(tok, 1)], xbuf.at[slt, pl.ds(r, 1)], sem_in.at[slt])

    def row_out(tok, r, slt):
        return pltpu.make_async_copy(ybuf.at[slt, pl.ds(r, 1)], y_hbm.at[pl.ds(tok, 1)], sem_out.at[slt])

    unroll = 8

    def gather(tile, slt):
        def body(r, carry):
            row_in(inv_ref[tile * tmg + r], r, slt).start()
            return carry
        lax.fori_loop(0, tmg, body, 0, unroll=unroll)

    def wait_gather(slt):
        pltpu.make_async_copy(xg_hbm.at[pl.ds(0, tmg)], xbuf.at[slt], sem_in.at[slt]).wait()

    def scatter(tile, slt):
        def body(r, carry):
            row_out(inv_ref[tile * tmg + r], r, slt).start()
            return carry
        full = nv_ref[tile] == tmg

        @pl.when(full)
        def _():
            for r in range(tmg):
                body(r, 0)

        @pl.when(jnp.logical_not(full))
        def _():
            lax.fori_loop(0, nv_ref[tile], body, 0)

    def wait_scatter(tile, slt):
        def body(r, carry):
            row_out(0, r, slt).wait()
            return carry
        full = nv_ref[tile] == tmg

        @pl.when(full)
        def _():
            pltpu.make_async_copy(ybuf.at[slt], y_hbm.at[pl.ds(0, tmg)], sem_out.at[slt]).wait()

        @pl.when(jnp.logical_not(full))
        def _():
            lax.fori_loop(0, nv_ref[tile], body, 0)

    @pl.when(i == 0)
    def _():
        gather(0, 0)

    nxt = jnp.minimum(i + 1, n - 1)

    @pl.when(i >= 2)
    def _():
        wait_scatter(jnp.maximum(i - 2, 0), slot)

    @pl.when(nv_ref[i] == 0)
    def _():
        gather(nxt, 1 - slot)
        wait_gather(slot)

    @pl.when(nv_ref[i] > 0)
    def _():
        wait_gather(slot)
        x1 = xbuf[slot, :, 0:D_MODEL]
        for r in range(tmg // 2):
            row_in(inv_ref[nxt * tmg + r], r, 1 - slot).start()
        gate = xbuf[slot, :, D_MODEL:D_MODEL + LANES]
        for r in range(tmg // 2, tmg):
            row_in(inv_ref[nxt * tmg + r], r, 1 - slot).start()
        x1_res = xbuf[slot, :, 0:D_MODEL]
        ms = jnp.mean(x1 * x1, axis=-1, keepdims=True)
        h2 = (x1 * lax.rsqrt(ms + EPS) * ln2_ref[...]).astype(BF16)
        lane = lax.broadcasted_iota(jnp.int32, gate.shape, 1)
        first = N_EGROUPS + EXPERTS_PER_GROUP * tg_ref[i]
        pieces = []
        for e in range(EXPERTS_PER_GROUP):
            a = jnp.dot(h2, wg_ref[e], preferred_element_type=F32)
            up = jnp.dot(h2, wu_ref[e], preferred_element_type=F32)
            ge = jnp.sum(jnp.where(lane == first + e, gate, 0.0), axis=-1, keepdims=True)
            pieces.append((a * _sigmoid(a) * up * ge).astype(BF16))
        mo = jnp.dot(jnp.concatenate(pieces, axis=1), wd_ref[...], preferred_element_type=F32)
        x2 = x1_res + mo
        ms2 = jnp.mean(x2 * x2, axis=-1, keepdims=True)
        ybuf[slot] = x2 * lax.rsqrt(ms2 + EPS) * fw_ref[...]
        scatter(i, slot)

    @pl.when(i == n - 1)
    def _():
        wait_gather(1 - slot)

        @pl.when(i >= 1)
        def _():
            wait_scatter(jnp.maximum(i - 1, 0), 1 - slot)
        wait_scatter(i, slot)


def _moe(xg, inv, tile_group, n_valid, wg, wu, wd, ln2, fw, tmg):
    t = xg.shape[0]
    n_tiles = tile_group.shape[0]
    gcols = EXPERTS_PER_GROUP * D_EXPERT
    const = lambda i, inv, tg, nv: (0, 0)
    by_group = lambda i, inv, tg, nv: (tg[i], 0, 0)
    return pl.pallas_call(
        functools.partial(_moe_kernel, tmg),
        grid_spec=pltpu.PrefetchScalarGridSpec(
            num_scalar_prefetch=3,
            grid=(n_tiles,),
            in_specs=[
                pl.BlockSpec(memory_space=pltpu.HBM),
                pl.BlockSpec((EXPERTS_PER_GROUP, D_MODEL, D_EXPERT), by_group),
                pl.BlockSpec((EXPERTS_PER_GROUP, D_MODEL, D_EXPERT), by_group),
                pl.BlockSpec((None, gcols, D_MODEL), by_group),
                pl.BlockSpec((1, D_MODEL), const),
                pl.BlockSpec((1, D_MODEL), const),
            ],
            out_specs=pl.BlockSpec(memory_space=pltpu.HBM),
            scratch_shapes=[
                pltpu.VMEM((2, tmg, XG_COLS), F32),
                pltpu.VMEM((2, tmg, D_MODEL), F32),
                pltpu.SemaphoreType.DMA((2,)),
                pltpu.SemaphoreType.DMA((2,)),
            ],
        ),
        out_shape=jax.ShapeDtypeStruct((t, D_MODEL), F32),
        compiler_params=pltpu.CompilerParams(
            dimension_semantics=("arbitrary",), vmem_limit_bytes=VMEM_LIMIT),
        name="moe",
    )(inv, tile_group, n_valid, xg, wg, wu, wd, ln2, fw)


def _route(xg, tmg):
    t = xg.shape[0]
    g = xg[:, D_MODEL].astype(jnp.int32)
    rank = xg[:, D_MODEL + 1].astype(jnp.int32)
    counts = jnp.sum(g[:, None] == jnp.arange(N_EGROUPS, dtype=jnp.int32)[None, :], axis=0, dtype=jnp.int32)
    tiles_per = (counts + tmg - 1) // tmg
    tile_end = jnp.cumsum(tiles_per)
    tile_base = tile_end - tiles_per
    n_tiles = t // tmg + N_EGROUPS
    pos = tile_base[g] * tmg + rank
    inv = jnp.zeros((n_tiles * tmg,), jnp.int32).at[pos].set(
        jnp.arange(t, dtype=jnp.int32), unique_indices=True, mode="promise_in_bounds")
    ti = jnp.arange(n_tiles, dtype=jnp.int32)
    tg = jnp.minimum(jnp.sum(ti[:, None] >= tile_end[None, :], axis=1, dtype=jnp.int32), N_EGROUPS - 1)
    nv = jnp.clip(counts[tg] - (ti - tile_base[tg]) * tmg, 0, tmg)
    nv = jnp.where(ti < tile_end[-1], nv, 0).astype(jnp.int32)
    return inv, tg, nv


def _pad_lanes(v, n=LANES):
    v = v.reshape(1, -1).astype(F32)
    return jnp.pad(v, ((0, 0), (0, n - v.shape[1])))


def _pad_hist(h):
    return jnp.pad(h, ((0, 0), (SUBLANES - h.shape[1], 0), (0, 0)))


def kernel(x_prompt, x_sample, state_ssm, state_ssd_conv, state_short_conv, meta_tokens, ln1_w, w_in,
           ssd_conv_w, ssd_conv_b, ssd_dt_bias, ssd_A_log, ssd_D, ssd_norm_w, w_ssd_out, sc_conv_w,
           w_sc_out, w_o, ln2_w, w_rg, b_rg, w_re, b_re, w_gate, w_up, w_down, final_norm_w):
    assert ln1_w.shape[0] == 1, "single-layer trunk"
    bp, sp, _ = x_prompt.shape
    bs, ss, _ = x_sample.shape
    l = 0

    a0, a1 = 0, A_COLS
    d0, d1 = A_COLS, A_COLS + SSD_HEADS
    wi = w_in[l]
    w_a = wi
    w_sg = wi[:, d1:]
    wdt = jnp.pad(wi[:, d0:d1], ((0, 0), (0, LANES - SSD_HEADS))).astype(BF16)
    lnw = ln1_w[l].reshape(1, D_MODEL)
    prm = (ssd_conv_w[l], ssd_conv_b[l].reshape(1, -1), sc_conv_w[l],
           _pad_lanes(ssd_dt_bias[l]), _pad_lanes(ssd_A_log[l]),
           jnp.repeat(ssd_D[l].astype(F32), SSD_HEAD_DIM).reshape(1, SSD_INNER),
           ssd_norm_w[l].reshape(1, SSD_INNER))
    wa = w_ssd_out[l].astype(BF16)
    ws = w_sc_out[l].astype(BF16)
    wo = w_o[l].astype(BF16)
    ln2 = ln2_w[l].reshape(1, D_MODEL)
    wr = jnp.pad(jnp.concatenate([w_rg[l], w_re[l]], axis=1),
                 ((0, 0), (0, LANES - N_EGROUPS - N_EXPERTS))).astype(BF16)
    br = _pad_lanes(jnp.concatenate([b_rg[l], b_re[l]]))
    gcols = EXPERTS_PER_GROUP * D_EXPERT
    wg = w_gate[l].reshape(N_EXPERTS, D_MODEL, D_EXPERT).astype(BF16)
    wu = w_up[l].reshape(N_EXPERTS, D_MODEL, D_EXPERT).astype(BF16)
    wd = w_down[l].reshape(N_EGROUPS, gcols, D_MODEL).astype(BF16)
    fw = final_norm_w.reshape(1, D_MODEL)

    def mlp_tail(u, v, g, x, tmg):
        xg = _merge(u, v, g, x, wa, ws, wo, ln2, wr, br, tm=1024)
        inv, tg, nv = _route(xg, tmg)
        return _moe(xg, inv, tg, nv, wg, wu, wd, ln2, fw, tmg)

    q = LANES
    xm = jnp.pad(meta_tokens.astype(F32), ((q - N_META, 0), (0, 0)))
    ma, msc, _, mdt = _inproj(xm, lnw, w_a, w_sg, wdt, tm=q)
    zeros_hist = jnp.zeros((1, SUBLANES, SSD_CONV_DIM), F32)
    zeros_sch = jnp.zeros((1, SUBLANES, SC_WIDTH), F32)
    zeros_st = jnp.zeros((1, SSD_INNER, SSD_STATE), F32)
    _, _, m_st, m_cs, m_scs = _seqmix(ma, msc, mdt, zeros_hist, zeros_sch, zeros_st, prm,
                                      nb=1, nc=1, q=q, ns=1, pad_rows=q - N_META, init_bcast=True)

    xp = x_prompt.reshape(bp * sp, D_MODEL)
    pa, psc, pg, pdt = _inproj(xp, lnw, w_a, w_sg, wdt, tm=2048)
    pu, pv, p_st, p_cs, p_scs = _seqmix(pa, psc, pdt, _pad_hist(m_cs), _pad_hist(m_scs), m_st, prm,
                                        nb=bp, nc=sp // q, q=q, ns=1, pad_rows=0, init_bcast=True)
    y_prompt = mlp_tail(pu, pv, pg, xp, MOE_TILE).reshape(bp, sp, D_MODEL)

    ns = 8
    xs = x_sample.reshape(bs * ss, D_MODEL)
    sa, ssc, sg, sdt = _inproj(xs, lnw, w_a, w_sg, wdt, tm=1024)
    su, sv, s_st, s_cs, s_scs = _seqmix(sa, ssc, sdt, _pad_hist(state_ssd_conv[l]),
                                        _pad_hist(state_short_conv[l]),
                                        state_ssm[l].reshape(bs, SSD_INNER, SSD_STATE), prm,
                                        nb=bs // ns, nc=1, q=ns * ss, ns=ns, pad_rows=0, init_bcast=False)
    y_sample = mlp_tail(su, sv, sg, xs, MOE_TILE_SMALL).reshape(bs, ss, D_MODEL)

    hshape = (SSD_HEADS, SSD_HEAD_DIM, SSD_STATE)
    return (y_prompt, y_sample,
            p_st.reshape(1, bp, *hshape), p_cs[None], p_scs[None],
            s_st.reshape(1, bs, *hshape), s_cs[None], s_scs[None])
```

```python
import functools

import jax
import jax.numpy as jnp
import numpy as np
from jax import lax
from jax.experimental import pallas as pl
from jax.experimental.pallas import tpu as pltpu

F32 = jnp.float32
BF16 = jnp.bfloat16

D_MODEL = 1024
N_META = 16
SSD_INNER = 2048
SSD_HEAD_DIM = 64
SSD_HEADS = 32
SSD_GROUPS = 4
SSD_HPG = 8
SSD_STATE = 128
SSD_CONV = 4
SSD_CONV_DIM = 3072
SC_WIDTH = 1024
SC_CONV = 3
N_EGROUPS = 4
EXPERTS_PER_GROUP = 8
N_EXPERTS = 32
D_EXPERT = 256
EPS = 1e-6

LANES = 128
SUBLANES = 8
GROUP_COLS = SSD_HPG * SSD_HEAD_DIM
A_COLS = SSD_INNER + SSD_CONV_DIM
S_COLS = 3 * SC_WIDTH
G_COLS = 2 * D_MODEL
XG_COLS = D_MODEL + LANES
MOE_TILE = 256
MOE_TILE_SMALL = 128
RANK_BLOCK = 512
NEG_BIG = -1e30
LOG2E = 1.4426950408889634
VMEM_LIMIT = 56 * 1024 * 1024


def _nt_dot(a, b):
    return lax.dot_general(a, b, (((1,), (1,)), ((), ())), preferred_element_type=F32)


def _dot01(m01_bf16, x):
    hi = x.astype(BF16)
    r1 = x - hi.astype(F32)
    mid = r1.astype(BF16)
    lo = (r1 - mid.astype(F32)).astype(BF16)
    out = jnp.dot(m01_bf16, hi, preferred_element_type=F32)
    out = out + jnp.dot(m01_bf16, mid, preferred_element_type=F32)
    return out + jnp.dot(m01_bf16, lo, preferred_element_type=F32)


def _softplus(x):
    return jnp.maximum(x, 0.0) + jnp.log1p(jnp.exp(-jnp.abs(x)))


def _sigmoid(x):
    return 1.0 / (1.0 + jnp.exp(-x))


def _inproj_kernel(nA, nS, n_sets, *refs):
    x_refs = refs[:n_sets]
    lnw_ref, wa_ref, wsg_ref, wdt_ref = refs[n_sets:n_sets + 4]
    outs = refs[n_sets + 4:n_sets + 4 + 4 * n_sets]
    h_refs = refs[n_sets + 4 + 4 * n_sets:]
    j = pl.program_id(1)

    @pl.when(j == 0)
    def _():
        for k in range(n_sets):
            x = x_refs[k][...]
            ms = jnp.mean(x * x, axis=-1, keepdims=True)
            h = (x * lax.rsqrt(ms + EPS) * lnw_ref[...]).astype(BF16)
            h_refs[k][...] = h
            outs[4 * k + 3][...] = jnp.dot(h, wdt_ref[...], preferred_element_type=F32)

    def proj(w_ref, which):
        w = w_ref[...].astype(BF16)
        for k in range(n_sets):
            outs[4 * k + which][...] = jnp.dot(h_refs[k][...], w, preferred_element_type=F32).astype(BF16)

    @pl.when(j < nA)
    def _():
        proj(wa_ref, 0)

    @pl.when(jnp.logical_and(j >= nA, j < nA + nS))
    def _():
        proj(wsg_ref, 1)

    @pl.when(j >= nA + nS)
    def _():
        proj(wsg_ref, 2)


def _inproj(x, lnw, wa, wsg, wdt, tm, tn=512, extra=None):
    t = x.shape[0]
    nA, nS, nG = A_COLS // tn, S_COLS // tn, G_COLS // tn
    nj = nA + nS + nG
    sets = [(x, tm, lambda i: i)]
    if extra is not None:
        assert t == tm, "extra rows are recomputed per row tile"
        sets.append((extra, extra.shape[0], lambda i: 0))
    x_specs, out_specs, out_shape, scratch = [], [], [], []
    for xs_, rows, ri in sets:
        x_specs.append(pl.BlockSpec((rows, D_MODEL), lambda i, j, ri=ri: (ri(i), 0)))
        out_specs += [
            pl.BlockSpec((rows, tn), lambda i, j, ri=ri: (ri(i), jnp.minimum(j, nA - 1))),
            pl.BlockSpec((rows, tn), lambda i, j, ri=ri: (ri(i), jnp.clip(j - nA, 0, nS - 1))),
            pl.BlockSpec((rows, tn), lambda i, j, ri=ri: (ri(i), jnp.clip(j - nA - nS, 0, nG - 1))),
            pl.BlockSpec((rows, LANES), lambda i, j, ri=ri: (ri(i), 0)),
        ]
        n_rows = xs_.shape[0]
        out_shape += [
            jax.ShapeDtypeStruct((n_rows, A_COLS), BF16),
            jax.ShapeDtypeStruct((n_rows, S_COLS), BF16),
            jax.ShapeDtypeStruct((n_rows, G_COLS), BF16),
            jax.ShapeDtypeStruct((n_rows, LANES), F32),
        ]
        scratch.append(pltpu.VMEM((rows, D_MODEL), BF16))
    res = pl.pallas_call(
        functools.partial(_inproj_kernel, nA, nS, len(sets)),
        grid=(t // tm, nj),
        in_specs=x_specs + [
            pl.BlockSpec((1, D_MODEL), lambda i, j: (0, 0)),
            pl.BlockSpec((D_MODEL, tn), lambda i, j: (0, jnp.minimum(j, nA - 1))),
            pl.BlockSpec((D_MODEL, tn), lambda i, j: (0, jnp.clip(j - nA, 0, nS + nG - 1))),
            pl.BlockSpec((D_MODEL, LANES), lambda i, j: (0, 0)),
        ],
        out_specs=out_specs,
        out_shape=out_shape,
        scratch_shapes=scratch,
        compiler_params=pltpu.CompilerParams(
            dimension_semantics=("arbitrary", "arbitrary"), vmem_limit_bytes=VMEM_LIMIT),
        name="inproj",
    )(*[st[0] for st in sets], lnw, wa, wsg, wdt)
    return (res[:4], res[4:]) if extra is not None else res


def halo_rows(ns):
    return max(2 * SUBLANES, ns * SUBLANES)


def _shift_matrices(q, ns, taps):
    L = q // ns
    hb = halo_rows(ns)
    r = np.arange(q)[:, None]
    col = np.arange(3 * hb + q)[None, :]
    seq, t = r // L, r % L
    mats = []
    for k in range(taps - 1):
        d = taps - 1 - k
        from_cur = (col >= 3 * hb) & (col - 3 * hb == r - d) & (t >= d)
        from_hist = (col < 3 * hb) & (col % hb == seq * SUBLANES + SUBLANES - d + t) & (t < d)
        mats.append(from_cur | from_hist)
    return jnp.asarray(np.stack(mats), dtype=BF16)


def _seqmix_kernel(Q, NS, pad_rows,
                   a_ref, s_ref, dt_ref, hist_ref, schist_ref, st0_ref, shift_ref,
                   cw_ref, cb_ref, scw_ref, dtb_ref, alog_ref, drow_ref, nw_ref,
                   u_ref, v_ref, st_ref, cs_ref, scs_ref,
                   halo, scpad, xbc_s, y_s, xst_s):
    L = Q // NS
    lg = L.bit_length() - 1
    c = pl.program_id(1)

    @pl.when(c == 0)
    def _():
        halo[...] = jnp.zeros_like(halo)
        for s in range(NS):
            halo[(s + 1) * SUBLANES - (SSD_CONV - 1):(s + 1) * SUBLANES, :] = hist_ref[s]
            scpad[s, SUBLANES - (SC_CONV - 1):SUBLANES, :] = schist_ref[s]
        st_ref[...] = st0_ref[...]

    CW = 256
    hb = halo_rows(NS)
    for cc in range(0, SSD_CONV_DIM, CW):
        raw_b = a_ref[:, SSD_INNER + cc:SSD_INNER + cc + CW]
        hl = halo[:, cc:cc + CW]
        if hb > hl.shape[0]:
            hl = jnp.concatenate([hl, jnp.zeros((hb - hl.shape[0], CW), F32)], axis=0)
        h_hi = hl.astype(BF16)
        h_r = hl - h_hi.astype(F32)
        h_mid = h_r.astype(BF16)
        h_lo = (h_r - h_mid.astype(F32)).astype(BF16)
        ext = jnp.concatenate([h_hi, h_mid, h_lo, raw_b], axis=0)
        raw = raw_b.astype(F32)
        acc = raw * cw_ref[3:4, cc:cc + CW] + cb_ref[:, cc:cc + CW]
        for k in range(SSD_CONV - 1):
            acc = acc + jnp.dot(shift_ref[k], ext, preferred_element_type=F32) * cw_ref[k:k + 1, cc:cc + CW]
        xbc_s[:, cc:cc + CW] = acc * _sigmoid(acc)
        for s in range(NS):
            halo[s * SUBLANES:(s + 1) * SUBLANES, cc:cc + CW] = raw[(s + 1) * L - SUBLANES:(s + 1) * L]
    for s in range(NS):
        cs_ref[s] = halo[(s + 1) * SUBLANES - (SSD_CONV - 1):(s + 1) * SUBLANES, :]

    for cc in range(0, SC_WIDTH, CW):
        scb = s_ref[:, cc:cc + CW].astype(F32)
        ch_all = (s_ref[:, SC_WIDTH + cc:SC_WIDTH + cc + CW].astype(F32)
                  * s_ref[:, 2 * SC_WIDTH + cc:2 * SC_WIDTH + cc + CW].astype(F32))
        for s in range(NS):
            ch = ch_all[s * L:(s + 1) * L]
            scpad[s, SUBLANES:SUBLANES + L, cc:cc + CW] = ch
            acc = ch * scw_ref[2:3, cc:cc + CW]
            for k in range(SC_CONV - 1):
                acc = acc + scpad[s, 6 + k:6 + k + L, cc:cc + CW] * scw_ref[k:k + 1, cc:cc + CW]
            v_ref[s * L:(s + 1) * L, cc:cc + CW] = (scb[s * L:(s + 1) * L] * acc).astype(BF16)
    for s in range(NS):
        scs_ref[s] = scpad[s, L + 6:L + 8, :]
        scpad[s, 0:SUBLANES, :] = scpad[s, L:L + SUBLANES, :]

    def padrows(x):
        if Q == LANES:
            return x
        return jnp.concatenate([x, jnp.zeros((LANES - Q, x.shape[1]), x.dtype)], axis=0)

    li = lax.broadcasted_iota(jnp.int32, (Q, Q), 0)
    si = lax.broadcasted_iota(jnp.int32, (Q, Q), 1)
    same = (li >> lg) == (si >> lg)
    causal = jnp.logical_and(same, si <= li)
    tril01 = jnp.where(causal, 1.0, 0.0).astype(BF16)
    same01 = jnp.where(same, 1.0, 0.0).astype(BF16)

    dt = _softplus(dt_ref[...] + dtb_ref[...])
    if pad_rows:
        ri = lax.broadcasted_iota(jnp.int32, (Q, LANES), 0)
        dt = jnp.where(ri >= pad_rows, dt, 0.0)
    da = dt * (-jnp.exp(alog_ref[...]))
    acum = _dot01(tril01, da)
    tot = _dot01(same01, da)
    acum_t = padrows(acum).T
    tot_t = padrows(tot).T
    dt_t = padrows(dt).T
    w_t = jnp.exp(tot_t - acum_t) * dt_t
    acum2 = acum * LOG2E
    rowq_t = acum_t * LOG2E - jnp.log2(dt_t)

    left_head = lax.broadcasted_iota(jnp.int32, (Q, LANES), 1) < SSD_HEAD_DIM
    rowseq = lax.broadcasted_iota(jnp.int32, (Q, GROUP_COLS), 0) >> lg
    rowseq_p = lax.broadcasted_iota(jnp.int32, (LANES, LANES), 0) >> lg

    for g in range(SSD_GROUPS):
        b_g = xbc_s[:, SSD_INNER + g * SSD_STATE:SSD_INNER + (g + 1) * SSD_STATE]
        c_g = xbc_s[:, SSD_INNER + GROUP_COLS + g * SSD_STATE:SSD_INNER + GROUP_COLS + (g + 1) * SSD_STATE]
        b_gb = b_g.astype(BF16)
        c_gb = c_g.astype(BF16)
        cbm = _nt_dot(c_gb, b_gb)
        yo = None
        for s in range(NS):
            h_s = st_ref[s, g * GROUP_COLS:(g + 1) * GROUP_COLS, :].astype(BF16)
            yo_s = _nt_dot(c_gb, h_s)
            yo = yo_s if yo is None else jnp.where(rowseq == s, yo_s, yo)
        for rp in range(SSD_HPG // 2):
            h0 = g * SSD_HPG + 2 * rp
            cols = slice(h0 * SSD_HEAD_DIM, (h0 + 2) * SSD_HEAD_DIM)
            colbs, w_pair = [], []
            for h in (h0, h0 + 1):
                colb = jnp.broadcast_to(acum2[:, h:h + 1], (Q, LANES))
                rowb = jnp.broadcast_to(rowq_t[h:h + 1, 0:Q], (Q, Q))
                dec_dt = jnp.exp2(jnp.where(causal, colb[:, 0:Q] - rowb, NEG_BIG))
                w_pair.append((cbm * dec_dt).astype(BF16))
                colbs.append(colb)
            x_p = xbc_s[:, cols]
            rhs = jnp.concatenate([jnp.where(left_head, x_p, 0.0).astype(BF16),
                                   jnp.where(left_head, 0.0, x_p).astype(BF16)], axis=0)
            yd = jnp.dot(jnp.concatenate(w_pair, axis=1), rhs, preferred_element_type=F32)
            ecol = jnp.exp2(jnp.where(left_head, colbs[0], colbs[1]))
            y_s[:, cols] = yd + yo[:, rp * LANES:(rp + 1) * LANES] * ecol + drow_ref[:, cols] * x_p

    for jb in range(SSD_INNER // LANES):
        xst_s[jb * LANES:(jb + 1) * LANES, :] = padrows(xbc_s[:, jb * LANES:(jb + 1) * LANES]).T
    for s in range(NS):
        da_b = jnp.exp(jnp.broadcast_to(tot_t[:, s * L:s * L + 1], (LANES, LANES)))
        for g in range(SSD_GROUPS):
            b_p = padrows(xbc_s[:, SSD_INNER + g * SSD_STATE:SSD_INNER + (g + 1) * SSD_STATE])
            if NS > 1:
                b_p = jnp.where(rowseq_p == s, b_p, 0.0)
            pieces = []
            for r in range(SSD_HPG):
                h = g * SSD_HPG + r
                pieces.append(xst_s[h * SSD_HEAD_DIM:(h + 1) * SSD_HEAD_DIM, :] * w_t[h:h + 1, :])
            xw_t = jnp.concatenate(pieces, axis=0).astype(BF16)
            upd = jnp.dot(xw_t, b_p.astype(BF16), preferred_element_type=F32)
            for r in range(SSD_HPG):
                h = g * SSD_HPG + r
                rows = slice(h * SSD_HEAD_DIM, (h + 1) * SSD_HEAD_DIM)
                dec_h = jnp.broadcast_to(da_b[h:h + 1, :], (SSD_HEAD_DIM, SSD_STATE))
                st_ref[s, rows, :] = dec_h * st_ref[s, rows, :] + upd[r * SSD_HEAD_DIM:(r + 1) * SSD_HEAD_DIM, :]

    for g in range(SSD_GROUPS):
        cols = slice(g * GROUP_COLS, (g + 1) * GROUP_COLS)
        z = a_ref[:, cols].astype(F32)
        ug = y_s[:, cols] * (z * _sigmoid(z))
        ms = jnp.mean(ug * ug, axis=-1, keepdims=True)
        u_ref[:, cols] = (ug * lax.rsqrt(ms + EPS) * nw_ref[:, cols]).astype(BF16)


def _seqmix(pa, ps, pdt, hist, schist, st0, prm, *, nb, nc, q, ns, pad_rows, init_bcast):
    L = q // ns
    cw, cb, scw, dtb, alog, drow, nw = prm

    def tok(b, c):
        return (b * nc + c, 0)

    def init3(b, c):
        return (0 if init_bcast else b, 0, 0)

    def const2(b, c):
        return (0, 0)

    def per_b(b, c):
        return (b, 0, 0)

    n_seq = nb * ns
    shifts = _shift_matrices(q, ns, SSD_CONV)
    return pl.pallas_call(
        functools.partial(_seqmix_kernel, q, ns, pad_rows),
        grid=(nb, nc),
        in_specs=[
            pl.BlockSpec((q, A_COLS), tok),
            pl.BlockSpec((q, S_COLS), tok),
            pl.BlockSpec((q, LANES), tok),
            pl.BlockSpec((ns, SSD_CONV - 1, SSD_CONV_DIM), init3),
            pl.BlockSpec((ns, SC_CONV - 1, SC_WIDTH), init3),
            pl.BlockSpec((ns, SSD_INNER, SSD_STATE), init3),
            pl.BlockSpec(shifts.shape, lambda b, c: (0, 0, 0)),
            pl.BlockSpec((SSD_CONV, SSD_CONV_DIM), const2),
            pl.BlockSpec((1, SSD_CONV_DIM), const2),
            pl.BlockSpec((SC_CONV, SC_WIDTH), const2),
            pl.BlockSpec((1, LANES), const2),
            pl.BlockSpec((1, LANES), const2),
            pl.BlockSpec((1, SSD_INNER), const2),
            pl.BlockSpec((1, SSD_INNER), const2),
        ],
        out_specs=[
            pl.BlockSpec((q, SSD_INNER), tok),
            pl.BlockSpec((q, SC_WIDTH), tok),
            pl.BlockSpec((ns, SSD_INNER, SSD_STATE), per_b),
            pl.BlockSpec((ns, SSD_CONV - 1, SSD_CONV_DIM), per_b),
            pl.BlockSpec((ns, SC_CONV - 1, SC_WIDTH), per_b),
        ],
        out_shape=[
            jax.ShapeDtypeStruct((nb * nc * q, SSD_INNER), BF16),
            jax.ShapeDtypeStruct((nb * nc * q, SC_WIDTH), BF16),
            jax.ShapeDtypeStruct((n_seq, SSD_INNER, SSD_STATE), F32),
            jax.ShapeDtypeStruct((n_seq, SSD_CONV - 1, SSD_CONV_DIM), F32),
            jax.ShapeDtypeStruct((n_seq, SC_CONV - 1, SC_WIDTH), F32),
        ],
        scratch_shapes=[
            pltpu.VMEM((ns * SUBLANES, SSD_CONV_DIM), F32),
            pltpu.VMEM((ns, SUBLANES + L, SC_WIDTH), F32),
            pltpu.VMEM((q, SSD_CONV_DIM), F32),
            pltpu.VMEM((q, SSD_INNER), F32),
            pltpu.VMEM((SSD_INNER, LANES), F32),
        ],
        compiler_params=pltpu.CompilerParams(
            dimension_semantics=("arbitrary", "arbitrary"), vmem_limit_bytes=VMEM_LIMIT),
        name="seqmix",
    )(pa, ps, pdt, hist, schist, st0, shifts, cw, cb, scw, dtb, alog, drow, nw)


def _merge_kernel(u_ref, v_ref, g_ref, x_ref, wa_ref, ws_ref, wo_ref, ln2_ref, wr_ref, br_ref,
                  xg_ref, cnt_s):
    i = pl.program_id(0)

    @pl.when(i == 0)
    def _():
        cnt_s[...] = jnp.zeros_like(cnt_s)

    y_ssd = jnp.dot(u_ref[...], wa_ref[...], preferred_element_type=F32)
    y_sc = jnp.dot(v_ref[...], ws_ref[...], preferred_element_type=F32)
    g1 = g_ref[:, 0:D_MODEL].astype(F32)
    g2 = g_ref[:, D_MODEL:2 * D_MODEL].astype(F32)
    merged = (_sigmoid(g1) * y_ssd + _sigmoid(g2) * y_sc).astype(BF16)
    x1 = x_ref[...] + jnp.dot(merged, wo_ref[...], preferred_element_type=F32)
    xg_ref[:, 0:D_MODEL] = x1
    ms = jnp.mean(x1 * x1, axis=-1, keepdims=True)
    h2 = (x1 * lax.rsqrt(ms + EPS) * ln2_ref[...]).astype(BF16)

    logits = jnp.dot(h2, wr_ref[...], preferred_element_type=F32) + br_ref[...]
    lane = lax.broadcasted_iota(jnp.int32, logits.shape, 1)
    big = jnp.int32(1 << 20)
    gl = jnp.where(lane < N_EGROUPS, logits, NEG_BIG)
    gmax = jnp.max(gl, axis=-1, keepdims=True)
    g_sel = jnp.min(jnp.where(gl == gmax, lane, big), axis=-1, keepdims=True)
    gsum = jnp.sum(jnp.exp(gl - gmax), axis=-1, keepdims=True)
    g_prob = 1.0 / gsum
    lo = N_EGROUPS + EXPERTS_PER_GROUP * g_sel
    emask = jnp.logical_and(lane >= lo, lane < lo + EXPERTS_PER_GROUP)
    el = jnp.where(emask, logits, NEG_BIG)
    m1 = jnp.max(el, axis=-1, keepdims=True)
    e = jnp.where(emask, jnp.exp(el - m1), -1.0)
    i1 = jnp.min(jnp.where(e == 1.0, lane, big), axis=-1, keepdims=True)
    e_rest = jnp.where(lane == i1, -1.0, e)
    e2 = jnp.max(e_rest, axis=-1, keepdims=True)
    i2 = jnp.min(jnp.where(e_rest == e2, lane, big), axis=-1, keepdims=True)
    denom = 1.0 + e2
    w1 = g_prob / denom
    w2 = g_prob * e2 / denom
    gates = jnp.where(lane == i1, w1, 0.0) + jnp.where(lane == i2, w2, 0.0)

    tm = logits.shape[0]
    rb = min(tm, RANK_BLOCK)
    onehot = jnp.where(lane == g_sel, 1.0, 0.0).astype(BF16)
    ri = lax.broadcasted_iota(jnp.int32, (rb, rb), 0)
    ci = lax.broadcasted_iota(jnp.int32, (rb, rb), 1)
    tril01 = jnp.where(ci <= ri, 1.0, 0.0).astype(BF16)
    carry = cnt_s[...]
    incl_blocks = []
    for r0 in range(0, tm, rb):
        incl_b = jnp.dot(tril01, onehot[r0:r0 + rb], preferred_element_type=F32) + carry
        carry = incl_b[rb - 1:rb, :]
        incl_blocks.append(incl_b)
    cnt_s[...] = carry
    incl = incl_blocks[0] if len(incl_blocks) == 1 else jnp.concatenate(incl_blocks, axis=0)
    rank = jnp.sum(jnp.where(lane == g_sel, incl - 1.0, 0.0), axis=-1, keepdims=True)
    xg_ref[:, D_MODEL:D_MODEL + LANES] = jnp.where(
        lane == 0, g_sel.astype(F32), jnp.where(lane == 1, rank, gates))


def _merge(u, v, g, x, wa, ws, wo, ln2, wr, br, tm):
    t = x.shape[0]
    row = lambda i: (i, 0)
    const = lambda i: (0, 0)
    return pl.pallas_call(
        _merge_kernel,
        grid=(t // tm,),
        in_specs=[
            pl.BlockSpec((tm, SSD_INNER), row),
            pl.BlockSpec((tm, SC_WIDTH), row),
            pl.BlockSpec((tm, G_COLS), row),
            pl.BlockSpec((tm, D_MODEL), row),
            pl.BlockSpec((SSD_INNER, D_MODEL), const, pipeline_mode=pl.Buffered(1)),
            pl.BlockSpec((SC_WIDTH, D_MODEL), const, pipeline_mode=pl.Buffered(1)),
            pl.BlockSpec((D_MODEL, D_MODEL), const, pipeline_mode=pl.Buffered(1)),
            pl.BlockSpec((1, D_MODEL), const),
            pl.BlockSpec((D_MODEL, LANES), const),
            pl.BlockSpec((1, LANES), const),
        ],
        out_specs=pl.BlockSpec((tm, XG_COLS), row),
        out_shape=jax.ShapeDtypeStruct((t, XG_COLS), F32),
        scratch_shapes=[pltpu.VMEM((1, LANES), F32)],
        compiler_params=pltpu.CompilerParams(
            dimension_semantics=("arbitrary",), vmem_limit_bytes=VMEM_LIMIT),
        name="merge",
    )(u, v, g, x, wa, ws, wo, ln2, wr, br)


def _moe_kernel(tmg, inv_ref, tg_ref, nv_ref, xg_hbm, wg_ref, wu_ref, wd_ref, ln2_ref, fw_ref,
                y_hbm, xbuf, ybuf, sem_in, sem_out):
    i = pl.program_id(0)
    n = pl.num_programs(0)
    slot = i % 2

    def row_in(tok, r, slt):
        return pltpu.make_async_copy(xg_hbm.at[pl.ds(tok, 1)], xbuf.at[slt, pl.ds(r, 1)], sem_in.at[slt])

    def row_out(tok, r, slt):
        return pltpu.make_async_copy(ybuf.at[slt, pl.ds(r, 1)], y_hbm.at[pl.ds(tok, 1)], sem_out.at[slt])

    unroll = 8

    def gather(tile, slt):
        def body(r, carry):
            row_in(inv_ref[tile * tmg + r], r, slt).start()
            return carry
        lax.fori_loop(0, tmg, body, 0, unroll=unroll)

    def wait_gather(slt):
        pltpu.make_async_copy(xg_hbm.at[pl.ds(0, tmg)], xbuf.at[slt], sem_in.at[slt]).wait()

    def scatter(tile, slt):
        def body(r, carry):
            row_out(inv_ref[tile * tmg + r], r, slt).start()
            return carry
        full = nv_ref[tile] == tmg

        @pl.when(full)
        def _():
            for r in range(tmg):
                body(r, 0)

        @pl.when(jnp.logical_not(full))
        def _():
            lax.fori_loop(0, nv_ref[tile], body, 0)

    def wait_scatter(tile, slt):
        def body(r, carry):
            row_out(0, r, slt).wait()
            return carry
        full = nv_ref[tile] == tmg

        @pl.when(full)
        def _():
            pltpu.make_async_copy(ybuf.at[slt], y_hbm.at[pl.ds(0, tmg)], sem_out.at[slt]).wait()

        @pl.when(jnp.logical_not(full))
        def _():
            lax.fori_loop(0, nv_ref[tile], body, 0)

    @pl.when(i == 0)
    def _():
        gather(0, 0)

    nxt = jnp.minimum(i + 1, n - 1)

    @pl.when(i >= 2)
    def _():
        wait_scatter(jnp.maximum(i - 2, 0), slot)

    @pl.when(nv_ref[i] == 0)
    def _():
        gather(nxt, 1 - slot)
        wait_gather(slot)

    @pl.when(nv_ref[i] > 0)
    def _():
        wait_gather(slot)
        x1 = xbuf[slot, :, 0:D_MODEL]
        for r in range(tmg // 2):
            row_in(inv_ref[nxt * tmg + r], r, 1 - slot).start()
        gate = xbuf[slot, :, D_MODEL:D_MODEL + LANES]
        for r in range(tmg // 2, tmg):
            row_in(inv_ref[nxt * tmg + r], r, 1 - slot).start()
        x1_res = xbuf[slot, :, 0:D_MODEL]
        ms = jnp.mean(x1 * x1, axis=-1, keepdims=True)
        h2 = (x1 * lax.rsqrt(ms + EPS) * ln2_ref[...]).astype(BF16)
        lane = lax.broadcasted_iota(jnp.int32, gate.shape, 1)
        first = N_EGROUPS + EXPERTS_PER_GROUP * tg_ref[i]
        pieces = []
        for e in range(EXPERTS_PER_GROUP):
            a = jnp.dot(h2, wg_ref[e], preferred_element_type=F32)
            up = jnp.dot(h2, wu_ref[e], preferred_element_type=F32)
            ge = jnp.sum(jnp.where(lane == first + e, gate, 0.0), axis=-1, keepdims=True)
            pieces.append((a * _sigmoid(a) * up * ge).astype(BF16))
        mo = jnp.dot(jnp.concatenate(pieces, axis=1), wd_ref[...], preferred_element_type=F32)
        x2 = x1_res + mo
        ms2 = jnp.mean(x2 * x2, axis=-1, keepdims=True)
        ybuf[slot] = x2 * lax.rsqrt(ms2 + EPS) * fw_ref[...]
        scatter(i, slot)

    @pl.when(i == n - 1)
    def _():
        wait_gather(1 - slot)

        @pl.when(i >= 1)
        def _():
            wait_scatter(jnp.maximum(i - 1, 0), 1 - slot)
        wait_scatter(i, slot)


def _moe(xg, inv, tile_group, n_valid, wg, wu, wd, ln2, fw, tmg):
    t = xg.shape[0]
    n_tiles = tile_group.shape[0]
    gcols = EXPERTS_PER_GROUP * D_EXPERT
    const = lambda i, inv, tg, nv: (0, 0)
    by_group = lambda i, inv, tg, nv: (tg[i], 0, 0)
    return pl.pallas_call(
        functools.partial(_moe_kernel, tmg),
        grid_spec=pltpu.PrefetchScalarGridSpec(
            num_scalar_prefetch=3,
            grid=(n_tiles,),
            in_specs=[
                pl.BlockSpec(memory_space=pltpu.HBM),
                pl.BlockSpec((EXPERTS_PER_GROUP, D_MODEL, D_EXPERT), by_group),
                pl.BlockSpec((EXPERTS_PER_GROUP, D_MODEL, D_EXPERT), by_group),
                pl.BlockSpec((None, gcols, D_MODEL), by_group),
                pl.BlockSpec((1, D_MODEL), const),
                pl.BlockSpec((1, D_MODEL), const),
            ],
            out_specs=pl.BlockSpec(memory_space=pltpu.HBM),
            scratch_shapes=[
                pltpu.VMEM((2, tmg, XG_COLS), F32),
                pltpu.VMEM((2, tmg, D_MODEL), F32),
                pltpu.SemaphoreType.DMA((2,)),
                pltpu.SemaphoreType.DMA((2,)),
            ],
        ),
        out_shape=jax.ShapeDtypeStruct((t, D_MODEL), F32),
        compiler_params=pltpu.CompilerParams(
            dimension_semantics=("arbitrary",), vmem_limit_bytes=VMEM_LIMIT),
        name="moe",
    )(inv, tile_group, n_valid, xg, wg, wu, wd, ln2, fw)


def _route(xg, tmg):
    t = xg.shape[0]
    g = xg[:, D_MODEL].astype(jnp.int32)
    rank = xg[:, D_MODEL + 1].astype(jnp.int32)
    counts = jnp.sum(g[:, None] == jnp.arange(N_EGROUPS, dtype=jnp.int32)[None, :], axis=0, dtype=jnp.int32)
    tiles_per = (counts + tmg - 1) // tmg
    tile_end = jnp.cumsum(tiles_per)
    tile_base = tile_end - tiles_per
    n_tiles = t // tmg + N_EGROUPS
    pos = tile_base[g] * tmg + rank
    inv = jnp.zeros((n_tiles * tmg,), jnp.int32).at[pos].set(
        jnp.arange(t, dtype=jnp.int32), unique_indices=True, mode="promise_in_bounds")
    ti = jnp.arange(n_tiles, dtype=jnp.int32)
    tg = jnp.minimum(jnp.sum(ti[:, None] >= tile_end[None, :], axis=1, dtype=jnp.int32), N_EGROUPS - 1)
    nv = jnp.clip(counts[tg] - (ti - tile_base[tg]) * tmg, 0, tmg)
    nv = jnp.where(ti < tile_end[-1], nv, 0).astype(jnp.int32)
    return inv, tg, nv


def _pad_lanes(v, n=LANES):
    v = v.reshape(1, -1).astype(F32)
    return jnp.pad(v, ((0, 0), (0, n - v.shape[1])))


def _cast_kernel(a_ref, b_ref, c_ref, oa_ref, ob_ref, oc_ref):
    oa_ref[...] = a_ref[...].astype(BF16)
    ob_ref[...] = b_ref[...].astype(BF16)
    oc_ref[...] = c_ref[...].astype(BF16)


def _cast_expert_weights(wg, wu, wd, eb=4):
    n = wg.shape[0]
    specs = [pl.BlockSpec((eb,) + w.shape[1:], lambda i: (i, 0, 0)) for w in (wg, wu, wd)]
    return pl.pallas_call(
        _cast_kernel,
        grid=(n // eb,),
        in_specs=specs,
        out_specs=specs,
        out_shape=[jax.ShapeDtypeStruct(w.shape, BF16) for w in (wg, wu, wd)],
        compiler_params=pltpu.CompilerParams(
            dimension_semantics=("arbitrary",), vmem_limit_bytes=VMEM_LIMIT),
        name="castw",
    )(wg, wu, wd)


def kernel(x_prompt, x_sample, state_ssm, state_ssd_conv, state_short_conv, meta_tokens, ln1_w, w_in,
           ssd_conv_w, ssd_conv_b, ssd_dt_bias, ssd_A_log, ssd_D, ssd_norm_w, w_ssd_out, sc_conv_w,
           w_sc_out, w_o, ln2_w, w_rg, b_rg, w_re, b_re, w_gate, w_up, w_down, final_norm_w):
    assert ln1_w.shape[0] == 1, "single-layer trunk"
    bp, sp, _ = x_prompt.shape
    bs, ss, _ = x_sample.shape
    l = 0

    a0, a1 = 0, A_COLS
    d0, d1 = A_COLS, A_COLS + SSD_HEADS
    wi = w_in[l]
    w_a = wi
    w_sg = wi[:, d1:]
    wdt = jnp.pad(wi[:, d0:d1], ((0, 0), (0, LANES - SSD_HEADS))).astype(BF16)
    lnw = ln1_w[l].reshape(1, D_MODEL)
    prm = (ssd_conv_w[l], ssd_conv_b[l].reshape(1, -1), sc_conv_w[l],
           _pad_lanes(ssd_dt_bias[l]), _pad_lanes(ssd_A_log[l]),
           jnp.repeat(ssd_D[l].astype(F32), SSD_HEAD_DIM).reshape(1, SSD_INNER),
           ssd_norm_w[l].reshape(1, SSD_INNER))
    wa = w_ssd_out[l].astype(BF16)
    ws = w_sc_out[l].astype(BF16)
    wo = w_o[l].astype(BF16)
    ln2 = ln2_w[l].reshape(1, D_MODEL)
    wr = jnp.pad(jnp.concatenate([w_rg[l], w_re[l]], axis=1),
                 ((0, 0), (0, LANES - N_EGROUPS - N_EXPERTS))).astype(BF16)
    br = _pad_lanes(jnp.concatenate([b_rg[l], b_re[l]]))
    gcols = EXPERTS_PER_GROUP * D_EXPERT
    wg, wu, wd = _cast_expert_weights(w_gate[l].reshape(N_EXPERTS, D_MODEL, D_EXPERT),
                                      w_up[l].reshape(N_EXPERTS, D_MODEL, D_EXPERT),
                                      w_down[l].reshape(N_EXPERTS, D_EXPERT, D_MODEL))
    wd = wd.reshape(N_EGROUPS, gcols, D_MODEL)
    fw = final_norm_w.reshape(1, D_MODEL)

    def mlp_tail(u, v, g, x, tmg):
        xg = _merge(u, v, g, x, wa, ws, wo, ln2, wr, br, tm=1024)
        inv, tg, nv = _route(xg, tmg)
        return _moe(xg, inv, tg, nv, wg, wu, wd, ln2, fw, tmg)

    q = LANES
    xm = jnp.pad(meta_tokens.astype(F32), ((q - N_META, 0), (0, 0)))
    xs = x_sample.reshape(bs * ss, D_MODEL)
    (sa, ssc, sg, sdt), (ma, msc, _, mdt) = _inproj(xs, lnw, w_a, w_sg, wdt, tm=bs * ss, extra=xm)
    zeros_hist = jnp.zeros((1, SSD_CONV - 1, SSD_CONV_DIM), F32)
    zeros_sch = jnp.zeros((1, SC_CONV - 1, SC_WIDTH), F32)
    zeros_st = jnp.zeros((1, SSD_INNER, SSD_STATE), F32)
    _, _, m_st, m_cs, m_scs = _seqmix(ma, msc, mdt, zeros_hist, zeros_sch, zeros_st, prm,
                                      nb=1, nc=1, q=q, ns=1, pad_rows=q - N_META, init_bcast=True)

    xp = x_prompt.reshape(bp * sp, D_MODEL)
    pa, psc, pg, pdt = _inproj(xp, lnw, w_a, w_sg, wdt, tm=2048)
    pu, pv, p_st, p_cs, p_scs = _seqmix(pa, psc, pdt, m_cs, m_scs, m_st, prm,
                                        nb=bp, nc=sp // q, q=q, ns=1, pad_rows=0, init_bcast=True)
    y_prompt = mlp_tail(pu, pv, pg, xp, MOE_TILE).reshape(bp, sp, D_MODEL)

    ns = 8
    su, sv, s_st, s_cs, s_scs = _seqmix(sa, ssc, sdt, state_ssd_conv[l], state_short_conv[l],
                                        state_ssm[l].reshape(bs, SSD_INNER, SSD_STATE), prm,
                                        nb=bs // ns, nc=1, q=ns * ss, ns=ns, pad_rows=0, init_bcast=False)
    y_sample = mlp_tail(su, sv, sg, xs, MOE_TILE_SMALL).reshape(bs, ss, D_MODEL)

    hshape = (SSD_HEADS, SSD_HEAD_DIM, SSD_STATE)
    return (y_prompt, y_sample,
            p_st.reshape(1, bp, *hshape), p_cs[None], p_scs[None],
            s_st.reshape(1, bs, *hshape), s_cs[None], s_scs[None])
```

```python
import functools

import jax
import jax.numpy as jnp
import numpy as np
from jax import lax
from jax.experimental import pallas as pl
from jax.experimental.pallas import tpu as pltpu

F32 = jnp.float32
BF16 = jnp.bfloat16

D_MODEL = 1024
N_META = 16
SSD_INNER = 2048
SSD_HEAD_DIM = 64
SSD_HEADS = 32
SSD_GROUPS = 4
SSD_HPG = 8
SSD_STATE = 128
SSD_CONV = 4
SSD_CONV_DIM = 3072
SC_WIDTH = 1024
SC_CONV = 3
N_EGROUPS = 4
EXPERTS_PER_GROUP = 8
N_EXPERTS = 32
D_EXPERT = 256
EPS = 1e-6

LANES = 128
SUBLANES = 8
GROUP_COLS = SSD_HPG * SSD_HEAD_DIM
A_COLS = SSD_INNER + SSD_CONV_DIM
S_COLS = 3 * SC_WIDTH
G_COLS = 2 * D_MODEL
XG_COLS = D_MODEL + LANES
MOE_TILE = 256
MOE_TILE_SMALL = 128
RANK_BLOCK = 512
NEG_BIG = -1e30
LOG2E = 1.4426950408889634
VMEM_LIMIT = 56 * 1024 * 1024


def _nt_dot(a, b):
    return lax.dot_general(a, b, (((1,), (1,)), ((), ())), preferred_element_type=F32)


def _dot01(m01_bf16, x):
    hi = x.astype(BF16)
    r1 = x - hi.astype(F32)
    mid = r1.astype(BF16)
    lo = (r1 - mid.astype(F32)).astype(BF16)
    out = jnp.dot(m01_bf16, hi, preferred_element_type=F32)
    out = out + jnp.dot(m01_bf16, mid, preferred_element_type=F32)
    return out + jnp.dot(m01_bf16, lo, preferred_element_type=F32)


def _softplus(x):
    return jnp.maximum(x, 0.0) + jnp.log1p(jnp.exp(-jnp.abs(x)))


def _sigmoid(x):
    return 1.0 / (1.0 + jnp.exp(-x))


def _inproj_kernel(nA, nS, n_sets, *refs):
    x_refs = refs[:n_sets]
    lnw_ref, wa_ref, wsg_ref, wdt_ref = refs[n_sets:n_sets + 4]
    outs = refs[n_sets + 4:n_sets + 4 + 4 * n_sets]
    h_refs = refs[n_sets + 4 + 4 * n_sets:]
    j = pl.program_id(1)

    @pl.when(j == 0)
    def _():
        for k in range(n_sets):
            x = x_refs[k][...]
            ms = jnp.mean(x * x, axis=-1, keepdims=True)
            h = (x * lax.rsqrt(ms + EPS) * lnw_ref[...]).astype(BF16)
            h_refs[k][...] = h
            outs[4 * k + 3][...] = _nt_dot(h, wdt_ref[...])

    def proj(w_ref, which):
        w = w_ref[...].astype(BF16)
        for k in range(n_sets):
            outs[4 * k + which][...] = _nt_dot(h_refs[k][...], w).astype(BF16)

    @pl.when(j < nA)
    def _():
        proj(wa_ref, 0)

    @pl.when(jnp.logical_and(j >= nA, j < nA + nS))
    def _():
        proj(wsg_ref, 1)

    @pl.when(j >= nA + nS)
    def _():
        proj(wsg_ref, 2)


def _inproj(x, lnw, wa, wsg, wdt, tm, tn=512, extra=None):
    t = x.shape[0]
    nA, nS, nG = A_COLS // tn, S_COLS // tn, G_COLS // tn
    nj = nA + nS + nG
    sets = [(x, tm, lambda i: i)]
    if extra is not None:
        assert t == tm, "extra rows are recomputed per row tile"
        sets.append((extra, extra.shape[0], lambda i: 0))
    x_specs, out_specs, out_shape, scratch = [], [], [], []
    for xs_, rows, ri in sets:
        x_specs.append(pl.BlockSpec((rows, D_MODEL), lambda i, j, ri=ri: (ri(i), 0)))
        out_specs += [
            pl.BlockSpec((rows, tn), lambda i, j, ri=ri: (ri(i), jnp.minimum(j, nA - 1))),
            pl.BlockSpec((rows, tn), lambda i, j, ri=ri: (ri(i), jnp.clip(j - nA, 0, nS - 1))),
            pl.BlockSpec((rows, tn), lambda i, j, ri=ri: (ri(i), jnp.clip(j - nA - nS, 0, nG - 1))),
            pl.BlockSpec((rows, LANES), lambda i, j, ri=ri: (ri(i), 0)),
        ]
        n_rows = xs_.shape[0]
        out_shape += [
            jax.ShapeDtypeStruct((n_rows, A_COLS), BF16),
            jax.ShapeDtypeStruct((n_rows, S_COLS), BF16),
            jax.ShapeDtypeStruct((n_rows, G_COLS), BF16),
            jax.ShapeDtypeStruct((n_rows, LANES), F32),
        ]
        scratch.append(pltpu.VMEM((rows, D_MODEL), BF16))
    res = pl.pallas_call(
        functools.partial(_inproj_kernel, nA, nS, len(sets)),
        grid=(t // tm, nj),
        in_specs=x_specs + [
            pl.BlockSpec((1, D_MODEL), lambda i, j: (0, 0)),
            pl.BlockSpec((tn, D_MODEL), lambda i, j: (jnp.minimum(j, nA - 1), 0)),
            pl.BlockSpec((tn, D_MODEL), lambda i, j: (jnp.clip(j - nA, 0, nS + nG - 1), 0)),
            pl.BlockSpec((LANES, D_MODEL), lambda i, j: (0, 0)),
        ],
        out_specs=out_specs,
        out_shape=out_shape,
        scratch_shapes=scratch,
        compiler_params=pltpu.CompilerParams(
            dimension_semantics=("arbitrary", "arbitrary"), vmem_limit_bytes=VMEM_LIMIT),
        name="inproj",
    )(*[st[0] for st in sets], lnw, wa, wsg, wdt)
    return (res[:4], res[4:]) if extra is not None else res


def halo_rows(ns):
    return max(2 * SUBLANES, ns * SUBLANES)


def _shift_matrices(q, ns, taps):
    L = q // ns
    hb = halo_rows(ns)
    r = np.arange(q)[:, None]
    col = np.arange(3 * hb + q)[None, :]
    seq, t = r // L, r % L
    mats = []
    for k in range(taps - 1):
        d = taps - 1 - k
        from_cur = (col >= 3 * hb) & (col - 3 * hb == r - d) & (t >= d)
        from_hist = (col < 3 * hb) & (col % hb == seq * SUBLANES + SUBLANES - d + t) & (t < d)
        mats.append(from_cur | from_hist)
    return jnp.asarray(np.stack(mats), dtype=BF16)


def _seqmix_kernel(Q, NS, pad_rows,
                   a_ref, s_ref, dt_ref, hist_ref, schist_ref, st0_ref, shift_ref,
                   cw_ref, cb_ref, scw_ref, dtb_ref, alog_ref, drow_ref, nw_ref,
                   u_ref, v_ref, st_ref, cs_ref, scs_ref,
                   halo, scpad, xbc_s, y_s, xst_s):
    L = Q // NS
    lg = L.bit_length() - 1
    c = pl.program_id(1)

    @pl.when(c == 0)
    def _():
        halo[...] = jnp.zeros_like(halo)
        for s in range(NS):
            halo[(s + 1) * SUBLANES - (SSD_CONV - 1):(s + 1) * SUBLANES, :] = hist_ref[s]
            scpad[s, SUBLANES - (SC_CONV - 1):SUBLANES, :] = schist_ref[s]
        st_ref[...] = st0_ref[...]

    CW = 256
    hb = halo_rows(NS)
    for cc in range(0, SSD_CONV_DIM, CW):
        raw_b = a_ref[:, SSD_INNER + cc:SSD_INNER + cc + CW]
        hl = halo[:, cc:cc + CW]
        if hb > hl.shape[0]:
            hl = jnp.concatenate([hl, jnp.zeros((hb - hl.shape[0], CW), F32)], axis=0)
        h_hi = hl.astype(BF16)
        h_r = hl - h_hi.astype(F32)
        h_mid = h_r.astype(BF16)
        h_lo = (h_r - h_mid.astype(F32)).astype(BF16)
        ext = jnp.concatenate([h_hi, h_mid, h_lo, raw_b], axis=0)
        raw = raw_b.astype(F32)
        acc = raw * cw_ref[3:4, cc:cc + CW] + cb_ref[:, cc:cc + CW]
        for k in range(SSD_CONV - 1):
            acc = acc + jnp.dot(shift_ref[k], ext, preferred_element_type=F32) * cw_ref[k:k + 1, cc:cc + CW]
        xbc_s[:, cc:cc + CW] = acc * _sigmoid(acc)
        for s in range(NS):
            halo[s * SUBLANES:(s + 1) * SUBLANES, cc:cc + CW] = raw[(s + 1) * L - SUBLANES:(s + 1) * L]
    for s in range(NS):
        cs_ref[s] = halo[(s + 1) * SUBLANES - (SSD_CONV - 1):(s + 1) * SUBLANES, :]

    for cc in range(0, SC_WIDTH, CW):
        scb = s_ref[:, cc:cc + CW].astype(F32)
        ch_all = (s_ref[:, SC_WIDTH + cc:SC_WIDTH + cc + CW].astype(F32)
                  * s_ref[:, 2 * SC_WIDTH + cc:2 * SC_WIDTH + cc + CW].astype(F32))
        for s in range(NS):
            ch = ch_all[s * L:(s + 1) * L]
            scpad[s, SUBLANES:SUBLANES + L, cc:cc + CW] = ch
            acc = ch * scw_ref[2:3, cc:cc + CW]
            for k in range(SC_CONV - 1):
                acc = acc + scpad[s, 6 + k:6 + k + L, cc:cc + CW] * scw_ref[k:k + 1, cc:cc + CW]
            v_ref[s * L:(s + 1) * L, cc:cc + CW] = (scb[s * L:(s + 1) * L] * acc).astype(BF16)
    for s in range(NS):
        scs_ref[s] = scpad[s, L + 6:L + 8, :]
        scpad[s, 0:SUBLANES, :] = scpad[s, L:L + SUBLANES, :]

    def padrows(x):
        if Q == LANES:
            return x
        return jnp.concatenate([x, jnp.zeros((LANES - Q, x.shape[1]), x.dtype)], axis=0)

    li = lax.broadcasted_iota(jnp.int32, (Q, Q), 0)
    si = lax.broadcasted_iota(jnp.int32, (Q, Q), 1)
    same = (li >> lg) == (si >> lg)
    causal = jnp.logical_and(same, si <= li)
    tril01 = jnp.where(causal, 1.0, 0.0).astype(BF16)
    same01 = jnp.where(same, 1.0, 0.0).astype(BF16)

    dt = _softplus(dt_ref[...] + dtb_ref[...])
    if pad_rows:
        ri = lax.broadcasted_iota(jnp.int32, (Q, LANES), 0)
        dt = jnp.where(ri >= pad_rows, dt, 0.0)
    da = dt * (-jnp.exp(alog_ref[...]))
    acum = _dot01(tril01, da)
    tot = _dot01(same01, da)
    acum_t = padrows(acum).T
    tot_t = padrows(tot).T
    dt_t = padrows(dt).T
    w_t = jnp.exp(tot_t - acum_t) * dt_t
    acum2 = acum * LOG2E
    rowq_t = acum_t * LOG2E - jnp.log2(dt_t)

    left_head = lax.broadcasted_iota(jnp.int32, (Q, LANES), 1) < SSD_HEAD_DIM
    rowseq = lax.broadcasted_iota(jnp.int32, (Q, GROUP_COLS), 0) >> lg
    rowseq_p = lax.broadcasted_iota(jnp.int32, (LANES, LANES), 0) >> lg

    for g in range(SSD_GROUPS):
        b_g = xbc_s[:, SSD_INNER + g * SSD_STATE:SSD_INNER + (g + 1) * SSD_STATE]
        c_g = xbc_s[:, SSD_INNER + GROUP_COLS + g * SSD_STATE:SSD_INNER + GROUP_COLS + (g + 1) * SSD_STATE]
        b_gb = b_g.astype(BF16)
        c_gb = c_g.astype(BF16)
        cbm = _nt_dot(c_gb, b_gb)
        yo = None
        for s in range(NS):
            h_s = st_ref[s, g * GROUP_COLS:(g + 1) * GROUP_COLS, :].astype(BF16)
            yo_s = _nt_dot(c_gb, h_s)
            yo = yo_s if yo is None else jnp.where(rowseq == s, yo_s, yo)
        for rp in range(SSD_HPG // 2):
            h0 = g * SSD_HPG + 2 * rp
            cols = slice(h0 * SSD_HEAD_DIM, (h0 + 2) * SSD_HEAD_DIM)
            colbs, w_pair = [], []
            for h in (h0, h0 + 1):
                colb = jnp.broadcast_to(acum2[:, h:h + 1], (Q, LANES))
                rowb = jnp.broadcast_to(rowq_t[h:h + 1, 0:Q], (Q, Q))
                dec_dt = jnp.exp2(jnp.where(causal, colb[:, 0:Q] - rowb, NEG_BIG))
                w_pair.append((cbm * dec_dt).astype(BF16))
                colbs.append(colb)
            x_p = xbc_s[:, cols]
            rhs = jnp.concatenate([jnp.where(left_head, x_p, 0.0).astype(BF16),
                                   jnp.where(left_head, 0.0, x_p).astype(BF16)], axis=0)
            yd = jnp.dot(jnp.concatenate(w_pair, axis=1), rhs, preferred_element_type=F32)
            ecol = jnp.exp2(jnp.where(left_head, colbs[0], colbs[1]))
            y_s[:, cols] = yd + yo[:, rp * LANES:(rp + 1) * LANES] * ecol + drow_ref[:, cols] * x_p

    for jb in range(SSD_INNER // LANES):
        xst_s[jb * LANES:(jb + 1) * LANES, :] = padrows(xbc_s[:, jb * LANES:(jb + 1) * LANES]).T
    for s in range(NS):
        da_b = jnp.exp(jnp.broadcast_to(tot_t[:, s * L:s * L + 1], (LANES, LANES)))
        for g in range(SSD_GROUPS):
            b_p = padrows(xbc_s[:, SSD_INNER + g * SSD_STATE:SSD_INNER + (g + 1) * SSD_STATE])
            if NS > 1:
                b_p = jnp.where(rowseq_p == s, b_p, 0.0)
            pieces = []
            for r in range(SSD_HPG):
                h = g * SSD_HPG + r
                pieces.append(xst_s[h * SSD_HEAD_DIM:(h + 1) * SSD_HEAD_DIM, :] * w_t[h:h + 1, :])
            xw_t = jnp.concatenate(pieces, axis=0).astype(BF16)
            upd = jnp.dot(xw_t, b_p.astype(BF16), preferred_element_type=F32)
            for r in range(SSD_HPG):
                h = g * SSD_HPG + r
                rows = slice(h * SSD_HEAD_DIM, (h + 1) * SSD_HEAD_DIM)
                dec_h = jnp.broadcast_to(da_b[h:h + 1, :], (SSD_HEAD_DIM, SSD_STATE))
                st_ref[s, rows, :] = dec_h * st_ref[s, rows, :] + upd[r * SSD_HEAD_DIM:(r + 1) * SSD_HEAD_DIM, :]

    for g in range(SSD_GROUPS):
        cols = slice(g * GROUP_COLS, (g + 1) * GROUP_COLS)
        z = a_ref[:, cols].astype(F32)
        ug = y_s[:, cols] * (z * _sigmoid(z))
        ms = jnp.mean(ug * ug, axis=-1, keepdims=True)
        u_ref[:, cols] = (ug * lax.rsqrt(ms + EPS) * nw_ref[:, cols]).astype(BF16)


def _seqmix(pa, ps, pdt, hist, schist, st0, prm, *, nb, nc, q, ns, pad_rows, init_bcast):
    L = q // ns
    cw, cb, scw, dtb, alog, drow, nw = prm

    def tok(b, c):
        return (b * nc + c, 0)

    def init3(b, c):
        return (0 if init_bcast else b, 0, 0)

    def const2(b, c):
        return (0, 0)

    def per_b(b, c):
        return (b, 0, 0)

    n_seq = nb * ns
    shifts = _shift_matrices(q, ns, SSD_CONV)
    return pl.pallas_call(
        functools.partial(_seqmix_kernel, q, ns, pad_rows),
        grid=(nb, nc),
        in_specs=[
            pl.BlockSpec((q, A_COLS), tok),
            pl.BlockSpec((q, S_COLS), tok),
            pl.BlockSpec((q, LANES), tok),
            pl.BlockSpec((ns, SSD_CONV - 1, SSD_CONV_DIM), init3),
            pl.BlockSpec((ns, SC_CONV - 1, SC_WIDTH), init3),
            pl.BlockSpec((ns, SSD_INNER, SSD_STATE), init3),
            pl.BlockSpec(shifts.shape, lambda b, c: (0, 0, 0)),
            pl.BlockSpec((SSD_CONV, SSD_CONV_DIM), const2),
            pl.BlockSpec((1, SSD_CONV_DIM), const2),
            pl.BlockSpec((SC_CONV, SC_WIDTH), const2),
            pl.BlockSpec((1, LANES), const2),
            pl.BlockSpec((1, LANES), const2),
            pl.BlockSpec((1, SSD_INNER), const2),
            pl.BlockSpec((1, SSD_INNER), const2),
        ],
        out_specs=[
            pl.BlockSpec((q, SSD_INNER), tok),
            pl.BlockSpec((q, SC_WIDTH), tok),
            pl.BlockSpec((ns, SSD_INNER, SSD_STATE), per_b),
            pl.BlockSpec((ns, SSD_CONV - 1, SSD_CONV_DIM), per_b),
            pl.BlockSpec((ns, SC_CONV - 1, SC_WIDTH), per_b),
        ],
        out_shape=[
            jax.ShapeDtypeStruct((nb * nc * q, SSD_INNER), BF16),
            jax.ShapeDtypeStruct((nb * nc * q, SC_WIDTH), BF16),
            jax.ShapeDtypeStruct((n_seq, SSD_INNER, SSD_STATE), F32),
            jax.ShapeDtypeStruct((n_seq, SSD_CONV - 1, SSD_CONV_DIM), F32),
            jax.ShapeDtypeStruct((n_seq, SC_CONV - 1, SC_WIDTH), F32),
        ],
        scratch_shapes=[
            pltpu.VMEM((ns * SUBLANES, SSD_CONV_DIM), F32),
            pltpu.VMEM((ns, SUBLANES + L, SC_WIDTH), F32),
            pltpu.VMEM((q, SSD_CONV_DIM), F32),
            pltpu.VMEM((q, SSD_INNER), F32),
            pltpu.VMEM((SSD_INNER, LANES), F32),
        ],
        compiler_params=pltpu.CompilerParams(
            dimension_semantics=("arbitrary", "arbitrary"), vmem_limit_bytes=VMEM_LIMIT),
        name="seqmix",
    )(pa, ps, pdt, hist, schist, st0, shifts, cw, cb, scw, dtb, alog, drow, nw)


def _merge_kernel(u_ref, v_ref, g_ref, x_ref, wa_ref, ws_ref, wo_ref, ln2_ref, wr_ref, br_ref,
                  xg_ref, cnt_s):
    i = pl.program_id(0)

    @pl.when(i == 0)
    def _():
        cnt_s[...] = jnp.zeros_like(cnt_s)

    y_ssd = jnp.dot(u_ref[...], wa_ref[...], preferred_element_type=F32)
    y_sc = jnp.dot(v_ref[...], ws_ref[...], preferred_element_type=F32)
    g1 = g_ref[:, 0:D_MODEL].astype(F32)
    g2 = g_ref[:, D_MODEL:2 * D_MODEL].astype(F32)
    merged = (_sigmoid(g1) * y_ssd + _sigmoid(g2) * y_sc).astype(BF16)
    x1 = x_ref[...] + jnp.dot(merged, wo_ref[...], preferred_element_type=F32)
    xg_ref[:, 0:D_MODEL] = x1
    ms = jnp.mean(x1 * x1, axis=-1, keepdims=True)
    h2 = (x1 * lax.rsqrt(ms + EPS) * ln2_ref[...]).astype(BF16)

    logits = jnp.dot(h2, wr_ref[...], preferred_element_type=F32) + br_ref[...]
    lane = lax.broadcasted_iota(jnp.int32, logits.shape, 1)
    big = jnp.int32(1 << 20)
    gl = jnp.where(lane < N_EGROUPS, logits, NEG_BIG)
    gmax = jnp.max(gl, axis=-1, keepdims=True)
    g_sel = jnp.min(jnp.where(gl == gmax, lane, big), axis=-1, keepdims=True)
    gsum = jnp.sum(jnp.exp(gl - gmax), axis=-1, keepdims=True)
    g_prob = 1.0 / gsum
    lo = N_EGROUPS + EXPERTS_PER_GROUP * g_sel
    emask = jnp.logical_and(lane >= lo, lane < lo + EXPERTS_PER_GROUP)
    el = jnp.where(emask, logits, NEG_BIG)
    m1 = jnp.max(el, axis=-1, keepdims=True)
    e = jnp.where(emask, jnp.exp(el - m1), -1.0)
    i1 = jnp.min(jnp.where(e == 1.0, lane, big), axis=-1, keepdims=True)
    e_rest = jnp.where(lane == i1, -1.0, e)
    e2 = jnp.max(e_rest, axis=-1, keepdims=True)
    i2 = jnp.min(jnp.where(e_rest == e2, lane, big), axis=-1, keepdims=True)
    denom = 1.0 + e2
    w1 = g_prob / denom
    w2 = g_prob * e2 / denom
    gates = jnp.where(lane == i1, w1, 0.0) + jnp.where(lane == i2, w2, 0.0)

    tm = logits.shape[0]
    rb = min(tm, RANK_BLOCK)
    onehot = jnp.where(lane == g_sel, 1.0, 0.0).astype(BF16)
    ri = lax.broadcasted_iota(jnp.int32, (rb, rb), 0)
    ci = lax.broadcasted_iota(jnp.int32, (rb, rb), 1)
    tril01 = jnp.where(ci <= ri, 1.0, 0.0).astype(BF16)
    carry = cnt_s[...]
    incl_blocks = []
    for r0 in range(0, tm, rb):
        incl_b = jnp.dot(tril01, onehot[r0:r0 + rb], preferred_element_type=F32) + carry
        carry = incl_b[rb - 1:rb, :]
        incl_blocks.append(incl_b)
    cnt_s[...] = carry
    incl = incl_blocks[0] if len(incl_blocks) == 1 else jnp.concatenate(incl_blocks, axis=0)
    rank = jnp.sum(jnp.where(lane == g_sel, incl - 1.0, 0.0), axis=-1, keepdims=True)
    xg_ref[:, D_MODEL:D_MODEL + LANES] = jnp.where(
        lane == 0, g_sel.astype(F32), jnp.where(lane == 1, rank, gates))


def _merge(u, v, g, x, wa, ws, wo, ln2, wr, br, tm):
    t = x.shape[0]
    row = lambda i: (i, 0)
    const = lambda i: (0, 0)
    return pl.pallas_call(
        _merge_kernel,
        grid=(t // tm,),
        in_specs=[
            pl.BlockSpec((tm, SSD_INNER), row),
            pl.BlockSpec((tm, SC_WIDTH), row),
            pl.BlockSpec((tm, G_COLS), row),
            pl.BlockSpec((tm, D_MODEL), row),
            pl.BlockSpec((SSD_INNER, D_MODEL), const, pipeline_mode=pl.Buffered(1)),
            pl.BlockSpec((SC_WIDTH, D_MODEL), const, pipeline_mode=pl.Buffered(1)),
            pl.BlockSpec((D_MODEL, D_MODEL), const, pipeline_mode=pl.Buffered(1)),
            pl.BlockSpec((1, D_MODEL), const),
            pl.BlockSpec((D_MODEL, LANES), const),
            pl.BlockSpec((1, LANES), const),
        ],
        out_specs=pl.BlockSpec((tm, XG_COLS), row),
        out_shape=jax.ShapeDtypeStruct((t, XG_COLS), F32),
        scratch_shapes=[pltpu.VMEM((1, LANES), F32)],
        compiler_params=pltpu.CompilerParams(
            dimension_semantics=("arbitrary",), vmem_limit_bytes=VMEM_LIMIT),
        name="merge",
    )(u, v, g, x, wa, ws, wo, ln2, wr, br)


def _moe_kernel(tmg, inv_ref, tg_ref, nv_ref, xg_hbm, wg_ref, wu_ref, wd_ref, ln2_ref, fw_ref,
                y_hbm, xbuf, ybuf, sem_in, sem_out):
    i = pl.program_id(0)
    n = pl.num_programs(0)
    slot = i % 2

    def row_in(tok, r, slt):
        return pltpu.make_async_copy(xg_hbm.at[pl.ds(tok, 1)], xbuf.at[slt, pl.ds(r, 1)], sem_in.at[slt])

    def row_out(tok, r, slt):
        return pltpu.make_async_copy(ybuf.at[slt, pl.ds(r, 1)], y_hbm.at[pl.ds(tok, 1)], sem_out.at[slt])

    unroll = 8

    def gather(tile, slt):
        def body(r, carry):
            row_in(inv_ref[tile * tmg + r], r, slt).start()
            return carry
        lax.fori_loop(0, tmg, body, 0, unroll=unroll)

    def wait_gather(slt):
        pltpu.make_async_copy(xg_hbm.at[pl.ds(0, tmg)], xbuf.at[slt], sem_in.at[slt]).wait()

    def scatter(tile, slt):
        def body(r, carry):
            row_out(inv_ref[tile * tmg + r], r, slt).start()
            return carry
        full = nv_ref[tile] == tmg

        @pl.when(full)
        def _():
            for r in range(tmg):
                body(r, 0)

        @pl.when(jnp.logical_not(full))
        def _():
            lax.fori_loop(0, nv_ref[tile], body, 0)

    def wait_scatter(tile, slt):
        def body(r, carry):
            row_out(0, r, slt).wait()
            return carry
        full = nv_ref[tile] == tmg

        @pl.when(full)
        def _():
            pltpu.make_async_copy(ybuf.at[slt], y_hbm.at[pl.ds(0, tmg)], sem_out.at[slt]).wait()

        @pl.when(jnp.logical_not(full))
        def _():
            lax.fori_loop(0, nv_ref[tile], body, 0)

    @pl.when(i == 0)
    def _():
        gather(0, 0)

    nxt = jnp.minimum(i + 1, n - 1)

    @pl.when(i >= 2)
    def _():
        wait_scatter(jnp.maximum(i - 2, 0), slot)

    @pl.when(nv_ref[i] == 0)
    def _():
        gather(nxt, 1 - slot)
        wait_gather(slot)

    @pl.when(nv_ref[i] > 0)
    def _():
        wait_gather(slot)
        x1 = xbuf[slot, :, 0:D_MODEL]
        for r in range(tmg // 2):
            row_in(inv_ref[nxt * tmg + r], r, 1 - slot).start()
        gate = xbuf[slot, :, D_MODEL:D_MODEL + LANES]
        for r in range(tmg // 2, tmg):
            row_in(inv_ref[nxt * tmg + r], r, 1 - slot).start()
        x1_res = xbuf[slot, :, 0:D_MODEL]
        ms = jnp.mean(x1 * x1, axis=-1, keepdims=True)
        h2 = (x1 * lax.rsqrt(ms + EPS) * ln2_ref[...]).astype(BF16)
        lane = lax.broadcasted_iota(jnp.int32, gate.shape, 1)
        first = N_EGROUPS + EXPERTS_PER_GROUP * tg_ref[i]
        pieces = []
        for e in range(EXPERTS_PER_GROUP):
            a = jnp.dot(h2, wg_ref[e], preferred_element_type=F32)
            up = jnp.dot(h2, wu_ref[e], preferred_element_type=F32)
            ge = jnp.sum(jnp.where(lane == first + e, gate, 0.0), axis=-1, keepdims=True)
            pieces.append((a * _sigmoid(a) * up * ge).astype(BF16))
        mo = jnp.dot(jnp.concatenate(pieces, axis=1), wd_ref[...], preferred_element_type=F32)
        x2 = x1_res + mo
        ms2 = jnp.mean(x2 * x2, axis=-1, keepdims=True)
        ybuf[slot] = x2 * lax.rsqrt(ms2 + EPS) * fw_ref[...]
        scatter(i, slot)

    @pl.when(i == n - 1)
    def _():
        wait_gather(1 - slot)

        @pl.when(i >= 1)
        def _():
            wait_scatter(jnp.maximum(i - 1, 0), 1 - slot)
        wait_scatter(i, slot)


def _moe(xg, inv, tile_group, n_valid, wg, wu, wd, ln2, fw, tmg):
    t = xg.shape[0]
    n_tiles = tile_group.shape[0]
    gcols = EXPERTS_PER_GROUP * D_EXPERT
    const = lambda i, inv, tg, nv: (0, 0)
    by_group = lambda i, inv, tg, nv: (tg[i], 0, 0)
    return pl.pallas_call(
        functools.partial(_moe_kernel, tmg),
        grid_spec=pltpu.PrefetchScalarGridSpec(
            num_scalar_prefetch=3,
            grid=(n_tiles,),
            in_specs=[
                pl.BlockSpec(memory_space=pltpu.HBM),
                pl.BlockSpec((EXPERTS_PER_GROUP, D_MODEL, D_EXPERT), by_group),
                pl.BlockSpec((EXPERTS_PER_GROUP, D_MODEL, D_EXPERT), by_group),
                pl.BlockSpec((None, gcols, D_MODEL), by_group),
                pl.BlockSpec((1, D_MODEL), const),
                pl.BlockSpec((1, D_MODEL), const),
            ],
            out_specs=pl.BlockSpec(memory_space=pltpu.HBM),
            scratch_shapes=[
                pltpu.VMEM((2, tmg, XG_COLS), F32),
                pltpu.VMEM((2, tmg, D_MODEL), F32),
                pltpu.SemaphoreType.DMA((2,)),
                pltpu.SemaphoreType.DMA((2,)),
            ],
        ),
        out_shape=jax.ShapeDtypeStruct((t, D_MODEL), F32),
        compiler_params=pltpu.CompilerParams(
            dimension_semantics=("arbitrary",), vmem_limit_bytes=VMEM_LIMIT),
        name="moe",
    )(inv, tile_group, n_valid, xg, wg, wu, wd, ln2, fw)


def _route(xg, tmg):
    t = xg.shape[0]
    g = xg[:, D_MODEL].astype(jnp.int32)
    rank = xg[:, D_MODEL + 1].astype(jnp.int32)
    counts = jnp.sum(g[:, None] == jnp.arange(N_EGROUPS, dtype=jnp.int32)[None, :], axis=0, dtype=jnp.int32)
    tiles_per = (counts + tmg - 1) // tmg
    tile_end = jnp.cumsum(tiles_per)
    tile_base = tile_end - tiles_per
    n_tiles = t // tmg + N_EGROUPS
    pos = tile_base[g] * tmg + rank
    inv = jnp.zeros((n_tiles * tmg,), jnp.int32).at[pos].set(
        jnp.arange(t, dtype=jnp.int32), unique_indices=True, mode="promise_in_bounds")
    ti = jnp.arange(n_tiles, dtype=jnp.int32)
    tg = jnp.minimum(jnp.sum(ti[:, None] >= tile_end[None, :], axis=1, dtype=jnp.int32), N_EGROUPS - 1)
    nv = jnp.clip(counts[tg] - (ti - tile_base[tg]) * tmg, 0, tmg)
    nv = jnp.where(ti < tile_end[-1], nv, 0).astype(jnp.int32)
    return inv, tg, nv


def _pad_lanes(v, n=LANES):
    v = v.reshape(1, -1).astype(F32)
    return jnp.pad(v, ((0, 0), (0, n - v.shape[1])))


def _cast_kernel(a_ref, b_ref, c_ref, oa_ref, ob_ref, oc_ref):
    oa_ref[...] = a_ref[...].astype(BF16)
    ob_ref[...] = b_ref[...].astype(BF16)
    oc_ref[...] = c_ref[...].astype(BF16)


def _cast_expert_weights(wg, wu, wd, eb=4):
    n = wg.shape[0]
    specs = [pl.BlockSpec((eb,) + w.shape[1:], lambda i: (i, 0, 0)) for w in (wg, wu, wd)]
    return pl.pallas_call(
        _cast_kernel,
        grid=(n // eb,),
        in_specs=specs,
        out_specs=specs,
        out_shape=[jax.ShapeDtypeStruct(w.shape, BF16) for w in (wg, wu, wd)],
        compiler_params=pltpu.CompilerParams(
            dimension_semantics=("arbitrary",), vmem_limit_bytes=VMEM_LIMIT),
        name="castw",
    )(wg, wu, wd)


def kernel(x_prompt, x_sample, state_ssm, state_ssd_conv, state_short_conv, meta_tokens, ln1_w, w_in,
           ssd_conv_w, ssd_conv_b, ssd_dt_bias, ssd_A_log, ssd_D, ssd_norm_w, w_ssd_out, sc_conv_w,
           w_sc_out, w_o, ln2_w, w_rg, b_rg, w_re, b_re, w_gate, w_up, w_down, final_norm_w):
    assert ln1_w.shape[0] == 1, "single-layer trunk"
    bp, sp, _ = x_prompt.shape
    bs, ss, _ = x_sample.shape
    l = 0

    a0, a1 = 0, A_COLS
    d0, d1 = A_COLS, A_COLS + SSD_HEADS
    wt = jnp.swapaxes(w_in[l], 0, 1)
    w_a = wt
    w_sg = wt[d1:]
    wdt = jnp.pad(wt[d0:d1], ((0, LANES - SSD_HEADS), (0, 0))).astype(BF16)
    lnw = ln1_w[l].reshape(1, D_MODEL)
    prm = (ssd_conv_w[l], ssd_conv_b[l].reshape(1, -1), sc_conv_w[l],
           _pad_lanes(ssd_dt_bias[l]), _pad_lanes(ssd_A_log[l]),
           jnp.repeat(ssd_D[l].astype(F32), SSD_HEAD_DIM).reshape(1, SSD_INNER),
           ssd_norm_w[l].reshape(1, SSD_INNER))
    wa = w_ssd_out[l].astype(BF16)
    ws = w_sc_out[l].astype(BF16)
    wo = w_o[l].astype(BF16)
    ln2 = ln2_w[l].reshape(1, D_MODEL)
    wr = jnp.pad(jnp.concatenate([w_rg[l], w_re[l]], axis=1),
                 ((0, 0), (0, LANES - N_EGROUPS - N_EXPERTS))).astype(BF16)
    br = _pad_lanes(jnp.concatenate([b_rg[l], b_re[l]]))
    gcols = EXPERTS_PER_GROUP * D_EXPERT
    wg, wu, wd = _cast_expert_weights(w_gate[l].reshape(N_EXPERTS, D_MODEL, D_EXPERT),
                                      w_up[l].reshape(N_EXPERTS, D_MODEL, D_EXPERT),
                                      w_down[l].reshape(N_EXPERTS, D_EXPERT, D_MODEL))
    wd = wd.reshape(N_EGROUPS, gcols, D_MODEL)
    fw = final_norm_w.reshape(1, D_MODEL)

    def mlp_tail(u, v, g, x, tmg):
        xg = _merge(u, v, g, x, wa, ws, wo, ln2, wr, br, tm=1024)
        inv, tg, nv = _route(xg, tmg)
        return _moe(xg, inv, tg, nv, wg, wu, wd, ln2, fw, tmg)

    q = LANES
    xm = jnp.pad(meta_tokens.astype(F32), ((q - N_META, 0), (0, 0)))
    xs = x_sample.reshape(bs * ss, D_MODEL)
    (sa, ssc, sg, sdt), (ma, msc, _, mdt) = _inproj(xs, lnw, w_a, w_sg, wdt, tm=bs * ss, extra=xm)
    zeros_hist = jnp.zeros((1, SSD_CONV - 1, SSD_CONV_DIM), F32)
    zeros_sch = jnp.zeros((1, SC_CONV - 1, SC_WIDTH), F32)
    zeros_st = jnp.zeros((1, SSD_INNER, SSD_STATE), F32)
    _, _, m_st, m_cs, m_scs = _seqmix(ma, msc, mdt, zeros_hist, zeros_sch, zeros_st, prm,
                                      nb=1, nc=1, q=q, ns=1, pad_rows=q - N_META, init_bcast=True)

    xp = x_prompt.reshape(bp * sp, D_MODEL)
    pa, psc, pg, pdt = _inproj(xp, lnw, w_a, w_sg, wdt, tm=2048)
    pu, pv, p_st, p_cs, p_scs = _seqmix(pa, psc, pdt, m_cs, m_scs, m_st, prm,
                                        nb=bp, nc=sp // q, q=q, ns=1, pad_rows=0, init_bcast=True)
    y_prompt = mlp_tail(pu, pv, pg, xp, MOE_TILE).reshape(bp, sp, D_MODEL)

    ns = 8
    su, sv, s_st, s_cs, s_scs = _seqmix(sa, ssc, sdt, state_ssd_conv[l], state_short_conv[l],
                                        state_ssm[l].reshape(bs, SSD_INNER, SSD_STATE), prm,
                                        nb=bs // ns, nc=1, q=ns * ss, ns=ns, pad_rows=0, init_bcast=False)
    y_sample = mlp_tail(su, sv, sg, xs, MOE_TILE_SMALL).reshape(bs, ss, D_MODEL)

    hshape = (SSD_HEADS, SSD_HEAD_DIM, SSD_STATE)
    return (y_prompt, y_sample,
            p_st.reshape(1, bp, *hshape), p_cs[None], p_scs[None],
            s_st.reshape(1, bs, *hshape), s_cs[None], s_scs[None])
```

```python
import functools

import jax
import jax.numpy as jnp
import numpy as np
from jax import lax
from jax.experimental import pallas as pl
from jax.experimental.pallas import tpu as pltpu

F32 = jnp.float32
BF16 = jnp.bfloat16

D_MODEL = 1024
N_META = 16
SSD_INNER = 2048
SSD_HEAD_DIM = 64
SSD_HEADS = 32
SSD_GROUPS = 4
SSD_HPG = 8
SSD_STATE = 128
SSD_CONV = 4
SSD_CONV_DIM = 3072
SC_WIDTH = 1024
SC_CONV = 3
N_EGROUPS = 4
EXPERTS_PER_GROUP = 8
N_EXPERTS = 32
D_EXPERT = 256
EPS = 1e-6

LANES = 128
SUBLANES = 8
GROUP_COLS = SSD_HPG * SSD_HEAD_DIM
A_COLS = SSD_INNER + SSD_CONV_DIM
S_COLS = 3 * SC_WIDTH
G_COLS = 2 * D_MODEL
PROJ_COLS = A_COLS + S_COLS + G_COLS
XG_COLS = D_MODEL + LANES
MOE_TILE = 256
MOE_TILE_SMALL = 128
RANK_BLOCK = 512
NEG_BIG = -1e30
LOG2E = 1.4426950408889634
VMEM_LIMIT = 56 * 1024 * 1024


def _nt_dot(a, b):
    return lax.dot_general(a, b, (((1,), (1,)), ((), ())), preferred_element_type=F32)


def _dot01(m01_bf16, x):
    hi = x.astype(BF16)
    r1 = x - hi.astype(F32)
    mid = r1.astype(BF16)
    lo = (r1 - mid.astype(F32)).astype(BF16)
    out = jnp.dot(m01_bf16, hi, preferred_element_type=F32)
    out = out + jnp.dot(m01_bf16, mid, preferred_element_type=F32)
    return out + jnp.dot(m01_bf16, lo, preferred_element_type=F32)


def _softplus(x):
    return jnp.maximum(x, 0.0) + jnp.log1p(jnp.exp(-jnp.abs(x)))


def _sigmoid(x):
    return 1.0 / (1.0 + jnp.exp(-x))


def _inproj_kernel(n_sets, *refs):
    x_refs = refs[:n_sets]
    lnw_ref, wt_ref, wdt_ref = refs[n_sets:n_sets + 3]
    outs = refs[n_sets + 3:n_sets + 3 + 2 * n_sets]
    h_refs = refs[n_sets + 3 + 2 * n_sets:]

    @pl.when(pl.program_id(1) == 0)
    def _():
        for k in range(n_sets):
            x = x_refs[k][...]
            ms = jnp.mean(x * x, axis=-1, keepdims=True)
            h = (x * lax.rsqrt(ms + EPS) * lnw_ref[...]).astype(BF16)
            h_refs[k][...] = h
            outs[2 * k + 1][...] = _nt_dot(h, wdt_ref[...])

    w = wt_ref[...].astype(BF16)
    for k in range(n_sets):
        outs[2 * k][...] = _nt_dot(h_refs[k][...], w).astype(BF16)


def _inproj(x, lnw, wt, wdt, tm, tn=1024, extra=None):
    t = x.shape[0]
    n_a = A_COLS // tn
    nj = PROJ_COLS // tn

    def w_rows(i, j):
        return (pl.multiple_of(jnp.where(j < n_a, j * tn, j * tn + SSD_HEADS), SSD_HEADS), 0)

    sets = [(x, tm, lambda i: i)]
    if extra is not None:
        assert t == tm, "extra rows are recomputed per row tile"
        sets.append((extra, extra.shape[0], lambda i: 0))
    x_specs, out_specs, out_shape, scratch = [], [], [], []
    for xs_, rows, ri in sets:
        x_specs.append(pl.BlockSpec((rows, D_MODEL), lambda i, j, ri=ri: (ri(i), 0)))
        out_specs += [pl.BlockSpec((rows, tn), lambda i, j, ri=ri: (ri(i), j)),
                      pl.BlockSpec((rows, LANES), lambda i, j, ri=ri: (ri(i), 0))]
        out_shape += [jax.ShapeDtypeStruct((xs_.shape[0], PROJ_COLS), BF16),
                      jax.ShapeDtypeStruct((xs_.shape[0], LANES), F32)]
        scratch.append(pltpu.VMEM((rows, D_MODEL), BF16))
    res = pl.pallas_call(
        functools.partial(_inproj_kernel, len(sets)),
        grid=(t // tm, nj),
        in_specs=x_specs + [
            pl.BlockSpec((1, D_MODEL), lambda i, j: (0, 0)),
            pl.BlockSpec((pl.Element(tn), pl.Element(D_MODEL)), w_rows),
            pl.BlockSpec((LANES, D_MODEL), lambda i, j: (0, 0)),
        ],
        out_specs=out_specs,
        out_shape=out_shape,
        scratch_shapes=scratch,
        compiler_params=pltpu.CompilerParams(
            dimension_semantics=("arbitrary", "arbitrary"), vmem_limit_bytes=VMEM_LIMIT),
        name="inproj",
    )(*[st[0] for st in sets], lnw, wt, wdt)
    return (res[:2], res[2:]) if extra is not None else res


def halo_rows(ns):
    return max(2 * SUBLANES, ns * SUBLANES)


def _shift_matrices(q, ns, taps):
    L = q // ns
    hb = halo_rows(ns)
    r = np.arange(q)[:, None]
    col = np.arange(3 * hb + q)[None, :]
    seq, t = r // L, r % L
    mats = []
    for k in range(taps - 1):
        d = taps - 1 - k
        from_cur = (col >= 3 * hb) & (col - 3 * hb == r - d) & (t >= d)
        from_hist = (col < 3 * hb) & (col % hb == seq * SUBLANES + SUBLANES - d + t) & (t < d)
        mats.append(from_cur | from_hist)
    return jnp.asarray(np.stack(mats), dtype=BF16)


def _seqmix_kernel(Q, NS, CPS, one_step, pad_rows,
                   a_blk, sb_blk, sc_blk, sh_blk, dt_blk, hist_ref, schist_ref, st0_ref, shift_ref,
                   cw_ref, cb_ref, scw_ref, dtb_ref, alog_ref, drow_ref, nw_ref,
                   u_blk, v_blk, st_ref, cs_ref, scs_ref,
                   halo, scpad, xbc_s, y_s, xst_s):
    @pl.when(pl.program_id(1) == 0)
    def _():
        halo[...] = jnp.zeros_like(halo)
        for s in range(NS):
            halo[(s + 1) * SUBLANES - (SSD_CONV - 1):(s + 1) * SUBLANES, :] = hist_ref[s]
            scpad[s, SUBLANES - (SC_CONV - 1):SUBLANES, :] = schist_ref[s]
        if not one_step:
            st_ref[...] = st0_ref[...]

    src_ref = st0_ref if one_step else st_ref
    shared = (shift_ref, cw_ref, cb_ref, scw_ref, dtb_ref, alog_ref, drow_ref, nw_ref,
              src_ref, st_ref, cs_ref, scs_ref, halo, scpad, xbc_s, y_s, xst_s)
    row_blocks = (a_blk, sb_blk, sc_blk, sh_blk, dt_blk, u_blk, v_blk)
    if CPS == 1:
        _seqmix_chunk(Q, NS, pad_rows, *row_blocks, *shared)
    else:
        def body(k, carry):
            rows = pl.ds(pl.multiple_of(k * Q, Q), Q)
            _seqmix_chunk(Q, NS, pad_rows, *[r.at[rows] for r in row_blocks], *shared)
            return carry
        lax.fori_loop(0, CPS, body, 0)


def _seqmix_chunk(Q, NS, pad_rows, a_ref, sb_ref, sc_ref, sh_ref, dt_ref, u_ref, v_ref,
                  shift_ref, cw_ref, cb_ref, scw_ref, dtb_ref, alog_ref, drow_ref, nw_ref,
                  src_ref, st_ref, cs_ref, scs_ref, halo, scpad, xbc_s, y_s, xst_s):
    L = Q // NS
    lg = L.bit_length() - 1

    CW = 256
    hb = halo_rows(NS)
    for cc in range(0, SSD_CONV_DIM, CW):
        raw_b = a_ref[:, SSD_INNER + cc:SSD_INNER + cc + CW]
        hl = halo[:, cc:cc + CW]
        if hb > hl.shape[0]:
            hl = jnp.concatenate([hl, jnp.zeros((hb - hl.shape[0], CW), F32)], axis=0)
        h_hi = hl.astype(BF16)
        h_r = hl - h_hi.astype(F32)
        h_mid = h_r.astype(BF16)
        h_lo = (h_r - h_mid.astype(F32)).astype(BF16)
        ext = jnp.concatenate([h_hi, h_mid, h_lo, raw_b], axis=0)
        raw = raw_b.astype(F32)
        acc = raw * cw_ref[3:4, cc:cc + CW] + cb_ref[:, cc:cc + CW]
        for k in range(SSD_CONV - 1):
            acc = acc + jnp.dot(shift_ref[k], ext, preferred_element_type=F32) * cw_ref[k:k + 1, cc:cc + CW]
        xbc_s[:, cc:cc + CW] = acc * _sigmoid(acc)
        for s in range(NS):
            halo[s * SUBLANES:(s + 1) * SUBLANES, cc:cc + CW] = raw[(s + 1) * L - SUBLANES:(s + 1) * L]
    for s in range(NS):
        cs_ref[s] = halo[(s + 1) * SUBLANES - (SSD_CONV - 1):(s + 1) * SUBLANES, :]

    for cc in range(0, SC_WIDTH, CW):
        scb = sb_ref[:, cc:cc + CW].astype(F32)
        ch_all = sc_ref[:, cc:cc + CW].astype(F32) * sh_ref[:, cc:cc + CW].astype(F32)
        for s in range(NS):
            ch = ch_all[s * L:(s + 1) * L]
            scpad[s, SUBLANES:SUBLANES + L, cc:cc + CW] = ch
            acc = ch * scw_ref[2:3, cc:cc + CW]
            for k in range(SC_CONV - 1):
                acc = acc + scpad[s, 6 + k:6 + k + L, cc:cc + CW] * scw_ref[k:k + 1, cc:cc + CW]
            v_ref[s * L:(s + 1) * L, cc:cc + CW] = (scb[s * L:(s + 1) * L] * acc).astype(BF16)
    for s in range(NS):
        scs_ref[s] = scpad[s, L + 6:L + 8, :]
        scpad[s, 0:SUBLANES, :] = scpad[s, L:L + SUBLANES, :]

    def padrows(x):
        if Q == LANES:
            return x
        return jnp.concatenate([x, jnp.zeros((LANES - Q, x.shape[1]), x.dtype)], axis=0)

    li = lax.broadcasted_iota(jnp.int32, (Q, Q), 0)
    si = lax.broadcasted_iota(jnp.int32, (Q, Q), 1)
    same = (li >> lg) == (si >> lg)
    causal = jnp.logical_and(same, si <= li)
    tril01 = jnp.where(causal, 1.0, 0.0).astype(BF16)
    same01 = jnp.where(same, 1.0, 0.0).astype(BF16)

    dt = _softplus(dt_ref[...] + dtb_ref[...])
    if pad_rows:
        ri = lax.broadcasted_iota(jnp.int32, (Q, LANES), 0)
        dt = jnp.where(ri >= pad_rows, dt, 0.0)
    da = dt * (-jnp.exp(alog_ref[...]))
    acum = _dot01(tril01, da)
    tot = _dot01(same01, da)
    acum_t = padrows(acum).T
    tot_t = padrows(tot).T
    dt_t = padrows(dt).T
    w_t = jnp.exp(tot_t - acum_t) * dt_t
    acum2 = acum * LOG2E
    rowq_t = acum_t * LOG2E - jnp.log2(dt_t)

    left_head = lax.broadcasted_iota(jnp.int32, (Q, LANES), 1) < SSD_HEAD_DIM
    rowseq = lax.broadcasted_iota(jnp.int32, (Q, GROUP_COLS), 0) >> lg
    rowseq_p = lax.broadcasted_iota(jnp.int32, (LANES, LANES), 0) >> lg

    for g in range(SSD_GROUPS):
        b_g = xbc_s[:, SSD_INNER + g * SSD_STATE:SSD_INNER + (g + 1) * SSD_STATE]
        c_g = xbc_s[:, SSD_INNER + GROUP_COLS + g * SSD_STATE:SSD_INNER + GROUP_COLS + (g + 1) * SSD_STATE]
        b_gb = b_g.astype(BF16)
        c_gb = c_g.astype(BF16)
        cbm = _nt_dot(c_gb, b_gb)
        yo = None
        for s in range(NS):
            h_s = src_ref[s, g * GROUP_COLS:(g + 1) * GROUP_COLS, :].astype(BF16)
            yo_s = _nt_dot(c_gb, h_s)
            yo = yo_s if yo is None else jnp.where(rowseq == s, yo_s, yo)
        for rp in range(SSD_HPG // 2):
            h0 = g * SSD_HPG + 2 * rp
            cols = slice(h0 * SSD_HEAD_DIM, (h0 + 2) * SSD_HEAD_DIM)
            colbs, w_pair = [], []
            for h in (h0, h0 + 1):
                colb = jnp.broadcast_to(acum2[:, h:h + 1], (Q, LANES))
                rowb = jnp.broadcast_to(rowq_t[h:h + 1, 0:Q], (Q, Q))
                dec_dt = jnp.exp2(jnp.where(causal, colb[:, 0:Q] - rowb, NEG_BIG))
                w_pair.append((cbm * dec_dt).astype(BF16))
                colbs.append(colb)
            x_p = xbc_s[:, cols]
            rhs = jnp.concatenate([jnp.where(left_head, x_p, 0.0).astype(BF16),
                                   jnp.where(left_head, 0.0, x_p).astype(BF16)], axis=0)
            yd = jnp.dot(jnp.concatenate(w_pair, axis=1), rhs, preferred_element_type=F32)
            ecol = jnp.exp2(jnp.where(left_head, colbs[0], colbs[1]))
            y_s[:, cols] = yd + yo[:, rp * LANES:(rp + 1) * LANES] * ecol + drow_ref[:, cols] * x_p

    for jb in range(SSD_INNER // LANES):
        xst_s[jb * LANES:(jb + 1) * LANES, :] = padrows(xbc_s[:, jb * LANES:(jb + 1) * LANES]).T
    for s in range(NS):
        da_b = jnp.exp(jnp.broadcast_to(tot_t[:, s * L:s * L + 1], (LANES, LANES)))
        for g in range(SSD_GROUPS):
            b_p = padrows(xbc_s[:, SSD_INNER + g * SSD_STATE:SSD_INNER + (g + 1) * SSD_STATE])
            if NS > 1:
                b_p = jnp.where(rowseq_p == s, b_p, 0.0)
            pieces = []
            for r in range(SSD_HPG):
                h = g * SSD_HPG + r
                pieces.append(xst_s[h * SSD_HEAD_DIM:(h + 1) * SSD_HEAD_DIM, :] * w_t[h:h + 1, :])
            xw_t = jnp.concatenate(pieces, axis=0).astype(BF16)
            upd = jnp.dot(xw_t, b_p.astype(BF16), preferred_element_type=F32)
            for r in range(SSD_HPG):
                h = g * SSD_HPG + r
                rows = slice(h * SSD_HEAD_DIM, (h + 1) * SSD_HEAD_DIM)
                dec_h = jnp.broadcast_to(da_b[h:h + 1, :], (SSD_HEAD_DIM, SSD_STATE))
                st_ref[s, rows, :] = dec_h * src_ref[s, rows, :] + upd[r * SSD_HEAD_DIM:(r + 1) * SSD_HEAD_DIM, :]

    for g in range(SSD_GROUPS):
        cols = slice(g * GROUP_COLS, (g + 1) * GROUP_COLS)
        z = a_ref[:, cols].astype(F32)
        ug = y_s[:, cols] * (z * _sigmoid(z))
        ms = jnp.mean(ug * ug, axis=-1, keepdims=True)
        u_ref[:, cols] = (ug * lax.rsqrt(ms + EPS) * nw_ref[:, cols]).astype(BF16)


def _seqmix(proj, pdt, hist, schist, st0, prm, *, nb, nc, q, ns, pad_rows, init_bcast, cps=1):
    L = q // ns
    cw, cb, scw, dtb, alog, drow, nw = prm
    steps = nc // cps
    rows = q * cps
    sc_block0 = A_COLS // SC_WIDTH

    def tok(b, c):
        return (b * steps + c, 0)

    def init3(b, c):
        return (0 if init_bcast else b, 0, 0)

    def const2(b, c):
        return (0, 0)

    def per_b(b, c):
        return (b, 0, 0)

    n_seq = nb * ns
    shifts = _shift_matrices(q, ns, SSD_CONV)
    sc_specs = [pl.BlockSpec((rows, SC_WIDTH), lambda b, c, k=k: (b * steps + c, sc_block0 + k))
                for k in range(3)]
    return pl.pallas_call(
        functools.partial(_seqmix_kernel, q, ns, cps, steps == 1 and cps == 1, pad_rows),
        grid=(nb, steps),
        in_specs=[pl.BlockSpec((rows, A_COLS), tok)] + sc_specs + [
            pl.BlockSpec((rows, LANES), tok),
            pl.BlockSpec((ns, SSD_CONV - 1, SSD_CONV_DIM), init3),
            pl.BlockSpec((ns, SC_CONV - 1, SC_WIDTH), init3),
            pl.BlockSpec((ns, SSD_INNER, SSD_STATE), init3),
            pl.BlockSpec(shifts.shape, lambda b, c: (0, 0, 0)),
            pl.BlockSpec((SSD_CONV, SSD_CONV_DIM), const2),
            pl.BlockSpec((1, SSD_CONV_DIM), const2),
            pl.BlockSpec((SC_CONV, SC_WIDTH), const2),
            pl.BlockSpec((1, LANES), const2),
            pl.BlockSpec((1, LANES), const2),
            pl.BlockSpec((1, SSD_INNER), const2),
            pl.BlockSpec((1, SSD_INNER), const2),
        ],
        out_specs=[
            pl.BlockSpec((rows, SSD_INNER), tok),
            pl.BlockSpec((rows, SC_WIDTH), tok),
            pl.BlockSpec((ns, SSD_INNER, SSD_STATE), per_b),
            pl.BlockSpec((ns, SSD_CONV - 1, SSD_CONV_DIM), per_b),
            pl.BlockSpec((ns, SC_CONV - 1, SC_WIDTH), per_b),
        ],
        out_shape=[
            jax.ShapeDtypeStruct((nb * nc * q, SSD_INNER), BF16),
            jax.ShapeDtypeStruct((nb * nc * q, SC_WIDTH), BF16),
            jax.ShapeDtypeStruct((n_seq, SSD_INNER, SSD_STATE), F32),
            jax.ShapeDtypeStruct((n_seq, SSD_CONV - 1, SSD_CONV_DIM), F32),
            jax.ShapeDtypeStruct((n_seq, SC_CONV - 1, SC_WIDTH), F32),
        ],
        scratch_shapes=[
            pltpu.VMEM((ns * SUBLANES, SSD_CONV_DIM), F32),
            pltpu.VMEM((ns, SUBLANES + L, SC_WIDTH), F32),
            pltpu.VMEM((q, SSD_CONV_DIM), F32),
            pltpu.VMEM((q, SSD_INNER), F32),
            pltpu.VMEM((SSD_INNER, LANES), F32),
        ],
        compiler_params=pltpu.CompilerParams(
            dimension_semantics=("arbitrary", "arbitrary"), vmem_limit_bytes=VMEM_LIMIT),
        name="seqmix",
    )(proj, proj, proj, proj, pdt, hist, schist, st0, shifts, cw, cb, scw, dtb, alog, drow, nw)


def _merge_kernel(u_ref, v_ref, g_ref, x_ref, wa_ref, ws_ref, wo_ref, ln2_ref, wr_ref, br_ref,
                  xg_ref, cnt_s):
    i = pl.program_id(0)

    @pl.when(i == 0)
    def _():
        cnt_s[...] = jnp.zeros_like(cnt_s)

    y_ssd = jnp.dot(u_ref[...], wa_ref[...], preferred_element_type=F32)
    y_sc = jnp.dot(v_ref[...], ws_ref[...], preferred_element_type=F32)
    g1 = g_ref[:, 0:D_MODEL].astype(F32)
    g2 = g_ref[:, D_MODEL:2 * D_MODEL].astype(F32)
    merged = (_sigmoid(g1) * y_ssd + _sigmoid(g2) * y_sc).astype(BF16)
    x1 = x_ref[...] + jnp.dot(merged, wo_ref[...], preferred_element_type=F32)
    xg_ref[:, 0:D_MODEL] = x1
    ms = jnp.mean(x1 * x1, axis=-1, keepdims=True)
    h2 = (x1 * lax.rsqrt(ms + EPS) * ln2_ref[...]).astype(BF16)

    logits = jnp.dot(h2, wr_ref[...], preferred_element_type=F32) + br_ref[...]
    lane = lax.broadcasted_iota(jnp.int32, logits.shape, 1).astype(F32)
    big = float(LANES)
    gl = jnp.where(lane < N_EGROUPS, logits, NEG_BIG)
    gmax = jnp.max(gl, axis=-1, keepdims=True)
    g_sel = jnp.min(jnp.where(gl == gmax, lane, big), axis=-1, keepdims=True)
    gsum = jnp.sum(jnp.exp(gl - gmax), axis=-1, keepdims=True)
    g_prob = 1.0 / gsum
    lo = N_EGROUPS + EXPERTS_PER_GROUP * g_sel
    emask = jnp.logical_and(lane >= lo, lane < lo + EXPERTS_PER_GROUP)
    el = jnp.where(emask, logits, NEG_BIG)
    m1 = jnp.max(el, axis=-1, keepdims=True)
    e = jnp.where(emask, jnp.exp(el - m1), -1.0)
    i1 = jnp.min(jnp.where(e == 1.0, lane, big), axis=-1, keepdims=True)
    e_rest = jnp.where(lane == i1, -1.0, e)
    e2 = jnp.max(e_rest, axis=-1, keepdims=True)
    i2 = jnp.min(jnp.where(e_rest == e2, lane, big), axis=-1, keepdims=True)
    denom = 1.0 + e2
    w1 = g_prob / denom
    w2 = g_prob * e2 / denom
    gates = jnp.where(lane == i1, w1, 0.0) + jnp.where(lane == i2, w2, 0.0)

    tm = logits.shape[0]
    rb = min(tm, RANK_BLOCK)
    onehot = jnp.where(lane == g_sel, 1.0, 0.0).astype(BF16)
    ri = lax.broadcasted_iota(jnp.int32, (rb, rb), 0)
    ci = lax.broadcasted_iota(jnp.int32, (rb, rb), 1)
    tril01 = jnp.where(ci <= ri, 1.0, 0.0).astype(BF16)
    carry = cnt_s[...]
    incl_blocks = []
    for r0 in range(0, tm, rb):
        incl_b = jnp.dot(tril01, onehot[r0:r0 + rb], preferred_element_type=F32) + carry
        carry = incl_b[rb - 1:rb, :]
        incl_blocks.append(incl_b)
    cnt_s[...] = carry
    incl = incl_blocks[0] if len(incl_blocks) == 1 else jnp.concatenate(incl_blocks, axis=0)
    rank = jnp.sum(jnp.where(lane == g_sel, incl - 1.0, 0.0), axis=-1, keepdims=True)
    xg_ref[:, D_MODEL:D_MODEL + LANES] = jnp.where(
        lane == 0, g_sel.astype(F32), jnp.where(lane == 1, rank, gates))


def _merge(u, v, g, x, wa, ws, wo, ln2, wr, br, tm):
    t = x.shape[0]
    row = lambda i: (i, 0)
    const = lambda i: (0, 0)
    return pl.pallas_call(
        _merge_kernel,
        grid=(t // tm,),
        in_specs=[
            pl.BlockSpec((tm, SSD_INNER), row),
            pl.BlockSpec((tm, SC_WIDTH), row),
            pl.BlockSpec((tm, G_COLS), lambda i: (i, (A_COLS + S_COLS) // G_COLS)),
            pl.BlockSpec((tm, D_MODEL), row),
            pl.BlockSpec((SSD_INNER, D_MODEL), const, pipeline_mode=pl.Buffered(1)),
            pl.BlockSpec((SC_WIDTH, D_MODEL), const, pipeline_mode=pl.Buffered(1)),
            pl.BlockSpec((D_MODEL, D_MODEL), const, pipeline_mode=pl.Buffered(1)),
            pl.BlockSpec((1, D_MODEL), const),
            pl.BlockSpec((D_MODEL, LANES), const),
            pl.BlockSpec((1, LANES), const),
        ],
        out_specs=pl.BlockSpec((tm, XG_COLS), row),
        out_shape=jax.ShapeDtypeStruct((t, XG_COLS), F32),
        scratch_shapes=[pltpu.VMEM((1, LANES), F32)],
        compiler_params=pltpu.CompilerParams(
            dimension_semantics=("arbitrary",), vmem_limit_bytes=VMEM_LIMIT),
        name="merge",
    )(u, v, g, x, wa, ws, wo, ln2, wr, br)


def _moe_kernel(tmg, inv_ref, tg_ref, nv_ref, xg_hbm, wg_ref, wu_ref, wd_ref, ln2_ref, fw_ref,
                y_hbm, xbuf, ybuf, sem_in, sem_out):
    i = pl.program_id(0)
    n = pl.num_programs(0)
    slot = i % 2

    def row_in(tok, r, slt):
        return pltpu.make_async_copy(xg_hbm.at[pl.ds(tok, 1)], xbuf.at[slt, pl.ds(r, 1)], sem_in.at[slt])

    def row_out(tok, r, slt):
        return pltpu.make_async_copy(ybuf.at[slt, pl.ds(r, 1)], y_hbm.at[pl.ds(tok, 1)], sem_out.at[slt])

    unroll = 8

    def gather(tile, slt):
        def body(r, carry):
            row_in(inv_ref[tile * tmg + r], r, slt).start()
            return carry
        lax.fori_loop(0, tmg, body, 0, unroll=unroll)

    def wait_gather(slt):
        pltpu.make_async_copy(xg_hbm.at[pl.ds(0, tmg)], xbuf.at[slt], sem_in.at[slt]).wait()

    def scatter(tile, slt):
        def body(r, carry):
            row_out(inv_ref[tile * tmg + r], r, slt).start()
            return carry
        full = nv_ref[tile] == tmg

        @pl.when(full)
        def _():
            for r in range(tmg):
                body(r, 0)

        @pl.when(jnp.logical_not(full))
        def _():
            lax.fori_loop(0, nv_ref[tile], body, 0)

    def wait_scatter(tile, slt):
        def body(r, carry):
            row_out(0, r, slt).wait()
            return carry
        full = nv_ref[tile] == tmg

        @pl.when(full)
        def _():
            pltpu.make_async_copy(ybuf.at[slt], y_hbm.at[pl.ds(0, tmg)], sem_out.at[slt]).wait()

        @pl.when(jnp.logical_not(full))
        def _():
            lax.fori_loop(0, nv_ref[tile], body, 0)

    @pl.when(i == 0)
    def _():
        gather(0, 0)

    nxt = jnp.minimum(i + 1, n - 1)

    @pl.when(i >= 2)
    def _():
        wait_scatter(jnp.maximum(i - 2, 0), slot)

    @pl.when(nv_ref[i] == 0)
    def _():
        gather(nxt, 1 - slot)
        wait_gather(slot)

    @pl.when(nv_ref[i] > 0)
    def _():
        wait_gather(slot)
        x1 = xbuf[slot, :, 0:D_MODEL]
        for r in range(tmg // 2):
            row_in(inv_ref[nxt * tmg + r], r, 1 - slot).start()
        gate = xbuf[slot, :, D_MODEL:D_MODEL + LANES]
        for r in range(tmg // 2, tmg):
            row_in(inv_ref[nxt * tmg + r], r, 1 - slot).start()
        x1_res = xbuf[slot, :, 0:D_MODEL]
        ms = jnp.mean(x1 * x1, axis=-1, keepdims=True)
        h2 = (x1 * lax.rsqrt(ms + EPS) * ln2_ref[...]).astype(BF16)
        lane = lax.broadcasted_iota(jnp.int32, gate.shape, 1)
        first = N_EGROUPS + EXPERTS_PER_GROUP * tg_ref[i]
        pieces = []
        for e in range(EXPERTS_PER_GROUP):
            a = jnp.dot(h2, wg_ref[e], preferred_element_type=F32)
            up = jnp.dot(h2, wu_ref[e], preferred_element_type=F32)
            ge = jnp.sum(jnp.where(lane == first + e, gate, 0.0), axis=-1, keepdims=True)
            pieces.append((a * _sigmoid(a) * up * ge).astype(BF16))
        mo = jnp.dot(jnp.concatenate(pieces, axis=1), wd_ref[...], preferred_element_type=F32)
        x2 = x1_res + mo
        ms2 = jnp.mean(x2 * x2, axis=-1, keepdims=True)
        ybuf[slot] = x2 * lax.rsqrt(ms2 + EPS) * fw_ref[...]
        scatter(i, slot)

    @pl.when(i == n - 1)
    def _():
        wait_gather(1 - slot)

        @pl.when(i >= 1)
        def _():
            wait_scatter(jnp.maximum(i - 1, 0), 1 - slot)
        wait_scatter(i, slot)


def _moe(xg, inv, tile_group, n_valid, wg, wu, wd, ln2, fw, tmg):
    t = xg.shape[0]
    n_tiles = tile_group.shape[0]
    gcols = EXPERTS_PER_GROUP * D_EXPERT
    const = lambda i, inv, tg, nv: (0, 0)
    by_group = lambda i, inv, tg, nv: (tg[i], 0, 0)
    return pl.pallas_call(
        functools.partial(_moe_kernel, tmg),
        grid_spec=pltpu.PrefetchScalarGridSpec(
            num_scalar_prefetch=3,
            grid=(n_tiles,),
            in_specs=[
                pl.BlockSpec(memory_space=pltpu.HBM),
                pl.BlockSpec((EXPERTS_PER_GROUP, D_MODEL, D_EXPERT), by_group),
                pl.BlockSpec((EXPERTS_PER_GROUP, D_MODEL, D_EXPERT), by_group),
                pl.BlockSpec((None, gcols, D_MODEL), by_group),
                pl.BlockSpec((1, D_MODEL), const),
                pl.BlockSpec((1, D_MODEL), const),
            ],
            out_specs=pl.BlockSpec(memory_space=pltpu.HBM),
            scratch_shapes=[
                pltpu.VMEM((2, tmg, XG_COLS), F32),
                pltpu.VMEM((2, tmg, D_MODEL), F32),
                pltpu.SemaphoreType.DMA((2,)),
                pltpu.SemaphoreType.DMA((2,)),
            ],
        ),
        out_shape=jax.ShapeDtypeStruct((t, D_MODEL), F32),
        compiler_params=pltpu.CompilerParams(
            dimension_semantics=("arbitrary",), vmem_limit_bytes=VMEM_LIMIT),
        name="moe",
    )(inv, tile_group, n_valid, xg, wg, wu, wd, ln2, fw)


def _route(xg, tmg):
    t = xg.shape[0]
    g = xg[:, D_MODEL].astype(jnp.int32)
    rank = xg[:, D_MODEL + 1].astype(jnp.int32)
    counts = jnp.sum(g[:, None] == jnp.arange(N_EGROUPS, dtype=jnp.int32)[None, :], axis=0, dtype=jnp.int32)
    tiles_per = (counts + tmg - 1) // tmg
    tile_end = jnp.cumsum(tiles_per)
    tile_base = tile_end - tiles_per
    n_tiles = t // tmg + N_EGROUPS
    pos = tile_base[g] * tmg + rank
    inv = jnp.zeros((n_tiles * tmg,), jnp.int32).at[pos].set(
        jnp.arange(t, dtype=jnp.int32), unique_indices=True, mode="promise_in_bounds")
    ti = jnp.arange(n_tiles, dtype=jnp.int32)
    tg = jnp.minimum(jnp.sum(ti[:, None] >= tile_end[None, :], axis=1, dtype=jnp.int32), N_EGROUPS - 1)
    nv = jnp.clip(counts[tg] - (ti - tile_base[tg]) * tmg, 0, tmg)
    nv = jnp.where(ti < tile_end[-1], nv, 0).astype(jnp.int32)
    return inv, tg, nv


def _pad_lanes(v, n=LANES):
    v = v.reshape(1, -1).astype(F32)
    return jnp.pad(v, ((0, 0), (0, n - v.shape[1])))


def _cast_kernel(a_ref, b_ref, c_ref, oa_ref, ob_ref, oc_ref):
    oa_ref[...] = a_ref[...].astype(BF16)
    ob_ref[...] = b_ref[...].astype(BF16)
    oc_ref[...] = c_ref[...].astype(BF16)


def _cast_expert_weights(wg, wu, wd, eb=4):
    n = wg.shape[0]
    specs = [pl.BlockSpec((eb,) + w.shape[1:], lambda i: (i, 0, 0)) for w in (wg, wu, wd)]
    return pl.pallas_call(
        _cast_kernel,
        grid=(n // eb,),
        in_specs=specs,
        out_specs=specs,
        out_shape=[jax.ShapeDtypeStruct(w.shape, BF16) for w in (wg, wu, wd)],
        compiler_params=pltpu.CompilerParams(
            dimension_semantics=("arbitrary",), vmem_limit_bytes=VMEM_LIMIT),
        name="castw",
    )(wg, wu, wd)


def kernel(x_prompt, x_sample, state_ssm, state_ssd_conv, state_short_conv, meta_tokens, ln1_w, w_in,
           ssd_conv_w, ssd_conv_b, ssd_dt_bias, ssd_A_log, ssd_D, ssd_norm_w, w_ssd_out, sc_conv_w,
           w_sc_out, w_o, ln2_w, w_rg, b_rg, w_re, b_re, w_gate, w_up, w_down, final_norm_w):
    assert ln1_w.shape[0] == 1, "single-layer trunk"
    bp, sp, _ = x_prompt.shape
    bs, ss, _ = x_sample.shape
    l = 0

    wt = jnp.swapaxes(w_in[l], 0, 1)
    wdt = jnp.pad(wt[A_COLS:A_COLS + SSD_HEADS], ((0, LANES - SSD_HEADS), (0, 0))).astype(BF16)
    lnw = ln1_w[l].reshape(1, D_MODEL)
    prm = (ssd_conv_w[l], ssd_conv_b[l].reshape(1, -1), sc_conv_w[l],
           _pad_lanes(ssd_dt_bias[l]), _pad_lanes(ssd_A_log[l]),
           jnp.repeat(ssd_D[l].astype(F32), SSD_HEAD_DIM).reshape(1, SSD_INNER),
           ssd_norm_w[l].reshape(1, SSD_INNER))
    wa = w_ssd_out[l].astype(BF16)
    ws = w_sc_out[l].astype(BF16)
    wo = w_o[l].astype(BF16)
    ln2 = ln2_w[l].reshape(1, D_MODEL)
    wr = jnp.pad(jnp.concatenate([w_rg[l], w_re[l]], axis=1),
                 ((0, 0), (0, LANES - N_EGROUPS - N_EXPERTS))).astype(BF16)
    br = _pad_lanes(jnp.concatenate([b_rg[l], b_re[l]]))
    gcols = EXPERTS_PER_GROUP * D_EXPERT
    wg, wu, wd = _cast_expert_weights(w_gate[l].reshape(N_EXPERTS, D_MODEL, D_EXPERT),
                                      w_up[l].reshape(N_EXPERTS, D_MODEL, D_EXPERT),
                                      w_down[l].reshape(N_EXPERTS, D_EXPERT, D_MODEL))
    wd = wd.reshape(N_EGROUPS, gcols, D_MODEL)
    fw = final_norm_w.reshape(1, D_MODEL)

    def mlp_tail(u, v, g, x, tmg):
        xg = _merge(u, v, g, x, wa, ws, wo, ln2, wr, br, tm=1024)
        inv, tg, nv = _route(xg, tmg)
        return _moe(xg, inv, tg, nv, wg, wu, wd, ln2, fw, tmg)

    q = LANES
    xm = jnp.pad(meta_tokens.astype(F32), ((q - N_META, 0), (0, 0)))
    xs = x_sample.reshape(bs * ss, D_MODEL)
    (sproj, sdt), (mproj, mdt) = _inproj(xs, lnw, wt, wdt, tm=bs * ss, extra=xm)
    zeros_hist = jnp.zeros((1, SSD_CONV - 1, SSD_CONV_DIM), F32)
    zeros_sch = jnp.zeros((1, SC_CONV - 1, SC_WIDTH), F32)
    zeros_st = jnp.zeros((1, SSD_INNER, SSD_STATE), F32)
    _, _, m_st, m_cs, m_scs = _seqmix(mproj, mdt, zeros_hist, zeros_sch, zeros_st, prm,
                                      nb=1, nc=1, q=q, ns=1, pad_rows=q - N_META, init_bcast=True)

    xp = x_prompt.reshape(bp * sp, D_MODEL)
    pproj, pdt = _inproj(xp, lnw, wt, wdt, tm=2048)
    pu, pv, p_st, p_cs, p_scs = _seqmix(pproj, pdt, m_cs, m_scs, m_st, prm, nb=bp, nc=sp // q, q=q,
                                        ns=1, pad_rows=0, init_bcast=True, cps=4)
    y_prompt = mlp_tail(pu, pv, pproj, xp, MOE_TILE).reshape(bp, sp, D_MODEL)

    ns = 8
    su, sv, s_st, s_cs, s_scs = _seqmix(sproj, sdt, state_ssd_conv[l], state_short_conv[l],
                                        state_ssm[l].reshape(bs, SSD_INNER, SSD_STATE), prm,
                                        nb=bs // ns, nc=1, q=ns * ss, ns=ns, pad_rows=0, init_bcast=False)
    y_sample = mlp_tail(su, sv, sproj, xs, MOE_TILE_SMALL).reshape(bs, ss, D_MODEL)

    hshape = (SSD_HEADS, SSD_HEAD_DIM, SSD_STATE)
    return (y_prompt, y_sample,
            p_st.reshape(1, bp, *hshape), p_cs[None], p_scs[None],
            s_st.reshape(1, bs, *hshape), s_cs[None], s_scs[None])
```

```python
import functools

import jax
import jax.numpy as jnp
import numpy as np
from jax import lax
from jax.experimental import pallas as pl
from jax.experimental.pallas import tpu as pltpu

F32 = jnp.float32
BF16 = jnp.bfloat16

D_MODEL = 1024
N_META = 16
SSD_INNER = 2048
SSD_HEAD_DIM = 64
SSD_HEADS = 32
SSD_GROUPS = 4
SSD_HPG = 8
SSD_STATE = 128
SSD_CONV = 4
SSD_CONV_DIM = 3072
SC_WIDTH = 1024
SC_CONV = 3
N_EGROUPS = 4
EXPERTS_PER_GROUP = 8
N_EXPERTS = 32
D_EXPERT = 256
EPS = 1e-6

LANES = 128
SUBLANES = 8
GROUP_COLS = SSD_HPG * SSD_HEAD_DIM
A_COLS = SSD_INNER + SSD_CONV_DIM
S_COLS = 3 * SC_WIDTH
G_COLS = 2 * D_MODEL
PROJ_COLS = A_COLS + S_COLS + G_COLS
XG_COLS = D_MODEL + LANES
MOE_TILE = 256
MOE_TILE_SMALL = 128
RANK_BLOCK = 512
NEG_BIG = -1e30
LOG2E = 1.4426950408889634
VMEM_LIMIT = 56 * 1024 * 1024


def _nt_dot(a, b):
    return lax.dot_general(a, b, (((1,), (1,)), ((), ())), preferred_element_type=F32)


def _dot01(m01_bf16, x):
    hi = x.astype(BF16)
    r1 = x - hi.astype(F32)
    mid = r1.astype(BF16)
    lo = (r1 - mid.astype(F32)).astype(BF16)
    out = jnp.dot(m01_bf16, hi, preferred_element_type=F32)
    out = out + jnp.dot(m01_bf16, mid, preferred_element_type=F32)
    return out + jnp.dot(m01_bf16, lo, preferred_element_type=F32)


def _softplus(x):
    return jnp.maximum(x, 0.0) + jnp.log1p(jnp.exp(-jnp.abs(x)))


def _sigmoid(x):
    return 1.0 / (1.0 + jnp.exp(-x))


def _inproj_kernel(n_sets, *refs):
    x_refs = refs[:n_sets]
    lnw_ref, wt_ref, wdt_ref = refs[n_sets:n_sets + 3]
    outs = refs[n_sets + 3:n_sets + 3 + 2 * n_sets]
    h_refs = refs[n_sets + 3 + 2 * n_sets:]

    @pl.when(pl.program_id(1) == 0)
    def _():
        for k in range(n_sets):
            x = x_refs[k][...]
            ms = jnp.mean(x * x, axis=-1, keepdims=True)
            h = (x * lax.rsqrt(ms + EPS) * lnw_ref[...]).astype(BF16)
            h_refs[k][...] = h
            outs[2 * k + 1][...] = _nt_dot(h, wdt_ref[...])

    w = wt_ref[...].astype(BF16)
    for k in range(n_sets):
        outs[2 * k][...] = _nt_dot(h_refs[k][...], w).astype(BF16)


def _inproj(x, lnw, wt, wdt, tm, tn=1024, extra=None):
    t = x.shape[0]
    n_a = A_COLS // tn
    nj = PROJ_COLS // tn

    def w_rows(i, j):
        return (pl.multiple_of(jnp.where(j < n_a, j * tn, j * tn + SSD_HEADS), SSD_HEADS), 0)

    sets = [(x, tm, lambda i: i)]
    if extra is not None:
        assert t == tm, "extra rows are recomputed per row tile"
        sets.append((extra, extra.shape[0], lambda i: 0))
    x_specs, out_specs, out_shape, scratch = [], [], [], []
    for xs_, rows, ri in sets:
        x_specs.append(pl.BlockSpec((rows, D_MODEL), lambda i, j, ri=ri: (ri(i), 0)))
        out_specs += [pl.BlockSpec((rows, tn), lambda i, j, ri=ri: (ri(i), j)),
                      pl.BlockSpec((rows, LANES), lambda i, j, ri=ri: (ri(i), 0))]
        out_shape += [jax.ShapeDtypeStruct((xs_.shape[0], PROJ_COLS), BF16),
                      jax.ShapeDtypeStruct((xs_.shape[0], LANES), F32)]
        scratch.append(pltpu.VMEM((rows, D_MODEL), BF16))
    res = pl.pallas_call(
        functools.partial(_inproj_kernel, len(sets)),
        grid=(t // tm, nj),
        in_specs=x_specs + [
            pl.BlockSpec((1, D_MODEL), lambda i, j: (0, 0)),
            pl.BlockSpec((pl.Element(tn), pl.Element(D_MODEL)), w_rows),
            pl.BlockSpec((LANES, D_MODEL), lambda i, j: (0, 0)),
        ],
        out_specs=out_specs,
        out_shape=out_shape,
        scratch_shapes=scratch,
        compiler_params=pltpu.CompilerParams(
            dimension_semantics=("arbitrary", "arbitrary"), vmem_limit_bytes=VMEM_LIMIT),
        name="inproj",
    )(*[st[0] for st in sets], lnw, wt, wdt)
    return (res[:2], res[2:]) if extra is not None else res


def halo_rows(ns):
    return max(2 * SUBLANES, ns * SUBLANES)


def _shift_matrices(q, ns, taps):
    L = q // ns
    hb = halo_rows(ns)
    r = np.arange(q)[:, None]
    col = np.arange(3 * hb + q)[None, :]
    seq, t = r // L, r % L
    mats = []
    for k in range(taps - 1):
        d = taps - 1 - k
        from_cur = (col >= 3 * hb) & (col - 3 * hb == r - d) & (t >= d)
        from_hist = (col < 3 * hb) & (col % hb == seq * SUBLANES + SUBLANES - d + t) & (t < d)
        mats.append(from_cur | from_hist)
    return jnp.asarray(np.stack(mats), dtype=BF16)


def _seqmix_kernel(Q, NS, CPS, one_step, pad_rows,
                   a_blk, sb_blk, sc_blk, sh_blk, dt_blk, hist_ref, schist_ref, st0_ref, shift_ref,
                   cw_ref, cb_ref, scw_ref, dtb_ref, alog_ref, drow_ref, nw_ref,
                   u_blk, v_blk, st_ref, cs_ref, scs_ref,
                   halo, scpad, xbc_s, y_s, xst_s):
    @pl.when(pl.program_id(1) == 0)
    def _():
        halo[...] = jnp.zeros_like(halo)
        for s in range(NS):
            halo[(s + 1) * SUBLANES - (SSD_CONV - 1):(s + 1) * SUBLANES, :] = hist_ref[s]
            scpad[s, SUBLANES - (SC_CONV - 1):SUBLANES, :] = schist_ref[s]
        if not one_step:
            st_ref[...] = st0_ref[...]

    src_ref = st0_ref if one_step else st_ref
    shared = (shift_ref, cw_ref, cb_ref, scw_ref, dtb_ref, alog_ref, drow_ref, nw_ref,
              src_ref, st_ref, cs_ref, scs_ref, halo, scpad, xbc_s, y_s, xst_s)
    row_blocks = (a_blk, sb_blk, sc_blk, sh_blk, dt_blk, u_blk, v_blk)
    if CPS == 1:
        _seqmix_chunk(Q, NS, pad_rows, *row_blocks, *shared)
    else:
        def body(k, carry):
            rows = pl.ds(pl.multiple_of(k * Q, Q), Q)
            _seqmix_chunk(Q, NS, pad_rows, *[r.at[rows] for r in row_blocks], *shared)
            return carry
        lax.fori_loop(0, CPS, body, 0)


def _seqmix_chunk(Q, NS, pad_rows, a_ref, sb_ref, sc_ref, sh_ref, dt_ref, u_ref, v_ref,
                  shift_ref, cw_ref, cb_ref, scw_ref, dtb_ref, alog_ref, drow_ref, nw_ref,
                  src_ref, st_ref, cs_ref, scs_ref, halo, scpad, xbc_s, y_s, xst_s):
    L = Q // NS
    lg = L.bit_length() - 1

    CW = 256
    hb = halo_rows(NS)
    for cc in range(0, SSD_CONV_DIM, CW):
        raw_b = a_ref[:, SSD_INNER + cc:SSD_INNER + cc + CW]
        hl = halo[:, cc:cc + CW]
        if hb > hl.shape[0]:
            hl = jnp.concatenate([hl, jnp.zeros((hb - hl.shape[0], CW), F32)], axis=0)
        h_hi = hl.astype(BF16)
        h_r = hl - h_hi.astype(F32)
        h_mid = h_r.astype(BF16)
        h_lo = (h_r - h_mid.astype(F32)).astype(BF16)
        ext = jnp.concatenate([h_hi, h_mid, h_lo, raw_b], axis=0)
        raw = raw_b.astype(F32)
        acc = raw * cw_ref[3:4, cc:cc + CW] + cb_ref[:, cc:cc + CW]
        for k in range(SSD_CONV - 1):
            acc = acc + jnp.dot(shift_ref[k], ext, preferred_element_type=F32) * cw_ref[k:k + 1, cc:cc + CW]
        xbc_s[:, cc:cc + CW] = acc * _sigmoid(acc)
        for s in range(NS):
            halo[s * SUBLANES:(s + 1) * SUBLANES, cc:cc + CW] = raw[(s + 1) * L - SUBLANES:(s + 1) * L]
    for s in range(NS):
        cs_ref[s] = halo[(s + 1) * SUBLANES - (SSD_CONV - 1):(s + 1) * SUBLANES, :]

    for cc in range(0, SC_WIDTH, CW):
        scb = sb_ref[:, cc:cc + CW].astype(F32)
        ch_all = sc_ref[:, cc:cc + CW].astype(F32) * sh_ref[:, cc:cc + CW].astype(F32)
        for s in range(NS):
            ch = ch_all[s * L:(s + 1) * L]
            scpad[s, SUBLANES:SUBLANES + L, cc:cc + CW] = ch
            acc = ch * scw_ref[2:3, cc:cc + CW]
            for k in range(SC_CONV - 1):
                acc = acc + scpad[s, 6 + k:6 + k + L, cc:cc + CW] * scw_ref[k:k + 1, cc:cc + CW]
            v_ref[s * L:(s + 1) * L, cc:cc + CW] = (scb[s * L:(s + 1) * L] * acc).astype(BF16)
    for s in range(NS):
        scs_ref[s] = scpad[s, L + 6:L + 8, :]
        scpad[s, 0:SUBLANES, :] = scpad[s, L:L + SUBLANES, :]

    def padrows(x):
        if Q == LANES:
            return x
        return jnp.concatenate([x, jnp.zeros((LANES - Q, x.shape[1]), x.dtype)], axis=0)

    li = lax.broadcasted_iota(jnp.int32, (Q, Q), 0)
    si = lax.broadcasted_iota(jnp.int32, (Q, Q), 1)
    same = (li >> lg) == (si >> lg)
    causal = jnp.logical_and(same, si <= li)
    tril01 = jnp.where(causal, 1.0, 0.0).astype(BF16)
    same01 = jnp.where(same, 1.0, 0.0).astype(BF16)

    dt = _softplus(dt_ref[...] + dtb_ref[...])
    if pad_rows:
        ri = lax.broadcasted_iota(jnp.int32, (Q, LANES), 0)
        dt = jnp.where(ri >= pad_rows, dt, 0.0)
    da = dt * (-jnp.exp(alog_ref[...]))
    acum = _dot01(tril01, da)
    tot = _dot01(same01, da)
    acum_t = padrows(acum).T
    tot_t = padrows(tot).T
    dt_t = padrows(dt).T
    w_t = jnp.exp(tot_t - acum_t) * dt_t
    acum2 = acum * LOG2E
    rowq_t = acum_t * LOG2E - jnp.log2(dt_t)

    left_head = lax.broadcasted_iota(jnp.int32, (Q, LANES), 1) < SSD_HEAD_DIM
    rowseq = lax.broadcasted_iota(jnp.int32, (Q, GROUP_COLS), 0) >> lg
    rowseq_p = lax.broadcasted_iota(jnp.int32, (LANES, LANES), 0) >> lg

    for g in range(SSD_GROUPS):
        b_g = xbc_s[:, SSD_INNER + g * SSD_STATE:SSD_INNER + (g + 1) * SSD_STATE]
        c_g = xbc_s[:, SSD_INNER + GROUP_COLS + g * SSD_STATE:SSD_INNER + GROUP_COLS + (g + 1) * SSD_STATE]
        b_gb = b_g.astype(BF16)
        c_gb = c_g.astype(BF16)
        cbm = _nt_dot(c_gb, b_gb)
        yo = None
        for s in range(NS):
            h_s = src_ref[s, g * GROUP_COLS:(g + 1) * GROUP_COLS, :].astype(BF16)
            yo_s = _nt_dot(c_gb, h_s)
            yo = yo_s if yo is None else jnp.where(rowseq == s, yo_s, yo)
        for rp in range(SSD_HPG // 2):
            h0 = g * SSD_HPG + 2 * rp
            cols = slice(h0 * SSD_HEAD_DIM, (h0 + 2) * SSD_HEAD_DIM)
            colbs, w_pair = [], []
            for h in (h0, h0 + 1):
                colb = jnp.broadcast_to(acum2[:, h:h + 1], (Q, LANES))
                rowb = jnp.broadcast_to(rowq_t[h:h + 1, 0:Q], (Q, Q))
                dec_dt = jnp.exp2(jnp.where(causal, colb[:, 0:Q] - rowb, NEG_BIG))
                w_pair.append((cbm * dec_dt).astype(BF16))
                colbs.append(colb)
            x_p = xbc_s[:, cols]
            rhs = jnp.concatenate([jnp.where(left_head, x_p, 0.0).astype(BF16),
                                   jnp.where(left_head, 0.0, x_p).astype(BF16)], axis=0)
            yd = jnp.dot(jnp.concatenate(w_pair, axis=1), rhs, preferred_element_type=F32)
            ecol = jnp.exp2(jnp.where(left_head, colbs[0], colbs[1]))
            y_s[:, cols] = yd + yo[:, rp * LANES:(rp + 1) * LANES] * ecol + drow_ref[:, cols] * x_p

    for jb in range(SSD_INNER // LANES):
        xst_s[jb * LANES:(jb + 1) * LANES, :] = padrows(xbc_s[:, jb * LANES:(jb + 1) * LANES]).T
    for s in range(NS):
        da_b = jnp.exp(jnp.broadcast_to(tot_t[:, s * L:s * L + 1], (LANES, LANES)))
        for g in range(SSD_GROUPS):
            b_p = padrows(xbc_s[:, SSD_INNER + g * SSD_STATE:SSD_INNER + (g + 1) * SSD_STATE])
            if NS > 1:
                b_p = jnp.where(rowseq_p == s, b_p, 0.0)
            pieces = []
            for r in range(SSD_HPG):
                h = g * SSD_HPG + r
                pieces.append(xst_s[h * SSD_HEAD_DIM:(h + 1) * SSD_HEAD_DIM, :] * w_t[h:h + 1, :])
            xw_t = jnp.concatenate(pieces, axis=0).astype(BF16)
            upd = jnp.dot(xw_t, b_p.astype(BF16), preferred_element_type=F32)
            for r in range(SSD_HPG):
                h = g * SSD_HPG + r
                rows = slice(h * SSD_HEAD_DIM, (h + 1) * SSD_HEAD_DIM)
                dec_h = jnp.broadcast_to(da_b[h:h + 1, :], (SSD_HEAD_DIM, SSD_STATE))
                st_ref[s, rows, :] = dec_h * src_ref[s, rows, :] + upd[r * SSD_HEAD_DIM:(r + 1) * SSD_HEAD_DIM, :]

    for g in range(SSD_GROUPS):
        cols = slice(g * GROUP_COLS, (g + 1) * GROUP_COLS)
        z = a_ref[:, cols].astype(F32)
        ug = y_s[:, cols] * (z * _sigmoid(z))
        ms = jnp.mean(ug * ug, axis=-1, keepdims=True)
        u_ref[:, cols] = (ug * lax.rsqrt(ms + EPS) * nw_ref[:, cols]).astype(BF16)


def _seqmix(proj, pdt, hist, schist, st0, prm, *, nb, nc, q, ns, pad_rows, init_bcast, cps=1):
    L = q // ns
    cw, cb, scw, dtb, alog, drow, nw = prm
    steps = nc // cps
    rows = q * cps
    sc_block0 = A_COLS // SC_WIDTH

    def tok(b, c):
        return (b * steps + c, 0)

    def init3(b, c):
        return (0 if init_bcast else b, 0, 0)

    def const2(b, c):
        return (0, 0)

    def per_b(b, c):
        return (b, 0, 0)

    n_seq = nb * ns
    shifts = _shift_matrices(q, ns, SSD_CONV)
    sc_specs = [pl.BlockSpec((rows, SC_WIDTH), lambda b, c, k=k: (b * steps + c, sc_block0 + k))
                for k in range(3)]
    return pl.pallas_call(
        functools.partial(_seqmix_kernel, q, ns, cps, steps == 1 and cps == 1, pad_rows),
        grid=(nb, steps),
        in_specs=[pl.BlockSpec((rows, A_COLS), tok)] + sc_specs + [
            pl.BlockSpec((rows, LANES), tok),
            pl.BlockSpec((ns, SSD_CONV - 1, SSD_CONV_DIM), init3),
            pl.BlockSpec((ns, SC_CONV - 1, SC_WIDTH), init3),
            pl.BlockSpec((ns, SSD_INNER, SSD_STATE), init3),
            pl.BlockSpec(shifts.shape, lambda b, c: (0, 0, 0)),
            pl.BlockSpec((SSD_CONV, SSD_CONV_DIM), const2),
            pl.BlockSpec((1, SSD_CONV_DIM), const2),
            pl.BlockSpec((SC_CONV, SC_WIDTH), const2),
            pl.BlockSpec((1, LANES), const2),
            pl.BlockSpec((1, LANES), const2),
            pl.BlockSpec((1, SSD_INNER), const2),
            pl.BlockSpec((1, SSD_INNER), const2),
        ],
        out_specs=[
            pl.BlockSpec((rows, SSD_INNER), tok),
            pl.BlockSpec((rows, SC_WIDTH), tok),
            pl.BlockSpec((ns, SSD_INNER, SSD_STATE), per_b),
            pl.BlockSpec((ns, SSD_CONV - 1, SSD_CONV_DIM), per_b),
            pl.BlockSpec((ns, SC_CONV - 1, SC_WIDTH), per_b),
        ],
        out_shape=[
            jax.ShapeDtypeStruct((nb * nc * q, SSD_INNER), BF16),
            jax.ShapeDtypeStruct((nb * nc * q, SC_WIDTH), BF16),
            jax.ShapeDtypeStruct((n_seq, SSD_INNER, SSD_STATE), F32),
            jax.ShapeDtypeStruct((n_seq, SSD_CONV - 1, SSD_CONV_DIM), F32),
            jax.ShapeDtypeStruct((n_seq, SC_CONV - 1, SC_WIDTH), F32),
        ],
        scratch_shapes=[
            pltpu.VMEM((ns * SUBLANES, SSD_CONV_DIM), F32),
            pltpu.VMEM((ns, SUBLANES + L, SC_WIDTH), F32),
            pltpu.VMEM((q, SSD_CONV_DIM), F32),
            pltpu.VMEM((q, SSD_INNER), F32),
            pltpu.VMEM((SSD_INNER, LANES), F32),
        ],
        compiler_params=pltpu.CompilerParams(
            dimension_semantics=("arbitrary", "arbitrary"), vmem_limit_bytes=VMEM_LIMIT),
        name="seqmix",
    )(proj, proj, proj, proj, pdt, hist, schist, st0, shifts, cw, cb, scw, dtb, alog, drow, nw)


def _merge_kernel(u_ref, v_ref, g_ref, x_ref, wa_ref, ws_ref, wo_ref, ln2_ref, wr_ref, br_ref,
                  xg_ref, cnt_s):
    i = pl.program_id(0)

    @pl.when(i == 0)
    def _():
        cnt_s[...] = jnp.zeros_like(cnt_s)

    y_ssd = jnp.dot(u_ref[...], wa_ref[...], preferred_element_type=F32)
    y_sc = jnp.dot(v_ref[...], ws_ref[...], preferred_element_type=F32)
    g1 = g_ref[:, 0:D_MODEL].astype(F32)
    g2 = g_ref[:, D_MODEL:2 * D_MODEL].astype(F32)
    merged = (_sigmoid(g1) * y_ssd + _sigmoid(g2) * y_sc).astype(BF16)
    x1 = x_ref[...] + jnp.dot(merged, wo_ref[...], preferred_element_type=F32)
    xg_ref[:, 0:D_MODEL] = x1
    ms = jnp.mean(x1 * x1, axis=-1, keepdims=True)
    h2 = (x1 * lax.rsqrt(ms + EPS) * ln2_ref[...]).astype(BF16)

    logits = jnp.dot(h2, wr_ref[...], preferred_element_type=F32) + br_ref[...]
    lane = lax.broadcasted_iota(jnp.int32, logits.shape, 1).astype(F32)
    big = float(LANES)
    gl = jnp.where(lane < N_EGROUPS, logits, NEG_BIG)
    gmax = jnp.max(gl, axis=-1, keepdims=True)
    g_sel = jnp.min(jnp.where(gl == gmax, lane, big), axis=-1, keepdims=True)
    gsum = jnp.sum(jnp.exp(gl - gmax), axis=-1, keepdims=True)
    g_prob = 1.0 / gsum
    lo = N_EGROUPS + EXPERTS_PER_GROUP * g_sel
    emask = jnp.logical_and(lane >= lo, lane < lo + EXPERTS_PER_GROUP)
    el = jnp.where(emask, logits, NEG_BIG)
    m1 = jnp.max(el, axis=-1, keepdims=True)
    e = jnp.where(emask, jnp.exp(el - m1), -1.0)
    i1 = jnp.min(jnp.where(e == 1.0, lane, big), axis=-1, keepdims=True)
    e_rest = jnp.where(lane == i1, -1.0, e)
    e2 = jnp.max(e_rest, axis=-1, keepdims=True)
    i2 = jnp.min(jnp.where(e_rest == e2, lane, big), axis=-1, keepdims=True)
    denom = 1.0 + e2
    w1 = g_prob / denom
    w2 = g_prob * e2 / denom
    gates = jnp.where(lane == i1, w1, 0.0) + jnp.where(lane == i2, w2, 0.0)

    tm = logits.shape[0]
    rb = min(tm, RANK_BLOCK)
    onehot = jnp.where(lane == g_sel, 1.0, 0.0).astype(BF16)
    ri = lax.broadcasted_iota(jnp.int32, (rb, rb), 0)
    ci = lax.broadcasted_iota(jnp.int32, (rb, rb), 1)
    tril01 = jnp.where(ci <= ri, 1.0, 0.0).astype(BF16)
    carry = cnt_s[...]
    incl_blocks = []
    for r0 in range(0, tm, rb):
        incl_b = jnp.dot(tril01, onehot[r0:r0 + rb], preferred_element_type=F32) + carry
        carry = incl_b[rb - 1:rb, :]
        incl_blocks.append(incl_b)
    cnt_s[...] = carry
    incl = incl_blocks[0] if len(incl_blocks) == 1 else jnp.concatenate(incl_blocks, axis=0)
    rank = jnp.sum(jnp.where(lane == g_sel, incl - 1.0, 0.0), axis=-1, keepdims=True)
    xg_ref[:, D_MODEL:D_MODEL + LANES] = jnp.where(
        lane == 0, g_sel.astype(F32), jnp.where(lane == 1, rank, gates))


def _merge(u, v, g, x, wa, ws, wo, ln2, wr, br, tm):
    t = x.shape[0]
    row = lambda i: (i, 0)
    const = lambda i: (0, 0)
    return pl.pallas_call(
        _merge_kernel,
        grid=(t // tm,),
        in_specs=[
            pl.BlockSpec((tm, SSD_INNER), row),
            pl.BlockSpec((tm, SC_WIDTH), row),
            pl.BlockSpec((tm, G_COLS), lambda i: (i, (A_COLS + S_COLS) // G_COLS)),
            pl.BlockSpec((tm, D_MODEL), row),
            pl.BlockSpec((SSD_INNER, D_MODEL), const, pipeline_mode=pl.Buffered(1)),
            pl.BlockSpec((SC_WIDTH, D_MODEL), const, pipeline_mode=pl.Buffered(1)),
            pl.BlockSpec((D_MODEL, D_MODEL), const, pipeline_mode=pl.Buffered(1)),
            pl.BlockSpec((1, D_MODEL), const),
            pl.BlockSpec((D_MODEL, LANES), const),
            pl.BlockSpec((1, LANES), const),
        ],
        out_specs=pl.BlockSpec((tm, XG_COLS), row),
        out_shape=jax.ShapeDtypeStruct((t, XG_COLS), F32),
        scratch_shapes=[pltpu.VMEM((1, LANES), F32)],
        compiler_params=pltpu.CompilerParams(
            dimension_semantics=("arbitrary",), vmem_limit_bytes=VMEM_LIMIT),
        name="merge",
    )(u, v, g, x, wa, ws, wo, ln2, wr, br)


def _moe_kernel(tmg, inv_ref, tg_ref, nv_ref, xg_hbm, wg_ref, wu_ref, wd_ref, ln2_ref, fw_ref,
                y_hbm, xbuf, ybuf, sem_in, sem_out):
    i = pl.program_id(0)
    n = pl.num_programs(0)
    slot = i % 2

    def row_in(tok, r, slt):
        return pltpu.make_async_copy(xg_hbm.at[pl.ds(tok, 1)], xbuf.at[slt, pl.ds(r, 1)], sem_in.at[slt])

    def row_out(tok, r, slt):
        return pltpu.make_async_copy(ybuf.at[slt, pl.ds(r, 1)], y_hbm.at[pl.ds(tok, 1)], sem_out.at[slt])

    unroll = 8

    def gather(tile, slt):
        def body(r, carry):
            row_in(inv_ref[tile * tmg + r], r, slt).start()
            return carry
        lax.fori_loop(0, tmg, body, 0, unroll=unroll)

    def wait_gather(slt):
        pltpu.make_async_copy(xg_hbm.at[pl.ds(0, tmg)], xbuf.at[slt], sem_in.at[slt]).wait()

    def scatter(tile, slt):
        def body(r, carry):
            row_out(inv_ref[tile * tmg + r], r, slt).start()
            return carry
        full = nv_ref[tile] == tmg

        @pl.when(full)
        def _():
            for r in range(tmg):
                body(r, 0)

        @pl.when(jnp.logical_not(full))
        def _():
            lax.fori_loop(0, nv_ref[tile], body, 0)

    def wait_scatter(tile, slt):
        def body(r, carry):
            row_out(0, r, slt).wait()
            return carry
        full = nv_ref[tile] == tmg

        @pl.when(full)
        def _():
            pltpu.make_async_copy(ybuf.at[slt], y_hbm.at[pl.ds(0, tmg)], sem_out.at[slt]).wait()

        @pl.when(jnp.logical_not(full))
        def _():
            lax.fori_loop(0, nv_ref[tile], body, 0)

    @pl.when(i == 0)
    def _():
        gather(0, 0)

    nxt = jnp.minimum(i + 1, n - 1)

    @pl.when(i >= 2)
    def _():
        wait_scatter(jnp.maximum(i - 2, 0), slot)

    @pl.when(nv_ref[i] == 0)
    def _():
        gather(nxt, 1 - slot)
        wait_gather(slot)

    @pl.when(nv_ref[i] > 0)
    def _():
        wait_gather(slot)
        x1 = xbuf[slot, :, 0:D_MODEL]
        for r in range(tmg // 2):
            row_in(inv_ref[nxt * tmg + r], r, 1 - slot).start()
        gate = xbuf[slot, :, D_MODEL:D_MODEL + LANES]
        for r in range(tmg // 2, tmg):
            row_in(inv_ref[nxt * tmg + r], r, 1 - slot).start()
        x1_res = xbuf[slot, :, 0:D_MODEL]
        ms = jnp.mean(x1 * x1, axis=-1, keepdims=True)
        h2 = (x1 * lax.rsqrt(ms + EPS) * ln2_ref[...]).astype(BF16)
        lane = lax.broadcasted_iota(jnp.int32, gate.shape, 1)
        first = N_EGROUPS + EXPERTS_PER_GROUP * tg_ref[i]
        pieces = []
        for e in range(EXPERTS_PER_GROUP):
            a = jnp.dot(h2, wg_ref[e], preferred_element_type=F32)
            up = jnp.dot(h2, wu_ref[e], preferred_element_type=F32)
            ge = jnp.sum(jnp.where(lane == first + e, gate, 0.0), axis=-1, keepdims=True)
            pieces.append((a * _sigmoid(a) * up * ge).astype(BF16))
        mo = jnp.dot(jnp.concatenate(pieces, axis=1), wd_ref[...], preferred_element_type=F32)
        x2 = x1_res + mo
        ms2 = jnp.mean(x2 * x2, axis=-1, keepdims=True)
        ybuf[slot] = x2 * lax.rsqrt(ms2 + EPS) * fw_ref[...]
        scatter(i, slot)

    @pl.when(i == n - 1)
    def _():
        wait_gather(1 - slot)

        @pl.when(i >= 1)
        def _():
            wait_scatter(jnp.maximum(i - 1, 0), 1 - slot)
        wait_scatter(i, slot)


def _moe(xg, inv, tile_group, n_valid, wg, wu, wd, ln2, fw, tmg):
    t = xg.shape[0]
    n_tiles = tile_group.shape[0]
    gcols = EXPERTS_PER_GROUP * D_EXPERT
    const = lambda i, inv, tg, nv: (0, 0)
    by_group = lambda i, inv, tg, nv: (tg[i], 0, 0)
    return pl.pallas_call(
        functools.partial(_moe_kernel, tmg),
        grid_spec=pltpu.PrefetchScalarGridSpec(
            num_scalar_prefetch=3,
            grid=(n_tiles,),
            in_specs=[
                pl.BlockSpec(memory_space=pltpu.HBM),
                pl.BlockSpec((EXPERTS_PER_GROUP, D_MODEL, D_EXPERT), by_group),
                pl.BlockSpec((EXPERTS_PER_GROUP, D_MODEL, D_EXPERT), by_group),
                pl.BlockSpec((None, gcols, D_MODEL), by_group),
                pl.BlockSpec((1, D_MODEL), const),
                pl.BlockSpec((1, D_MODEL), const),
            ],
            out_specs=pl.BlockSpec(memory_space=pltpu.HBM),
            scratch_shapes=[
                pltpu.VMEM((2, tmg, XG_COLS), F32),
                pltpu.VMEM((2, tmg, D_MODEL), F32),
                pltpu.SemaphoreType.DMA((2,)),
                pltpu.SemaphoreType.DMA((2,)),
            ],
        ),
        out_shape=jax.ShapeDtypeStruct((t, D_MODEL), F32),
        compiler_params=pltpu.CompilerParams(
            dimension_semantics=("arbitrary",), vmem_limit_bytes=VMEM_LIMIT),
        name="moe",
    )(inv, tile_group, n_valid, xg, wg, wu, wd, ln2, fw)


def _invperm_kernel(pos_ref, lo_ref, hi_ref, inv_ref):
    def clear(i, carry):
        inv_ref[i] = 0
        return carry
    for k in range(lo_ref.shape[0]):
        lax.fori_loop(lo_ref[k], hi_ref[k], clear, 0)

    def place(t, carry):
        inv_ref[pos_ref[t]] = t
        return carry
    lax.fori_loop(0, pos_ref.shape[0], place, 0, unroll=8)


def _invperm(pos, pad_lo, pad_hi, n):
    smem = pl.BlockSpec(memory_space=pltpu.SMEM)
    return pl.pallas_call(
        _invperm_kernel,
        in_specs=[smem, smem, smem],
        out_specs=smem,
        out_shape=jax.ShapeDtypeStruct((n,), jnp.int32),
        name="invperm",
    )(pos, pad_lo, pad_hi)


def _route(xg, tmg):
    t = xg.shape[0]
    g = xg[:, D_MODEL].astype(jnp.int32)
    rank = xg[:, D_MODEL + 1].astype(jnp.int32)
    counts = jnp.sum(g[:, None] == jnp.arange(N_EGROUPS, dtype=jnp.int32)[None, :], axis=0, dtype=jnp.int32)
    tiles_per = (counts + tmg - 1) // tmg
    tile_end = jnp.cumsum(tiles_per)
    tile_base = tile_end - tiles_per
    n_tiles = t // tmg + N_EGROUPS
    pos = tile_base[g] * tmg + rank
    pad_lo = jnp.concatenate([tile_base * tmg + counts, tile_end[-1:] * tmg]).astype(jnp.int32)
    pad_hi = jnp.concatenate([tile_end * tmg, jnp.full((1,), n_tiles * tmg, jnp.int32)]).astype(jnp.int32)
    inv = _invperm(pos, pad_lo, pad_hi, n_tiles * tmg)
    ti = jnp.arange(n_tiles, dtype=jnp.int32)
    tg = jnp.minimum(jnp.sum(ti[:, None] >= tile_end[None, :], axis=1, dtype=jnp.int32), N_EGROUPS - 1)
    nv = jnp.clip(counts[tg] - (ti - tile_base[tg]) * tmg, 0, tmg)
    nv = jnp.where(ti < tile_end[-1], nv, 0).astype(jnp.int32)
    return inv, tg, nv


def _pad_lanes(v, n=LANES):
    v = v.reshape(1, -1).astype(F32)
    return jnp.pad(v, ((0, 0), (0, n - v.shape[1])))


def _cast_kernel(a_ref, b_ref, c_ref, oa_ref, ob_ref, oc_ref):
    oa_ref[...] = a_ref[...].astype(BF16)
    ob_ref[...] = b_ref[...].astype(BF16)
    oc_ref[...] = c_ref[...].astype(BF16)


def _cast_expert_weights(wg, wu, wd, eb=4):
    n = wg.shape[0]
    specs = [pl.BlockSpec((eb,) + w.shape[1:], lambda i: (i, 0, 0)) for w in (wg, wu, wd)]
    return pl.pallas_call(
        _cast_kernel,
        grid=(n // eb,),
        in_specs=specs,
        out_specs=specs,
        out_shape=[jax.ShapeDtypeStruct(w.shape, BF16) for w in (wg, wu, wd)],
        compiler_params=pltpu.CompilerParams(
            dimension_semantics=("arbitrary",), vmem_limit_bytes=VMEM_LIMIT),
        name="castw",
    )(wg, wu, wd)


def kernel(x_prompt, x_sample, state_ssm, state_ssd_conv, state_short_conv, meta_tokens, ln1_w, w_in,
           ssd_conv_w, ssd_conv_b, ssd_dt_bias, ssd_A_log, ssd_D, ssd_norm_w, w_ssd_out, sc_conv_w,
           w_sc_out, w_o, ln2_w, w_rg, b_rg, w_re, b_re, w_gate, w_up, w_down, final_norm_w):
    assert ln1_w.shape[0] == 1, "single-layer trunk"
    bp, sp, _ = x_prompt.shape
    bs, ss, _ = x_sample.shape
    l = 0

    wt = jnp.swapaxes(w_in[l], 0, 1)
    wdt = jnp.pad(wt[A_COLS:A_COLS + SSD_HEADS], ((0, LANES - SSD_HEADS), (0, 0))).astype(BF16)
    lnw = ln1_w[l].reshape(1, D_MODEL)
    prm = (ssd_conv_w[l], ssd_conv_b[l].reshape(1, -1), sc_conv_w[l],
           _pad_lanes(ssd_dt_bias[l]), _pad_lanes(ssd_A_log[l]),
           jnp.repeat(ssd_D[l].astype(F32), SSD_HEAD_DIM).reshape(1, SSD_INNER),
           ssd_norm_w[l].reshape(1, SSD_INNER))
    wa = w_ssd_out[l].astype(BF16)
    ws = w_sc_out[l].astype(BF16)
    wo = w_o[l].astype(BF16)
    ln2 = ln2_w[l].reshape(1, D_MODEL)
    wr = jnp.pad(jnp.concatenate([w_rg[l], w_re[l]], axis=1),
                 ((0, 0), (0, LANES - N_EGROUPS - N_EXPERTS))).astype(BF16)
    br = _pad_lanes(jnp.concatenate([b_rg[l], b_re[l]]))
    gcols = EXPERTS_PER_GROUP * D_EXPERT
    wg, wu, wd = _cast_expert_weights(w_gate[l].reshape(N_EXPERTS, D_MODEL, D_EXPERT),
                                      w_up[l].reshape(N_EXPERTS, D_MODEL, D_EXPERT),
                                      w_down[l].reshape(N_EXPERTS, D_EXPERT, D_MODEL))
    wd = wd.reshape(N_EGROUPS, gcols, D_MODEL)
    fw = final_norm_w.reshape(1, D_MODEL)

    def mlp_tail(u, v, g, x, tmg):
        xg = _merge(u, v, g, x, wa, ws, wo, ln2, wr, br, tm=1024)
        inv, tg, nv = _route(xg, tmg)
        return _moe(xg, inv, tg, nv, wg, wu, wd, ln2, fw, tmg)

    q = LANES
    xm = jnp.pad(meta_tokens.astype(F32), ((q - N_META, 0), (0, 0)))
    xs = x_sample.reshape(bs * ss, D_MODEL)
    (sproj, sdt), (mproj, mdt) = _inproj(xs, lnw, wt, wdt, tm=bs * ss, extra=xm)
    zeros_hist = jnp.zeros((1, SSD_CONV - 1, SSD_CONV_DIM), F32)
    zeros_sch = jnp.zeros((1, SC_CONV - 1, SC_WIDTH), F32)
    zeros_st = jnp.zeros((1, SSD_INNER, SSD_STATE), F32)
    _, _, m_st, m_cs, m_scs = _seqmix(mproj, mdt, zeros_hist, zeros_sch, zeros_st, prm,
                                      nb=1, nc=1, q=q, ns=1, pad_rows=q - N_META, init_bcast=True)

    xp = x_prompt.reshape(bp * sp, D_MODEL)
    pproj, pdt = _inproj(xp, lnw, wt, wdt, tm=2048)
    pu, pv, p_st, p_cs, p_scs = _seqmix(pproj, pdt, m_cs, m_scs, m_st, prm, nb=bp, nc=sp // q, q=q,
                                        ns=1, pad_rows=0, init_bcast=True, cps=4)
    y_prompt = mlp_tail(pu, pv, pproj, xp, MOE_TILE).reshape(bp, sp, D_MODEL)

    ns = 8
    su, sv, s_st, s_cs, s_scs = _seqmix(sproj, sdt, state_ssd_conv[l], state_short_conv[l],
                                        state_ssm[l].reshape(bs, SSD_INNER, SSD_STATE), prm,
                                        nb=bs // ns, nc=1, q=ns * ss, ns=ns, pad_rows=0, init_bcast=False)
    y_sample = mlp_tail(su, sv, sproj, xs, MOE_TILE_SMALL).reshape(bs, ss, D_MODEL)

    hshape = (SSD_HEADS, SSD_HEAD_DIM, SSD_STATE)
    return (y_prompt, y_sample,
            p_st.reshape(1, bp, *hshape), p_cs[None], p_scs[None],
            s_st.reshape(1, bs, *hshape), s_cs[None], s_scs[None])
```

```python
import functools

import jax
import jax.numpy as jnp
import numpy as np
from jax import lax
from jax.experimental import pallas as pl
from jax.experimental.pallas import tpu as pltpu

F32 = jnp.float32
BF16 = jnp.bfloat16

D_MODEL = 1024
N_META = 16
SSD_INNER = 2048
SSD_HEAD_DIM = 64
SSD_HEADS = 32
SSD_GROUPS = 4
SSD_HPG = 8
SSD_STATE = 128
SSD_CONV = 4
SSD_CONV_DIM = 3072
SC_WIDTH = 1024
SC_CONV = 3
N_EGROUPS = 4
EXPERTS_PER_GROUP = 8
N_EXPERTS = 32
D_EXPERT = 256
EPS = 1e-6

LANES = 128
SUBLANES = 8
GROUP_COLS = SSD_HPG * SSD_HEAD_DIM
A_COLS = SSD_INNER + SSD_CONV_DIM
S_COLS = 3 * SC_WIDTH
G_COLS = 2 * D_MODEL
PROJ_COLS = A_COLS + S_COLS + G_COLS
XG_COLS = D_MODEL + LANES
MOE_TILE = 256
RANK_BLOCK = 512
NEG_BIG = -1e30
LOG2E = 1.4426950408889634
VMEM_LIMIT = 56 * 1024 * 1024


def _nt_dot(a, b):
    return lax.dot_general(a, b, (((1,), (1,)), ((), ())), preferred_element_type=F32)


def _dot01(m01_bf16, x):
    hi = x.astype(BF16)
    r1 = x - hi.astype(F32)
    mid = r1.astype(BF16)
    lo = (r1 - mid.astype(F32)).astype(BF16)
    out = jnp.dot(m01_bf16, hi, preferred_element_type=F32)
    out = out + jnp.dot(m01_bf16, mid, preferred_element_type=F32)
    return out + jnp.dot(m01_bf16, lo, preferred_element_type=F32)


def _softplus(x):
    return jnp.maximum(x, 0.0) + jnp.log1p(jnp.exp(-jnp.abs(x)))


def _sigmoid(x):
    return 1.0 / (1.0 + jnp.exp(-x))


def _inproj_kernel(n_sets, *refs):
    x_refs = refs[:n_sets]
    lnw_ref, wt_ref, wdt_ref = refs[n_sets:n_sets + 3]
    outs = refs[n_sets + 3:n_sets + 3 + 2 * n_sets]
    h_refs = refs[n_sets + 3 + 2 * n_sets:]

    @pl.when(pl.program_id(1) == 0)
    def _():
        for k in range(n_sets):
            x = x_refs[k][...]
            ms = jnp.mean(x * x, axis=-1, keepdims=True)
            h = (x * lax.rsqrt(ms + EPS) * lnw_ref[...]).astype(BF16)
            h_refs[k][...] = h
            outs[2 * k + 1][...] = _nt_dot(h, wdt_ref[...])

    w = wt_ref[...].astype(BF16)
    for k in range(n_sets):
        outs[2 * k][...] = _nt_dot(h_refs[k][...], w).astype(BF16)


def _inproj(x, lnw, wt, wdt, tm, tn=1024, extra=None):
    t = x.shape[0]
    n_a = A_COLS // tn
    nj = PROJ_COLS // tn

    def w_rows(i, j):
        return (pl.multiple_of(jnp.where(j < n_a, j * tn, j * tn + SSD_HEADS), SSD_HEADS), 0)

    sets = [(x, tm, lambda i: i)]
    if extra is not None:
        assert t == tm, "extra rows are recomputed per row tile"
        sets.append((extra, extra.shape[0], lambda i: 0))
    x_specs, out_specs, out_shape, scratch = [], [], [], []
    for xs_, rows, ri in sets:
        x_specs.append(pl.BlockSpec((rows, D_MODEL), lambda i, j, ri=ri: (ri(i), 0)))
        out_specs += [pl.BlockSpec((rows, tn), lambda i, j, ri=ri: (ri(i), j)),
                      pl.BlockSpec((rows, LANES), lambda i, j, ri=ri: (ri(i), 0))]
        out_shape += [jax.ShapeDtypeStruct((xs_.shape[0], PROJ_COLS), BF16),
                      jax.ShapeDtypeStruct((xs_.shape[0], LANES), F32)]
        scratch.append(pltpu.VMEM((rows, D_MODEL), BF16))
    res = pl.pallas_call(
        functools.partial(_inproj_kernel, len(sets)),
        grid=(t // tm, nj),
        in_specs=x_specs + [
            pl.BlockSpec((1, D_MODEL), lambda i, j: (0, 0)),
            pl.BlockSpec((pl.Element(tn), pl.Element(D_MODEL)), w_rows),
            pl.BlockSpec((LANES, D_MODEL), lambda i, j: (0, 0)),
        ],
        out_specs=out_specs,
        out_shape=out_shape,
        scratch_shapes=scratch,
        compiler_params=pltpu.CompilerParams(
            dimension_semantics=("arbitrary", "arbitrary"), vmem_limit_bytes=VMEM_LIMIT),
        name="inproj",
    )(*[st[0] for st in sets], lnw, wt, wdt)
    return (res[:2], res[2:]) if extra is not None else res


def halo_rows(ns):
    return max(2 * SUBLANES, ns * SUBLANES)


def _shift_matrices(q, ns, taps):
    L = q // ns
    hb = halo_rows(ns)
    r = np.arange(q)[:, None]
    col = np.arange(3 * hb + q)[None, :]
    seq, t = r // L, r % L
    mats = []
    for k in range(taps - 1):
        d = taps - 1 - k
        from_cur = (col >= 3 * hb) & (col - 3 * hb == r - d) & (t >= d)
        from_hist = (col < 3 * hb) & (col % hb == seq * SUBLANES + SUBLANES - d + t) & (t < d)
        mats.append(from_cur | from_hist)
    return jnp.asarray(np.stack(mats), dtype=BF16)


def _seqmix_kernel(Q, NS, CPS, one_step, pad_rows,
                   a_blk, sb_blk, sc_blk, sh_blk, dt_blk, hist_ref, schist_ref, st0_ref, shift_ref,
                   cw_ref, cb_ref, scw_ref, dtb_ref, alog_ref, drow_ref, nw_ref,
                   u_blk, v_blk, st_ref, cs_ref, scs_ref,
                   halo, scpad, xbc_s, y_s, xst_s):
    @pl.when(pl.program_id(1) == 0)
    def _():
        halo[...] = jnp.zeros_like(halo)
        for s in range(NS):
            halo[(s + 1) * SUBLANES - (SSD_CONV - 1):(s + 1) * SUBLANES, :] = hist_ref[s]
            scpad[s, SUBLANES - (SC_CONV - 1):SUBLANES, :] = schist_ref[s]
        if not one_step:
            st_ref[...] = st0_ref[...]

    src_ref = st0_ref if one_step else st_ref
    shared = (shift_ref, cw_ref, cb_ref, scw_ref, dtb_ref, alog_ref, drow_ref, nw_ref,
              src_ref, st_ref, cs_ref, scs_ref, halo, scpad, xbc_s, y_s, xst_s)
    row_blocks = (a_blk, sb_blk, sc_blk, sh_blk, dt_blk, u_blk, v_blk)
    if CPS == 1:
        _seqmix_chunk(Q, NS, pad_rows, *row_blocks, *shared)
    else:
        def body(k, carry):
            rows = pl.ds(pl.multiple_of(k * Q, Q), Q)
            _seqmix_chunk(Q, NS, pad_rows, *[r.at[rows] for r in row_blocks], *shared)
            return carry
        lax.fori_loop(0, CPS, body, 0)


def _seqmix_chunk(Q, NS, pad_rows, a_ref, sb_ref, sc_ref, sh_ref, dt_ref, u_ref, v_ref,
                  shift_ref, cw_ref, cb_ref, scw_ref, dtb_ref, alog_ref, drow_ref, nw_ref,
                  src_ref, st_ref, cs_ref, scs_ref, halo, scpad, xbc_s, y_s, xst_s):
    L = Q // NS
    lg = L.bit_length() - 1

    CW = 256
    hb = halo_rows(NS)
    for cc in range(0, SSD_CONV_DIM, CW):
        raw_b = a_ref[:, SSD_INNER + cc:SSD_INNER + cc + CW]
        hl = halo[:, cc:cc + CW]
        if hb > hl.shape[0]:
            hl = jnp.concatenate([hl, jnp.zeros((hb - hl.shape[0], CW), F32)], axis=0)
        h_hi = hl.astype(BF16)
        h_r = hl - h_hi.astype(F32)
        h_mid = h_r.astype(BF16)
        h_lo = (h_r - h_mid.astype(F32)).astype(BF16)
        ext = jnp.concatenate([h_hi, h_mid, h_lo, raw_b], axis=0)
        raw = raw_b.astype(F32)
        acc = raw * cw_ref[3:4, cc:cc + CW] + cb_ref[:, cc:cc + CW]
        for k in range(SSD_CONV - 1):
            acc = acc + jnp.dot(shift_ref[k], ext, preferred_element_type=F32) * cw_ref[k:k + 1, cc:cc + CW]
        xbc_s[:, cc:cc + CW] = acc * _sigmoid(acc)
        for s in range(NS):
            halo[s * SUBLANES:(s + 1) * SUBLANES, cc:cc + CW] = raw[(s + 1) * L - SUBLANES:(s + 1) * L]
    for s in range(NS):
        cs_ref[s] = halo[(s + 1) * SUBLANES - (SSD_CONV - 1):(s + 1) * SUBLANES, :]

    for cc in range(0, SC_WIDTH, CW):
        scb = sb_ref[:, cc:cc + CW].astype(F32)
        ch_all = sc_ref[:, cc:cc + CW].astype(F32) * sh_ref[:, cc:cc + CW].astype(F32)
        for s in range(NS):
            ch = ch_all[s * L:(s + 1) * L]
            scpad[s, SUBLANES:SUBLANES + L, cc:cc + CW] = ch
            acc = ch * scw_ref[2:3, cc:cc + CW]
            for k in range(SC_CONV - 1):
                acc = acc + scpad[s, 6 + k:6 + k + L, cc:cc + CW] * scw_ref[k:k + 1, cc:cc + CW]
            v_ref[s * L:(s + 1) * L, cc:cc + CW] = (scb[s * L:(s + 1) * L] * acc).astype(BF16)
    for s in range(NS):
        scs_ref[s] = scpad[s, L + 6:L + 8, :]
        scpad[s, 0:SUBLANES, :] = scpad[s, L:L + SUBLANES, :]

    def padrows(x):
        if Q == LANES:
            return x
        return jnp.concatenate([x, jnp.zeros((LANES - Q, x.shape[1]), x.dtype)], axis=0)

    li = lax.broadcasted_iota(jnp.int32, (Q, Q), 0)
    si = lax.broadcasted_iota(jnp.int32, (Q, Q), 1)
    same = (li >> lg) == (si >> lg)
    causal = jnp.logical_and(same, si <= li)
    tril01 = jnp.where(causal, 1.0, 0.0).astype(BF16)
    same01 = jnp.where(same, 1.0, 0.0).astype(BF16)

    dt = _softplus(dt_ref[...] + dtb_ref[...])
    if pad_rows:
        ri = lax.broadcasted_iota(jnp.int32, (Q, LANES), 0)
        dt = jnp.where(ri >= pad_rows, dt, 0.0)
    da = dt * (-jnp.exp(alog_ref[...]))
    acum = _dot01(tril01, da)
    tot = _dot01(same01, da)
    acum_t = padrows(acum).T
    tot_t = padrows(tot).T
    dt_t = padrows(dt).T
    w_t = jnp.exp(tot_t - acum_t) * dt_t
    acum2 = acum * LOG2E
    rowq_t = acum_t * LOG2E - jnp.log2(dt_t)

    left_head = lax.broadcasted_iota(jnp.int32, (Q, LANES), 1) < SSD_HEAD_DIM
    rowseq = lax.broadcasted_iota(jnp.int32, (Q, GROUP_COLS), 0) >> lg
    rowseq_p = lax.broadcasted_iota(jnp.int32, (LANES, LANES), 0) >> lg

    for g in range(SSD_GROUPS):
        b_g = xbc_s[:, SSD_INNER + g * SSD_STATE:SSD_INNER + (g + 1) * SSD_STATE]
        c_g = xbc_s[:, SSD_INNER + GROUP_COLS + g * SSD_STATE:SSD_INNER + GROUP_COLS + (g + 1) * SSD_STATE]
        b_gb = b_g.astype(BF16)
        c_gb = c_g.astype(BF16)
        cbm = _nt_dot(c_gb, b_gb)
        yo = None
        for s in range(NS):
            h_s = src_ref[s, g * GROUP_COLS:(g + 1) * GROUP_COLS, :].astype(BF16)
            yo_s = _nt_dot(c_gb, h_s)
            yo = yo_s if yo is None else jnp.where(rowseq == s, yo_s, yo)
        for rp in range(SSD_HPG // 2):
            h0 = g * SSD_HPG + 2 * rp
            cols = slice(h0 * SSD_HEAD_DIM, (h0 + 2) * SSD_HEAD_DIM)
            colbs, w_pair = [], []
            for h in (h0, h0 + 1):
                colb = jnp.broadcast_to(acum2[:, h:h + 1], (Q, LANES))
                rowb = jnp.broadcast_to(rowq_t[h:h + 1, 0:Q], (Q, Q))
                dec_dt = jnp.exp2(jnp.where(causal, colb[:, 0:Q] - rowb, NEG_BIG))
                w_pair.append((cbm * dec_dt).astype(BF16))
                colbs.append(colb)
            x_p = xbc_s[:, cols]
            rhs = jnp.concatenate([jnp.where(left_head, x_p, 0.0).astype(BF16),
                                   jnp.where(left_head, 0.0, x_p).astype(BF16)], axis=0)
            yd = jnp.dot(jnp.concatenate(w_pair, axis=1), rhs, preferred_element_type=F32)
            ecol = jnp.exp2(jnp.where(left_head, colbs[0], colbs[1]))
            y_s[:, cols] = yd + yo[:, rp * LANES:(rp + 1) * LANES] * ecol + drow_ref[:, cols] * x_p

    for jb in range(SSD_INNER // LANES):
        xst_s[jb * LANES:(jb + 1) * LANES, :] = padrows(xbc_s[:, jb * LANES:(jb + 1) * LANES]).T
    for s in range(NS):
        da_b = jnp.exp(jnp.broadcast_to(tot_t[:, s * L:s * L + 1], (LANES, LANES)))
        for g in range(SSD_GROUPS):
            b_p = padrows(xbc_s[:, SSD_INNER + g * SSD_STATE:SSD_INNER + (g + 1) * SSD_STATE])
            if NS > 1:
                b_p = jnp.where(rowseq_p == s, b_p, 0.0)
            pieces = []
            for r in range(SSD_HPG):
                h = g * SSD_HPG + r
                pieces.append(xst_s[h * SSD_HEAD_DIM:(h + 1) * SSD_HEAD_DIM, :] * w_t[h:h + 1, :])
            xw_t = jnp.concatenate(pieces, axis=0).astype(BF16)
            upd = jnp.dot(xw_t, b_p.astype(BF16), preferred_element_type=F32)
            for r in range(SSD_HPG):
                h = g * SSD_HPG + r
                rows = slice(h * SSD_HEAD_DIM, (h + 1) * SSD_HEAD_DIM)
                dec_h = jnp.broadcast_to(da_b[h:h + 1, :], (SSD_HEAD_DIM, SSD_STATE))
                st_ref[s, rows, :] = dec_h * src_ref[s, rows, :] + upd[r * SSD_HEAD_DIM:(r + 1) * SSD_HEAD_DIM, :]

    for g in range(SSD_GROUPS):
        cols = slice(g * GROUP_COLS, (g + 1) * GROUP_COLS)
        z = a_ref[:, cols].astype(F32)
        ug = y_s[:, cols] * (z * _sigmoid(z))
        ms = jnp.mean(ug * ug, axis=-1, keepdims=True)
        u_ref[:, cols] = (ug * lax.rsqrt(ms + EPS) * nw_ref[:, cols]).astype(BF16)


def _seqmix(proj, pdt, hist, schist, st0, prm, *, nb, nc, q, ns, pad_rows, init_bcast, cps=1):
    L = q // ns
    cw, cb, scw, dtb, alog, drow, nw = prm
    steps = nc // cps
    rows = q * cps
    sc_block0 = A_COLS // SC_WIDTH

    def tok(b, c):
        return (b * steps + c, 0)

    def init3(b, c):
        return (0 if init_bcast else b, 0, 0)

    def const2(b, c):
        return (0, 0)

    def per_b(b, c):
        return (b, 0, 0)

    n_seq = nb * ns
    shifts = _shift_matrices(q, ns, SSD_CONV)
    sc_specs = [pl.BlockSpec((rows, SC_WIDTH), lambda b, c, k=k: (b * steps + c, sc_block0 + k))
                for k in range(3)]
    return pl.pallas_call(
        functools.partial(_seqmix_kernel, q, ns, cps, steps == 1 and cps == 1, pad_rows),
        grid=(nb, steps),
        in_specs=[pl.BlockSpec((rows, A_COLS), tok)] + sc_specs + [
            pl.BlockSpec((rows, LANES), tok),
            pl.BlockSpec((ns, SSD_CONV - 1, SSD_CONV_DIM), init3),
            pl.BlockSpec((ns, SC_CONV - 1, SC_WIDTH), init3),
            pl.BlockSpec((ns, SSD_INNER, SSD_STATE), init3),
            pl.BlockSpec(shifts.shape, lambda b, c: (0, 0, 0)),
            pl.BlockSpec((SSD_CONV, SSD_CONV_DIM), const2),
            pl.BlockSpec((1, SSD_CONV_DIM), const2),
            pl.BlockSpec((SC_CONV, SC_WIDTH), const2),
            pl.BlockSpec((1, LANES), const2),
            pl.BlockSpec((1, LANES), const2),
            pl.BlockSpec((1, SSD_INNER), const2),
            pl.BlockSpec((1, SSD_INNER), const2),
        ],
        out_specs=[
            pl.BlockSpec((rows, SSD_INNER), tok),
            pl.BlockSpec((rows, SC_WIDTH), tok),
            pl.BlockSpec((ns, SSD_INNER, SSD_STATE), per_b),
            pl.BlockSpec((ns, SSD_CONV - 1, SSD_CONV_DIM), per_b),
            pl.BlockSpec((ns, SC_CONV - 1, SC_WIDTH), per_b),
        ],
        out_shape=[
            jax.ShapeDtypeStruct((nb * nc * q, SSD_INNER), BF16),
            jax.ShapeDtypeStruct((nb * nc * q, SC_WIDTH), BF16),
            jax.ShapeDtypeStruct((n_seq, SSD_INNER, SSD_STATE), F32),
            jax.ShapeDtypeStruct((n_seq, SSD_CONV - 1, SSD_CONV_DIM), F32),
            jax.ShapeDtypeStruct((n_seq, SC_CONV - 1, SC_WIDTH), F32),
        ],
        scratch_shapes=[
            pltpu.VMEM((ns * SUBLANES, SSD_CONV_DIM), F32),
            pltpu.VMEM((ns, SUBLANES + L, SC_WIDTH), F32),
            pltpu.VMEM((q, SSD_CONV_DIM), F32),
            pltpu.VMEM((q, SSD_INNER), F32),
            pltpu.VMEM((SSD_INNER, LANES), F32),
        ],
        compiler_params=pltpu.CompilerParams(
            dimension_semantics=("arbitrary", "arbitrary"), vmem_limit_bytes=VMEM_LIMIT),
        name="seqmix",
    )(proj, proj, proj, proj, pdt, hist, schist, st0, shifts, cw, cb, scw, dtb, alog, drow, nw)


def _merge_kernel(n_real, n_steps, *refs):
    i = pl.program_id(0)
    if n_real == n_steps:
        _merge_tile(i, *refs)
        return

    @pl.when(i >= n_real)
    def _():
        xg_ref = refs[10]
        xg_ref[...] = jnp.zeros_like(xg_ref)

    @pl.when(i < n_real)
    def _():
        _merge_tile(i, *refs)


def _merge_tile(i, u_ref, v_ref, g_ref, x_ref, wa_ref, ws_ref, wo_ref, ln2_ref, wr_ref, br_ref,
                xg_ref, cnt_s):
    @pl.when(i == 0)
    def _():
        cnt_s[...] = jnp.zeros_like(cnt_s)

    y_ssd = jnp.dot(u_ref[...], wa_ref[...], preferred_element_type=F32)
    y_sc = jnp.dot(v_ref[...], ws_ref[...], preferred_element_type=F32)
    g1 = g_ref[:, 0:D_MODEL].astype(F32)
    g2 = g_ref[:, D_MODEL:2 * D_MODEL].astype(F32)
    merged = (_sigmoid(g1) * y_ssd + _sigmoid(g2) * y_sc).astype(BF16)
    x1 = x_ref[...] + jnp.dot(merged, wo_ref[...], preferred_element_type=F32)
    xg_ref[:, 0:D_MODEL] = x1
    ms = jnp.mean(x1 * x1, axis=-1, keepdims=True)
    h2 = (x1 * lax.rsqrt(ms + EPS) * ln2_ref[...]).astype(BF16)

    logits = jnp.dot(h2, wr_ref[...], preferred_element_type=F32) + br_ref[...]
    lane = lax.broadcasted_iota(jnp.int32, logits.shape, 1).astype(F32)
    big = float(LANES)
    gl = jnp.where(lane < N_EGROUPS, logits, NEG_BIG)
    gmax = jnp.max(gl, axis=-1, keepdims=True)
    g_sel = jnp.min(jnp.where(gl == gmax, lane, big), axis=-1, keepdims=True)
    gsum = jnp.sum(jnp.exp(gl - gmax), axis=-1, keepdims=True)
    g_prob = 1.0 / gsum
    lo = N_EGROUPS + EXPERTS_PER_GROUP * g_sel
    emask = jnp.logical_and(lane >= lo, lane < lo + EXPERTS_PER_GROUP)
    el = jnp.where(emask, logits, NEG_BIG)
    m1 = jnp.max(el, axis=-1, keepdims=True)
    e = jnp.where(emask, jnp.exp(el - m1), -1.0)
    i1 = jnp.min(jnp.where(e == 1.0, lane, big), axis=-1, keepdims=True)
    e_rest = jnp.where(lane == i1, -1.0, e)
    e2 = jnp.max(e_rest, axis=-1, keepdims=True)
    i2 = jnp.min(jnp.where(e_rest == e2, lane, big), axis=-1, keepdims=True)
    denom = 1.0 + e2
    w1 = g_prob / denom
    w2 = g_prob * e2 / denom
    gates = jnp.where(lane == i1, w1, 0.0) + jnp.where(lane == i2, w2, 0.0)

    tm = logits.shape[0]
    rb = min(tm, RANK_BLOCK)
    onehot = jnp.where(lane == g_sel, 1.0, 0.0).astype(BF16)
    ri = lax.broadcasted_iota(jnp.int32, (rb, rb), 0)
    ci = lax.broadcasted_iota(jnp.int32, (rb, rb), 1)
    tril01 = jnp.where(ci <= ri, 1.0, 0.0).astype(BF16)
    carry = cnt_s[...]
    incl_blocks = []
    for r0 in range(0, tm, rb):
        incl_b = jnp.dot(tril01, onehot[r0:r0 + rb], preferred_element_type=F32) + carry
        carry = incl_b[rb - 1:rb, :]
        incl_blocks.append(incl_b)
    cnt_s[...] = carry
    incl = incl_blocks[0] if len(incl_blocks) == 1 else jnp.concatenate(incl_blocks, axis=0)
    rank = jnp.sum(jnp.where(lane == g_sel, incl - 1.0, 0.0), axis=-1, keepdims=True)
    xg_ref[:, D_MODEL:D_MODEL + LANES] = jnp.where(
        lane == 0, g_sel.astype(F32), jnp.where(lane == 1, rank, gates))


def _merge(u, v, g, x, wa, ws, wo, ln2, wr, br, tm, pad_tiles=0):
    t = x.shape[0]
    n_real = t // tm
    row = lambda i: (jnp.minimum(i, n_real - 1), 0)
    const = lambda i: (0, 0)
    return pl.pallas_call(
        functools.partial(_merge_kernel, n_real, n_real + pad_tiles),
        grid=(n_real + pad_tiles,),
        in_specs=[
            pl.BlockSpec((tm, SSD_INNER), row),
            pl.BlockSpec((tm, SC_WIDTH), row),
            pl.BlockSpec((tm, G_COLS), lambda i: (jnp.minimum(i, n_real - 1), (A_COLS + S_COLS) // G_COLS)),
            pl.BlockSpec((tm, D_MODEL), row),
            pl.BlockSpec((SSD_INNER, D_MODEL), const, pipeline_mode=pl.Buffered(1)),
            pl.BlockSpec((SC_WIDTH, D_MODEL), const, pipeline_mode=pl.Buffered(1)),
            pl.BlockSpec((D_MODEL, D_MODEL), const, pipeline_mode=pl.Buffered(1)),
            pl.BlockSpec((1, D_MODEL), const),
            pl.BlockSpec((D_MODEL, LANES), const),
            pl.BlockSpec((1, LANES), const),
        ],
        out_specs=pl.BlockSpec((tm, XG_COLS), lambda i: (i, 0)),
        out_shape=jax.ShapeDtypeStruct((t + pad_tiles * tm, XG_COLS), F32),
        scratch_shapes=[pltpu.VMEM((1, LANES), F32)],
        compiler_params=pltpu.CompilerParams(
            dimension_semantics=("arbitrary",), vmem_limit_bytes=VMEM_LIMIT),
        name="merge",
    )(u, v, g, x, wa, ws, wo, ln2, wr, br)


def _moe_kernel(tmg, offs, inv_ref, tg_ref, nv_ref, ts_ref, xg_hbm, wg_ref, wu_ref, wd_ref, ln2_ref,
                fw_ref, *rest):
    y_hbms = rest[:len(offs)]
    xbuf, ybuf, sem_in, sem_out = rest[len(offs):]
    i = pl.program_id(0)
    n = pl.num_programs(0)
    slot = i % 2

    def row_in(tok, r, slt):
        return pltpu.make_async_copy(xg_hbm.at[pl.ds(tok, 1)], xbuf.at[slt, pl.ds(r, 1)], sem_in.at[slt])

    def row_out(k, tok, r, slt):
        return pltpu.make_async_copy(ybuf.at[slt, pl.ds(r, 1)], y_hbms[k].at[pl.ds(tok - offs[k], 1)],
                                     sem_out.at[slt])

    unroll = 8

    def gather(tile, slt):
        def body(r, carry):
            row_in(inv_ref[tile * tmg + r], r, slt).start()
            return carry
        lax.fori_loop(0, tmg, body, 0, unroll=unroll)

    def wait_gather(slt):
        pltpu.make_async_copy(xg_hbm.at[pl.ds(0, tmg)], xbuf.at[slt], sem_in.at[slt]).wait()

    def scatter(tile, slt):
        full = nv_ref[tile] == tmg
        for k in range(len(offs)):
            def body(r, carry, k=k):
                row_out(k, inv_ref[tile * tmg + r], r, slt).start()
                return carry
            mine = ts_ref[tile] == k

            @pl.when(jnp.logical_and(mine, full))
            def _():
                for r in range(tmg):
                    body(r, 0)

            @pl.when(jnp.logical_and(mine, jnp.logical_not(full)))
            def _():
                lax.fori_loop(0, nv_ref[tile], body, 0)

    def wait_scatter(tile, slt):
        def body(r, carry):
            row_out(0, offs[0], r, slt).wait()
            return carry
        full = nv_ref[tile] == tmg

        @pl.when(full)
        def _():
            pltpu.make_async_copy(ybuf.at[slt], y_hbms[0].at[pl.ds(0, tmg)], sem_out.at[slt]).wait()

        @pl.when(jnp.logical_not(full))
        def _():
            lax.fori_loop(0, nv_ref[tile], body, 0)

    @pl.when(i == 0)
    def _():
        gather(0, 0)

    nxt = jnp.minimum(i + 1, n - 1)

    @pl.when(i >= 2)
    def _():
        wait_scatter(jnp.maximum(i - 2, 0), slot)

    @pl.when(nv_ref[i] == 0)
    def _():
        gather(nxt, 1 - slot)
        wait_gather(slot)

    @pl.when(nv_ref[i] > 0)
    def _():
        wait_gather(slot)
        x1 = xbuf[slot, :, 0:D_MODEL]
        for r in range(tmg // 2):
            row_in(inv_ref[nxt * tmg + r], r, 1 - slot).start()
        gate = xbuf[slot, :, D_MODEL:D_MODEL + LANES]
        for r in range(tmg // 2, tmg):
            row_in(inv_ref[nxt * tmg + r], r, 1 - slot).start()
        x1_res = xbuf[slot, :, 0:D_MODEL]
        ms = jnp.mean(x1 * x1, axis=-1, keepdims=True)
        h2 = (x1 * lax.rsqrt(ms + EPS) * ln2_ref[...]).astype(BF16)
        lane = lax.broadcasted_iota(jnp.int32, gate.shape, 1)
        first = N_EGROUPS + EXPERTS_PER_GROUP * tg_ref[i]
        pieces = []
        for e in range(EXPERTS_PER_GROUP):
            a = jnp.dot(h2, wg_ref[e], preferred_element_type=F32)
            up = jnp.dot(h2, wu_ref[e], preferred_element_type=F32)
            ge = jnp.sum(jnp.where(lane == first + e, gate, 0.0), axis=-1, keepdims=True)
            pieces.append((a * _sigmoid(a) * up * ge).astype(BF16))
        mo = jnp.dot(jnp.concatenate(pieces, axis=1), wd_ref[...], preferred_element_type=F32)
        x2 = x1_res + mo
        ms2 = jnp.mean(x2 * x2, axis=-1, keepdims=True)
        ybuf[slot] = x2 * lax.rsqrt(ms2 + EPS) * fw_ref[...]
        scatter(i, slot)

    @pl.when(i == n - 1)
    def _():
        wait_gather(1 - slot)

        @pl.when(i >= 1)
        def _():
            wait_scatter(jnp.maximum(i - 1, 0), 1 - slot)
        wait_scatter(i, slot)


def _moe(xg, inv, tile_group, n_valid, tile_out, out_rows, wg, wu, wd, ln2, fw, tmg):
    n_tiles = tile_group.shape[0]
    gcols = EXPERTS_PER_GROUP * D_EXPERT
    offs = tuple(int(sum(out_rows[:k])) for k in range(len(out_rows)))
    const = lambda i, inv, tg, nv, ts: (0, 0)
    by_group = lambda i, inv, tg, nv, ts: (tg[i], 0, 0)
    return pl.pallas_call(
        functools.partial(_moe_kernel, tmg, offs),
        grid_spec=pltpu.PrefetchScalarGridSpec(
            num_scalar_prefetch=4,
            grid=(n_tiles,),
            in_specs=[
                pl.BlockSpec(memory_space=pltpu.HBM),
                pl.BlockSpec((EXPERTS_PER_GROUP, D_MODEL, D_EXPERT), by_group),
                pl.BlockSpec((EXPERTS_PER_GROUP, D_MODEL, D_EXPERT), by_group),
                pl.BlockSpec((None, gcols, D_MODEL), by_group),
                pl.BlockSpec((1, D_MODEL), const),
                pl.BlockSpec((1, D_MODEL), const),
            ],
            out_specs=[pl.BlockSpec(memory_space=pltpu.HBM) for _ in out_rows],
            scratch_shapes=[
                pltpu.VMEM((2, tmg, XG_COLS), F32),
                pltpu.VMEM((2, tmg, D_MODEL), F32),
                pltpu.SemaphoreType.DMA((2,)),
                pltpu.SemaphoreType.DMA((2,)),
            ],
        ),
        out_shape=[jax.ShapeDtypeStruct((r, D_MODEL), F32) for r in out_rows],
        compiler_params=pltpu.CompilerParams(
            dimension_semantics=("arbitrary",), vmem_limit_bytes=VMEM_LIMIT),
        name="moe",
    )(inv, tile_group, n_valid, tile_out, xg, wg, wu, wd, ln2, fw)


def _invperm_kernel(pos_ref, lo_ref, hi_ref, inv_ref):
    def clear(i, carry):
        inv_ref[i] = 0
        return carry
    for k in range(lo_ref.shape[0]):
        lax.fori_loop(lo_ref[k], hi_ref[k], clear, 0)

    def place(t, carry):
        inv_ref[pos_ref[t]] = t
        return carry
    lax.fori_loop(0, pos_ref.shape[0], place, 0, unroll=8)


def _invperm(pos, pad_lo, pad_hi, n):
    smem = pl.BlockSpec(memory_space=pltpu.SMEM)
    return pl.pallas_call(
        _invperm_kernel,
        in_specs=[smem, smem, smem],
        out_specs=smem,
        out_shape=jax.ShapeDtypeStruct((n,), jnp.int32),
        name="invperm",
    )(pos, pad_lo, pad_hi)


def _route(xg, tmg, out_rows):
    t = xg.shape[0]
    n_seg = len(out_rows) * N_EGROUPS
    tok = jnp.arange(t, dtype=jnp.int32)
    src = sum((tok >= int(sum(out_rows[:k]))).astype(jnp.int32) for k in range(1, len(out_rows)))
    g = xg[:, D_MODEL].astype(jnp.int32) + N_EGROUPS * src
    rank = xg[:, D_MODEL + 1].astype(jnp.int32)
    counts = jnp.sum(g[:, None] == jnp.arange(n_seg, dtype=jnp.int32)[None, :], axis=0, dtype=jnp.int32)
    tiles_per = (counts + tmg - 1) // tmg
    tile_end = jnp.cumsum(tiles_per)
    tile_base = tile_end - tiles_per
    n_tiles = t // tmg + n_seg
    pos = tile_base[g] * tmg + rank
    pad_lo = jnp.concatenate([tile_base * tmg + counts, tile_end[-1:] * tmg]).astype(jnp.int32)
    pad_hi = jnp.concatenate([tile_end * tmg, jnp.full((1,), n_tiles * tmg, jnp.int32)]).astype(jnp.int32)
    inv = _invperm(pos, pad_lo, pad_hi, n_tiles * tmg)
    ti = jnp.arange(n_tiles, dtype=jnp.int32)
    seg = jnp.minimum(jnp.sum(ti[:, None] >= tile_end[None, :], axis=1, dtype=jnp.int32), n_seg - 1)
    nv = jnp.clip(counts[seg] - (ti - tile_base[seg]) * tmg, 0, tmg)
    nv = jnp.where(ti < tile_end[-1], nv, 0).astype(jnp.int32)
    return inv, seg % N_EGROUPS, nv, seg // N_EGROUPS


def _pad_lanes(v, n=LANES):
    v = v.reshape(1, -1).astype(F32)
    return jnp.pad(v, ((0, 0), (0, n - v.shape[1])))


def _cast_kernel(a_ref, b_ref, c_ref, oa_ref, ob_ref, oc_ref):
    oa_ref[...] = a_ref[...].astype(BF16)
    ob_ref[...] = b_ref[...].astype(BF16)
    oc_ref[...] = c_ref[...].astype(BF16)


def _cast_expert_weights(wg, wu, wd, eb=4):
    n = wg.shape[0]
    specs = [pl.BlockSpec((eb,) + w.shape[1:], lambda i: (i, 0, 0)) for w in (wg, wu, wd)]
    return pl.pallas_call(
        _cast_kernel,
        grid=(n // eb,),
        in_specs=specs,
        out_specs=specs,
        out_shape=[jax.ShapeDtypeStruct(w.shape, BF16) for w in (wg, wu, wd)],
        compiler_params=pltpu.CompilerParams(
            dimension_semantics=("arbitrary",), vmem_limit_bytes=VMEM_LIMIT),
        name="castw",
    )(wg, wu, wd)


def kernel(x_prompt, x_sample, state_ssm, state_ssd_conv, state_short_conv, meta_tokens, ln1_w, w_in,
           ssd_conv_w, ssd_conv_b, ssd_dt_bias, ssd_A_log, ssd_D, ssd_norm_w, w_ssd_out, sc_conv_w,
           w_sc_out, w_o, ln2_w, w_rg, b_rg, w_re, b_re, w_gate, w_up, w_down, final_norm_w):
    assert ln1_w.shape[0] == 1, "single-layer trunk"
    bp, sp, _ = x_prompt.shape
    bs, ss, _ = x_sample.shape
    l = 0

    wt = jnp.swapaxes(w_in[l], 0, 1)
    wdt = jnp.pad(wt[A_COLS:A_COLS + SSD_HEADS], ((0, LANES - SSD_HEADS), (0, 0))).astype(BF16)
    lnw = ln1_w[l].reshape(1, D_MODEL)
    prm = (ssd_conv_w[l], ssd_conv_b[l].reshape(1, -1), sc_conv_w[l],
           _pad_lanes(ssd_dt_bias[l]), _pad_lanes(ssd_A_log[l]),
           jnp.repeat(ssd_D[l].astype(F32), SSD_HEAD_DIM).reshape(1, SSD_INNER),
           ssd_norm_w[l].reshape(1, SSD_INNER))
    wa = w_ssd_out[l].astype(BF16)
    ws = w_sc_out[l].astype(BF16)
    wo = w_o[l].astype(BF16)
    ln2 = ln2_w[l].reshape(1, D_MODEL)
    wr = jnp.pad(jnp.concatenate([w_rg[l], w_re[l]], axis=1),
                 ((0, 0), (0, LANES - N_EGROUPS - N_EXPERTS))).astype(BF16)
    br = _pad_lanes(jnp.concatenate([b_rg[l], b_re[l]]))
    gcols = EXPERTS_PER_GROUP * D_EXPERT
    wg, wu, wd = _cast_expert_weights(w_gate[l].reshape(N_EXPERTS, D_MODEL, D_EXPERT),
                                      w_up[l].reshape(N_EXPERTS, D_MODEL, D_EXPERT),
                                      w_down[l].reshape(N_EXPERTS, D_EXPERT, D_MODEL))
    wd = wd.reshape(N_EGROUPS, gcols, D_MODEL)
    fw = final_norm_w.reshape(1, D_MODEL)

    q = LANES
    xm = jnp.pad(meta_tokens.astype(F32), ((q - N_META, 0), (0, 0)))
    xs = x_sample.reshape(bs * ss, D_MODEL)
    (sproj, sdt), (mproj, mdt) = _inproj(xs, lnw, wt, wdt, tm=bs * ss, extra=xm)
    zeros_hist = jnp.zeros((1, SSD_CONV - 1, SSD_CONV_DIM), F32)
    zeros_sch = jnp.zeros((1, SC_CONV - 1, SC_WIDTH), F32)
    zeros_st = jnp.zeros((1, SSD_INNER, SSD_STATE), F32)
    _, _, m_st, m_cs, m_scs = _seqmix(mproj, mdt, zeros_hist, zeros_sch, zeros_st, prm,
                                      nb=1, nc=1, q=q, ns=1, pad_rows=q - N_META, init_bcast=True)

    xp = x_prompt.reshape(bp * sp, D_MODEL)
    pproj, pdt = _inproj(xp, lnw, wt, wdt, tm=2048)
    pu, pv, p_st, p_cs, p_scs = _seqmix(pproj, pdt, m_cs, m_scs, m_st, prm, nb=bp, nc=sp // q, q=q,
                                        ns=1, pad_rows=0, init_bcast=True, cps=4)
    merge_tm = 1024
    assert (bs * ss) % merge_tm == 0
    pxg = _merge(pu, pv, pproj, xp, wa, ws, wo, ln2, wr, br, tm=merge_tm, pad_tiles=bs * ss // merge_tm)

    ns = 8
    su, sv, s_st, s_cs, s_scs = _seqmix(sproj, sdt, state_ssd_conv[l], state_short_conv[l],
                                        state_ssm[l].reshape(bs, SSD_INNER, SSD_STATE), prm,
                                        nb=bs // ns, nc=1, q=ns * ss, ns=ns, pad_rows=0, init_bcast=False)
    sxg = _merge(su, sv, sproj, xs, wa, ws, wo, ln2, wr, br, tm=merge_tm)

    out_rows = (bp * sp, bs * ss)
    xg = lax.dynamic_update_slice(pxg, sxg, (bp * sp, 0))
    inv, tg, nv, tsrc = _route(xg, MOE_TILE, out_rows)
    y_prompt, y_sample = _moe(xg, inv, tg, nv, tsrc, out_rows, wg, wu, wd, ln2, fw, MOE_TILE)
    y_prompt = y_prompt.reshape(bp, sp, D_MODEL)
    y_sample = y_sample.reshape(bs, ss, D_MODEL)

    hshape = (SSD_HEADS, SSD_HEAD_DIM, SSD_STATE)
    return (y_prompt, y_sample,
            p_st.reshape(1, bp, *hshape), p_cs[None], p_scs[None],
            s_st.reshape(1, bs, *hshape), s_cs[None], s_scs[None])
```

```python
import functools

import jax
import jax.numpy as jnp
import numpy as np
from jax import lax
from jax.experimental import pallas as pl
from jax.experimental.pallas import tpu as pltpu

F32 = jnp.float32
BF16 = jnp.bfloat16

D_MODEL = 1024
N_META = 16
SSD_INNER = 2048
SSD_HEAD_DIM = 64
SSD_HEADS = 32
SSD_GROUPS = 4
SSD_HPG = 8
SSD_STATE = 128
SSD_CONV = 4
SSD_CONV_DIM = 3072
SC_WIDTH = 1024
SC_CONV = 3
N_EGROUPS = 4
EXPERTS_PER_GROUP = 8
N_EXPERTS = 32
D_EXPERT = 256
EPS = 1e-6

LANES = 128
SUBLANES = 8
GROUP_COLS = SSD_HPG * SSD_HEAD_DIM
A_COLS = SSD_INNER + SSD_CONV_DIM
S_COLS = 3 * SC_WIDTH
G_COLS = 2 * D_MODEL
PROJ_COLS = A_COLS + S_COLS + G_COLS
XG_COLS = D_MODEL + LANES
MOE_TILE = 256
MOE_TILE_SMALL = 128
RANK_BLOCK = 512
NEG_BIG = -1e30
LOG2E = 1.4426950408889634
VMEM_LIMIT = 56 * 1024 * 1024


def _nt_dot(a, b):
    return lax.dot_general(a, b, (((1,), (1,)), ((), ())), preferred_element_type=F32)


def _dot01(m01_bf16, x):
    hi = x.astype(BF16)
    r1 = x - hi.astype(F32)
    mid = r1.astype(BF16)
    lo = (r1 - mid.astype(F32)).astype(BF16)
    out = jnp.dot(m01_bf16, hi, preferred_element_type=F32)
    out = out + jnp.dot(m01_bf16, mid, preferred_element_type=F32)
    return out + jnp.dot(m01_bf16, lo, preferred_element_type=F32)


def _softplus(x):
    return jnp.maximum(x, 0.0) + jnp.log1p(jnp.exp(-jnp.abs(x)))


def _sigmoid(x):
    return 1.0 / (1.0 + jnp.exp(-x))


def _inproj_kernel(n_sets, n_cast, cast_blocks, *refs):
    x_refs = refs[:n_sets]
    lnw_ref, wt_ref, wdt_ref = refs[n_sets:n_sets + 3]
    cast_in = refs[n_sets + 3:n_sets + 3 + n_cast]
    outs = refs[n_sets + 3 + n_cast:n_sets + 3 + n_cast + 2 * n_sets]
    cast_out = refs[n_sets + 3 + n_cast + 2 * n_sets:n_sets + 3 + 2 * n_cast + 2 * n_sets]
    h_refs = refs[n_sets + 3 + 2 * n_cast + 2 * n_sets:]

    if n_cast:
        step = pl.program_id(0) * pl.num_programs(1) + pl.program_id(1)

        @pl.when(step < cast_blocks)
        def _():
            for src, dst in zip(cast_in, cast_out):
                dst[...] = src[...].astype(BF16)

    @pl.when(pl.program_id(1) == 0)
    def _():
        for k in range(n_sets):
            x = x_refs[k][...]
            ms = jnp.mean(x * x, axis=-1, keepdims=True)
            h = (x * lax.rsqrt(ms + EPS) * lnw_ref[...]).astype(BF16)
            h_refs[k][...] = h
            outs[2 * k + 1][...] = _nt_dot(h, wdt_ref[...])

    w = wt_ref[...].astype(BF16)
    for k in range(n_sets):
        outs[2 * k][...] = _nt_dot(h_refs[k][...], w).astype(BF16)


def _inproj(x, lnw, wt, wdt, tm, tn=1024, extra=None, cast=()):
    t = x.shape[0]
    n_a = A_COLS // tn
    nj = PROJ_COLS // tn
    cast_blocks = cast[0].shape[0] if cast else 0
    assert all(c.shape[0] == cast_blocks for c in cast) and cast_blocks <= (t // tm) * nj

    def cast_idx(i, j):
        return (jnp.minimum(i * nj + j, cast_blocks - 1), 0, 0)

    cast_specs = [pl.BlockSpec((1,) + c.shape[1:], cast_idx) for c in cast]

    def w_rows(i, j):
        return (pl.multiple_of(jnp.where(j < n_a, j * tn, j * tn + SSD_HEADS), SSD_HEADS), 0)

    sets = [(x, tm, lambda i: i)]
    if extra is not None:
        assert t == tm, "extra rows are recomputed per row tile"
        sets.append((extra, extra.shape[0], lambda i: 0))
    x_specs, out_specs, out_shape, scratch = [], [], [], []
    for xs_, rows, ri in sets:
        x_specs.append(pl.BlockSpec((rows, D_MODEL), lambda i, j, ri=ri: (ri(i), 0)))
        out_specs += [pl.BlockSpec((rows, tn), lambda i, j, ri=ri: (ri(i), j)),
                      pl.BlockSpec((rows, LANES), lambda i, j, ri=ri: (ri(i), 0))]
        out_shape += [jax.ShapeDtypeStruct((xs_.shape[0], PROJ_COLS), BF16),
                      jax.ShapeDtypeStruct((xs_.shape[0], LANES), F32)]
        scratch.append(pltpu.VMEM((rows, D_MODEL), BF16))
    res = pl.pallas_call(
        functools.partial(_inproj_kernel, len(sets), len(cast), cast_blocks),
        grid=(t // tm, nj),
        in_specs=x_specs + [
            pl.BlockSpec((1, D_MODEL), lambda i, j: (0, 0)),
            pl.BlockSpec((pl.Element(tn), pl.Element(D_MODEL)), w_rows),
            pl.BlockSpec((LANES, D_MODEL), lambda i, j: (0, 0)),
        ] + cast_specs,
        out_specs=out_specs + cast_specs,
        out_shape=out_shape + [jax.ShapeDtypeStruct(c.shape, BF16) for c in cast],
        scratch_shapes=scratch,
        compiler_params=pltpu.CompilerParams(
            dimension_semantics=("arbitrary", "arbitrary"), vmem_limit_bytes=VMEM_LIMIT),
        name="inproj",
    )(*[st[0] for st in sets], lnw, wt, wdt, *cast)
    n_proj = 2 * len(sets)
    projs = tuple(tuple(res[2 * k:2 * k + 2]) for k in range(len(sets)))
    projs = projs if extra is not None else projs[0]
    return (projs, tuple(res[n_proj:])) if cast else projs


def halo_rows(ns):
    return max(2 * SUBLANES, ns * SUBLANES)


def _shift_matrices(q, ns, taps):
    L = q // ns
    hb = halo_rows(ns)
    r = np.arange(q)[:, None]
    col = np.arange(3 * hb + q)[None, :]
    seq, t = r // L, r % L
    mats = []
    for k in range(taps - 1):
        d = taps - 1 - k
        from_cur = (col >= 3 * hb) & (col - 3 * hb == r - d) & (t >= d)
        from_hist = (col < 3 * hb) & (col % hb == seq * SUBLANES + SUBLANES - d + t) & (t < d)
        mats.append(from_cur | from_hist)
    return jnp.asarray(np.stack(mats), dtype=BF16)


def _seqmix_kernel(Q, NS, CPS, one_step, pad_rows,
                   a_blk, sb_blk, sc_blk, sh_blk, dt_blk, hist_ref, schist_ref, st0_ref, shift_ref,
                   cw_ref, cb_ref, scw_ref, dtb_ref, alog_ref, drow_ref, nw_ref,
                   u_blk, v_blk, st_ref, cs_ref, scs_ref,
                   halo, scpad, xbc_s, y_s, xst_s):
    @pl.when(pl.program_id(1) == 0)
    def _():
        halo[...] = jnp.zeros_like(halo)
        for s in range(NS):
            halo[(s + 1) * SUBLANES - (SSD_CONV - 1):(s + 1) * SUBLANES, :] = hist_ref[s]
            scpad[s, SUBLANES - (SC_CONV - 1):SUBLANES, :] = schist_ref[s]
        if not one_step:
            st_ref[...] = st0_ref[...]

    src_ref = st0_ref if one_step else st_ref
    shared = (shift_ref, cw_ref, cb_ref, scw_ref, dtb_ref, alog_ref, drow_ref, nw_ref,
              src_ref, st_ref, cs_ref, scs_ref, halo, scpad, xbc_s, y_s, xst_s)
    row_blocks = (a_blk, sb_blk, sc_blk, sh_blk, dt_blk, u_blk, v_blk)
    if CPS == 1:
        _seqmix_chunk(Q, NS, pad_rows, *row_blocks, *shared)
    else:
        def body(k, carry):
            rows = pl.ds(pl.multiple_of(k * Q, Q), Q)
            _seqmix_chunk(Q, NS, pad_rows, *[r.at[rows] for r in row_blocks], *shared)
            return carry
        lax.fori_loop(0, CPS, body, 0)


def _seqmix_chunk(Q, NS, pad_rows, a_ref, sb_ref, sc_ref, sh_ref, dt_ref, u_ref, v_ref,
                  shift_ref, cw_ref, cb_ref, scw_ref, dtb_ref, alog_ref, drow_ref, nw_ref,
                  src_ref, st_ref, cs_ref, scs_ref, halo, scpad, xbc_s, y_s, xst_s):
    L = Q // NS
    lg = L.bit_length() - 1

    CW = 256
    hb = halo_rows(NS)
    for cc in range(0, SSD_CONV_DIM, CW):
        raw_b = a_ref[:, SSD_INNER + cc:SSD_INNER + cc + CW]
        hl = halo[:, cc:cc + CW]
        if hb > hl.shape[0]:
            hl = jnp.concatenate([hl, jnp.zeros((hb - hl.shape[0], CW), F32)], axis=0)
        h_hi = hl.astype(BF16)
        h_r = hl - h_hi.astype(F32)
        h_mid = h_r.astype(BF16)
        h_lo = (h_r - h_mid.astype(F32)).astype(BF16)
        ext = jnp.concatenate([h_hi, h_mid, h_lo, raw_b], axis=0)
        raw = raw_b.astype(F32)
        acc = raw * cw_ref[3:4, cc:cc + CW] + cb_ref[:, cc:cc + CW]
        for k in range(SSD_CONV - 1):
            acc = acc + jnp.dot(shift_ref[k], ext, preferred_element_type=F32) * cw_ref[k:k + 1, cc:cc + CW]
        xbc_s[:, cc:cc + CW] = acc * _sigmoid(acc)
        for s in range(NS):
            halo[s * SUBLANES:(s + 1) * SUBLANES, cc:cc + CW] = raw[(s + 1) * L - SUBLANES:(s + 1) * L]
    for s in range(NS):
        cs_ref[s] = halo[(s + 1) * SUBLANES - (SSD_CONV - 1):(s + 1) * SUBLANES, :]

    for cc in range(0, SC_WIDTH, CW):
        scb = sb_ref[:, cc:cc + CW].astype(F32)
        ch_all = sc_ref[:, cc:cc + CW].astype(F32) * sh_ref[:, cc:cc + CW].astype(F32)
        for s in range(NS):
            ch = ch_all[s * L:(s + 1) * L]
            scpad[s, SUBLANES:SUBLANES + L, cc:cc + CW] = ch
            acc = ch * scw_ref[2:3, cc:cc + CW]
            for k in range(SC_CONV - 1):
                acc = acc + scpad[s, 6 + k:6 + k + L, cc:cc + CW] * scw_ref[k:k + 1, cc:cc + CW]
            v_ref[s * L:(s + 1) * L, cc:cc + CW] = (scb[s * L:(s + 1) * L] * acc).astype(BF16)
    for s in range(NS):
        scs_ref[s] = scpad[s, L + 6:L + 8, :]
        scpad[s, 0:SUBLANES, :] = scpad[s, L:L + SUBLANES, :]

    def padrows(x):
        if Q == LANES:
            return x
        return jnp.concatenate([x, jnp.zeros((LANES - Q, x.shape[1]), x.dtype)], axis=0)

    li = lax.broadcasted_iota(jnp.int32, (Q, Q), 0)
    si = lax.broadcasted_iota(jnp.int32, (Q, Q), 1)
    same = (li >> lg) == (si >> lg)
    causal = jnp.logical_and(same, si <= li)
    tril01 = jnp.where(causal, 1.0, 0.0).astype(BF16)
    same01 = jnp.where(same, 1.0, 0.0).astype(BF16)

    dt = _softplus(dt_ref[...] + dtb_ref[...])
    if pad_rows:
        ri = lax.broadcasted_iota(jnp.int32, (Q, LANES), 0)
        dt = jnp.where(ri >= pad_rows, dt, 0.0)
    da = dt * (-jnp.exp(alog_ref[...]))
    acum = _dot01(tril01, da)
    tot = _dot01(same01, da)
    acum_t = padrows(acum).T
    tot_t = padrows(tot).T
    dt_t = padrows(dt).T
    w_t = jnp.exp(tot_t - acum_t) * dt_t
    acum2 = acum * LOG2E
    rowq_t = acum_t * LOG2E - jnp.log2(dt_t)

    left_head = lax.broadcasted_iota(jnp.int32, (Q, LANES), 1) < SSD_HEAD_DIM
    rowseq = lax.broadcasted_iota(jnp.int32, (Q, GROUP_COLS), 0) >> lg
    rowseq_p = lax.broadcasted_iota(jnp.int32, (LANES, LANES), 0) >> lg

    for g in range(SSD_GROUPS):
        b_g = xbc_s[:, SSD_INNER + g * SSD_STATE:SSD_INNER + (g + 1) * SSD_STATE]
        c_g = xbc_s[:, SSD_INNER + GROUP_COLS + g * SSD_STATE:SSD_INNER + GROUP_COLS + (g + 1) * SSD_STATE]
        b_gb = b_g.astype(BF16)
        c_gb = c_g.astype(BF16)
        cbm = _nt_dot(c_gb, b_gb)
        yo = None
        for s in range(NS):
            h_s = src_ref[s, g * GROUP_COLS:(g + 1) * GROUP_COLS, :].astype(BF16)
            yo_s = _nt_dot(c_gb, h_s)
            yo = yo_s if yo is None else jnp.where(rowseq == s, yo_s, yo)
        for rp in range(SSD_HPG // 2):
            h0 = g * SSD_HPG + 2 * rp
            cols = slice(h0 * SSD_HEAD_DIM, (h0 + 2) * SSD_HEAD_DIM)
            colbs, w_pair = [], []
            for h in (h0, h0 + 1):
                colb = jnp.broadcast_to(acum2[:, h:h + 1], (Q, LANES))
                rowb = jnp.broadcast_to(rowq_t[h:h + 1, 0:Q], (Q, Q))
                dec_dt = jnp.exp2(jnp.where(causal, colb[:, 0:Q] - rowb, NEG_BIG))
                w_pair.append((cbm * dec_dt).astype(BF16))
                colbs.append(colb)
            x_p = xbc_s[:, cols]
            rhs = jnp.concatenate([jnp.where(left_head, x_p, 0.0).astype(BF16),
                                   jnp.where(left_head, 0.0, x_p).astype(BF16)], axis=0)
            yd = jnp.dot(jnp.concatenate(w_pair, axis=1), rhs, preferred_element_type=F32)
            ecol = jnp.exp2(jnp.where(left_head, colbs[0], colbs[1]))
            y_s[:, cols] = yd + yo[:, rp * LANES:(rp + 1) * LANES] * ecol + drow_ref[:, cols] * x_p

    for jb in range(SSD_INNER // LANES):
        xst_s[jb * LANES:(jb + 1) * LANES, :] = padrows(xbc_s[:, jb * LANES:(jb + 1) * LANES]).T
    for s in range(NS):
        da_b = jnp.exp(jnp.broadcast_to(tot_t[:, s * L:s * L + 1], (LANES, LANES)))
        for g in range(SSD_GROUPS):
            b_p = padrows(xbc_s[:, SSD_INNER + g * SSD_STATE:SSD_INNER + (g + 1) * SSD_STATE])
            if NS > 1:
                b_p = jnp.where(rowseq_p == s, b_p, 0.0)
            pieces = []
            for r in range(SSD_HPG):
                h = g * SSD_HPG + r
                pieces.append(xst_s[h * SSD_HEAD_DIM:(h + 1) * SSD_HEAD_DIM, :] * w_t[h:h + 1, :])
            xw_t = jnp.concatenate(pieces, axis=0).astype(BF16)
            upd = jnp.dot(xw_t, b_p.astype(BF16), preferred_element_type=F32)
            for r in range(SSD_HPG):
                h = g * SSD_HPG + r
                rows = slice(h * SSD_HEAD_DIM, (h + 1) * SSD_HEAD_DIM)
                dec_h = jnp.broadcast_to(da_b[h:h + 1, :], (SSD_HEAD_DIM, SSD_STATE))
                st_ref[s, rows, :] = dec_h * src_ref[s, rows, :] + upd[r * SSD_HEAD_DIM:(r + 1) * SSD_HEAD_DIM, :]

    for g in range(SSD_GROUPS):
        cols = slice(g * GROUP_COLS, (g + 1) * GROUP_COLS)
        z = a_ref[:, cols].astype(F32)
        ug = y_s[:, cols] * (z * _sigmoid(z))
        ms = jnp.mean(ug * ug, axis=-1, keepdims=True)
        u_ref[:, cols] = (ug * lax.rsqrt(ms + EPS) * nw_ref[:, cols]).astype(BF16)


def _seqmix(proj, pdt, hist, schist, st0, prm, *, nb, nc, q, ns, pad_rows, init_bcast, cps=1):
    L = q // ns
    cw, cb, scw, dtb, alog, drow, nw = prm
    steps = nc // cps
    rows = q * cps
    sc_block0 = A_COLS // SC_WIDTH

    def tok(b, c):
        return (b * steps + c, 0)

    def init3(b, c):
        return (0 if init_bcast else b, 0, 0)

    def const2(b, c):
        return (0, 0)

    def per_b(b, c):
        return (b, 0, 0)

    n_seq = nb * ns
    shifts = _shift_matrices(q, ns, SSD_CONV)
    sc_specs = [pl.BlockSpec((rows, SC_WIDTH), lambda b, c, k=k: (b * steps + c, sc_block0 + k))
                for k in range(3)]
    return pl.pallas_call(
        functools.partial(_seqmix_kernel, q, ns, cps, steps == 1 and cps == 1, pad_rows),
        grid=(nb, steps),
        in_specs=[pl.BlockSpec((rows, A_COLS), tok)] + sc_specs + [
            pl.BlockSpec((rows, LANES), tok),
            pl.BlockSpec((ns, SSD_CONV - 1, SSD_CONV_DIM), init3),
            pl.BlockSpec((ns, SC_CONV - 1, SC_WIDTH), init3),
            pl.BlockSpec((ns, SSD_INNER, SSD_STATE), init3),
            pl.BlockSpec(shifts.shape, lambda b, c: (0, 0, 0)),
            pl.BlockSpec((SSD_CONV, SSD_CONV_DIM), const2),
            pl.BlockSpec((1, SSD_CONV_DIM), const2),
            pl.BlockSpec((SC_CONV, SC_WIDTH), const2),
            pl.BlockSpec((1, LANES), const2),
            pl.BlockSpec((1, LANES), const2),
            pl.BlockSpec((1, SSD_INNER), const2),
            pl.BlockSpec((1, SSD_INNER), const2),
        ],
        out_specs=[
            pl.BlockSpec((rows, SSD_INNER), tok),
            pl.BlockSpec((rows, SC_WIDTH), tok),
            pl.BlockSpec((ns, SSD_INNER, SSD_STATE), per_b),
            pl.BlockSpec((ns, SSD_CONV - 1, SSD_CONV_DIM), per_b),
            pl.BlockSpec((ns, SC_CONV - 1, SC_WIDTH), per_b),
        ],
        out_shape=[
            jax.ShapeDtypeStruct((nb * nc * q, SSD_INNER), BF16),
            jax.ShapeDtypeStruct((nb * nc * q, SC_WIDTH), BF16),
            jax.ShapeDtypeStruct((n_seq, SSD_INNER, SSD_STATE), F32),
            jax.ShapeDtypeStruct((n_seq, SSD_CONV - 1, SSD_CONV_DIM), F32),
            jax.ShapeDtypeStruct((n_seq, SC_CONV - 1, SC_WIDTH), F32),
        ],
        scratch_shapes=[
            pltpu.VMEM((ns * SUBLANES, SSD_CONV_DIM), F32),
            pltpu.VMEM((ns, SUBLANES + L, SC_WIDTH), F32),
            pltpu.VMEM((q, SSD_CONV_DIM), F32),
            pltpu.VMEM((q, SSD_INNER), F32),
            pltpu.VMEM((SSD_INNER, LANES), F32),
        ],
        compiler_params=pltpu.CompilerParams(
            dimension_semantics=("arbitrary", "arbitrary"), vmem_limit_bytes=VMEM_LIMIT),
        name="seqmix",
    )(proj, proj, proj, proj, pdt, hist, schist, st0, shifts, cw, cb, scw, dtb, alog, drow, nw)


def _merge_kernel(u_ref, v_ref, g_ref, x_ref, wa_ref, ws_ref, wo_ref, ln2_ref, wr_ref, br_ref,
                  xg_ref, cnt_s):
    i = pl.program_id(0)

    @pl.when(i == 0)
    def _():
        cnt_s[...] = jnp.zeros_like(cnt_s)

    y_ssd = jnp.dot(u_ref[...], wa_ref[...], preferred_element_type=F32)
    y_sc = jnp.dot(v_ref[...], ws_ref[...], preferred_element_type=F32)
    g1 = g_ref[:, 0:D_MODEL].astype(F32)
    g2 = g_ref[:, D_MODEL:2 * D_MODEL].astype(F32)
    merged = (_sigmoid(g1) * y_ssd + _sigmoid(g2) * y_sc).astype(BF16)
    x1 = x_ref[...] + jnp.dot(merged, wo_ref[...], preferred_element_type=F32)
    xg_ref[:, 0:D_MODEL] = x1
    ms = jnp.mean(x1 * x1, axis=-1, keepdims=True)
    h2 = (x1 * lax.rsqrt(ms + EPS) * ln2_ref[...]).astype(BF16)

    logits = jnp.dot(h2, wr_ref[...], preferred_element_type=F32) + br_ref[...]
    lane = lax.broadcasted_iota(jnp.int32, logits.shape, 1).astype(F32)
    big = float(LANES)
    gl = jnp.where(lane < N_EGROUPS, logits, NEG_BIG)
    gmax = jnp.max(gl, axis=-1, keepdims=True)
    g_sel = jnp.min(jnp.where(gl == gmax, lane, big), axis=-1, keepdims=True)
    gsum = jnp.sum(jnp.exp(gl - gmax), axis=-1, keepdims=True)
    g_prob = 1.0 / gsum
    lo = N_EGROUPS + EXPERTS_PER_GROUP * g_sel
    emask = jnp.logical_and(lane >= lo, lane < lo + EXPERTS_PER_GROUP)
    el = jnp.where(emask, logits, NEG_BIG)
    m1 = jnp.max(el, axis=-1, keepdims=True)
    e = jnp.where(emask, jnp.exp(el - m1), -1.0)
    i1 = jnp.min(jnp.where(e == 1.0, lane, big), axis=-1, keepdims=True)
    e_rest = jnp.where(lane == i1, -1.0, e)
    e2 = jnp.max(e_rest, axis=-1, keepdims=True)
    i2 = jnp.min(jnp.where(e_rest == e2, lane, big), axis=-1, keepdims=True)
    denom = 1.0 + e2
    w1 = g_prob / denom
    w2 = g_prob * e2 / denom
    gates = jnp.where(lane == i1, w1, 0.0) + jnp.where(lane == i2, w2, 0.0)

    tm = logits.shape[0]
    rb = min(tm, RANK_BLOCK)
    onehot = jnp.where(lane == g_sel, 1.0, 0.0).astype(BF16)
    ri = lax.broadcasted_iota(jnp.int32, (rb, rb), 0)
    ci = lax.broadcasted_iota(jnp.int32, (rb, rb), 1)
    tril01 = jnp.where(ci <= ri, 1.0, 0.0).astype(BF16)
    carry = cnt_s[...]
    incl_blocks = []
    for r0 in range(0, tm, rb):
        incl_b = jnp.dot(tril01, onehot[r0:r0 + rb], preferred_element_type=F32) + carry
        carry = incl_b[rb - 1:rb, :]
        incl_blocks.append(incl_b)
    cnt_s[...] = carry
    incl = incl_blocks[0] if len(incl_blocks) == 1 else jnp.concatenate(incl_blocks, axis=0)
    rank = jnp.sum(jnp.where(lane == g_sel, incl - 1.0, 0.0), axis=-1, keepdims=True)
    xg_ref[:, D_MODEL:D_MODEL + LANES] = jnp.where(
        lane == 0, g_sel.astype(F32), jnp.where(lane == 1, rank, gates))


def _merge(u, v, g, x, wa, ws, wo, ln2, wr, br, tm):
    t = x.shape[0]
    row = lambda i: (i, 0)
    const = lambda i: (0, 0)
    return pl.pallas_call(
        _merge_kernel,
        grid=(t // tm,),
        in_specs=[
            pl.BlockSpec((tm, SSD_INNER), row),
            pl.BlockSpec((tm, SC_WIDTH), row),
            pl.BlockSpec((tm, G_COLS), lambda i: (i, (A_COLS + S_COLS) // G_COLS)),
            pl.BlockSpec((tm, D_MODEL), row),
            pl.BlockSpec((SSD_INNER, D_MODEL), const, pipeline_mode=pl.Buffered(1)),
            pl.BlockSpec((SC_WIDTH, D_MODEL), const, pipeline_mode=pl.Buffered(1)),
            pl.BlockSpec((D_MODEL, D_MODEL), const, pipeline_mode=pl.Buffered(1)),
            pl.BlockSpec((1, D_MODEL), const),
            pl.BlockSpec((D_MODEL, LANES), const),
            pl.BlockSpec((1, LANES), const),
        ],
        out_specs=pl.BlockSpec((tm, XG_COLS), row),
        out_shape=jax.ShapeDtypeStruct((t, XG_COLS), F32),
        scratch_shapes=[pltpu.VMEM((1, LANES), F32)],
        compiler_params=pltpu.CompilerParams(
            dimension_semantics=("arbitrary",), vmem_limit_bytes=VMEM_LIMIT),
        name="merge",
    )(u, v, g, x, wa, ws, wo, ln2, wr, br)


def _moe_kernel(tmg, inv_ref, tg_ref, nv_ref, xg_hbm, wg_ref, wu_ref, wd_ref, ln2_ref, fw_ref,
                y_hbm, xbuf, ybuf, sem_in, sem_out):
    i = pl.program_id(0)
    n = pl.num_programs(0)
    slot = i % 2

    def row_in(tok, r, slt):
        return pltpu.make_async_copy(xg_hbm.at[pl.ds(tok, 1)], xbuf.at[slt, pl.ds(r, 1)], sem_in.at[slt])

    def row_out(tok, r, slt):
        return pltpu.make_async_copy(ybuf.at[slt, pl.ds(r, 1)], y_hbm.at[pl.ds(tok, 1)], sem_out.at[slt])

    unroll = 8

    def gather(tile, slt):
        def body(r, carry):
            row_in(inv_ref[tile * tmg + r], r, slt).start()
            return carry
        lax.fori_loop(0, tmg, body, 0, unroll=unroll)

    def wait_gather(slt):
        pltpu.make_async_copy(xg_hbm.at[pl.ds(0, tmg)], xbuf.at[slt], sem_in.at[slt]).wait()

    def scatter(tile, slt):
        def body(r, carry):
            row_out(inv_ref[tile * tmg + r], r, slt).start()
            return carry
        full = nv_ref[tile] == tmg

        @pl.when(full)
        def _():
            for r in range(tmg):
                body(r, 0)

        @pl.when(jnp.logical_not(full))
        def _():
            lax.fori_loop(0, nv_ref[tile], body, 0)

    def wait_scatter(tile, slt):
        def body(r, carry):
            row_out(0, r, slt).wait()
            return carry
        full = nv_ref[tile] == tmg

        @pl.when(full)
        def _():
            pltpu.make_async_copy(ybuf.at[slt], y_hbm.at[pl.ds(0, tmg)], sem_out.at[slt]).wait()

        @pl.when(jnp.logical_not(full))
        def _():
            lax.fori_loop(0, nv_ref[tile], body, 0)

    @pl.when(i == 0)
    def _():
        gather(0, 0)

    nxt = jnp.minimum(i + 1, n - 1)

    @pl.when(i >= 2)
    def _():
        wait_scatter(jnp.maximum(i - 2, 0), slot)

    @pl.when(nv_ref[i] == 0)
    def _():
        gather(nxt, 1 - slot)
        wait_gather(slot)

    @pl.when(nv_ref[i] > 0)
    def _():
        wait_gather(slot)
        x1 = xbuf[slot, :, 0:D_MODEL]
        for r in range(tmg // 2):
            row_in(inv_ref[nxt * tmg + r], r, 1 - slot).start()
        gate = xbuf[slot, :, D_MODEL:D_MODEL + LANES]
        for r in range(tmg // 2, tmg):
            row_in(inv_ref[nxt * tmg + r], r, 1 - slot).start()
        x1_res = xbuf[slot, :, 0:D_MODEL]
        ms = jnp.mean(x1 * x1, axis=-1, keepdims=True)
        h2 = (x1 * lax.rsqrt(ms + EPS) * ln2_ref[...]).astype(BF16)
        lane = lax.broadcasted_iota(jnp.int32, gate.shape, 1)
        first = N_EGROUPS + EXPERTS_PER_GROUP * tg_ref[i]
        pieces = []
        for e in range(EXPERTS_PER_GROUP):
            a = jnp.dot(h2, wg_ref[e], preferred_element_type=F32)
            up = jnp.dot(h2, wu_ref[e], preferred_element_type=F32)
            ge = jnp.sum(jnp.where(lane == first + e, gate, 0.0), axis=-1, keepdims=True)
            pieces.append((a * _sigmoid(a) * up * ge).astype(BF16))
        mo = jnp.dot(jnp.concatenate(pieces, axis=1), wd_ref[...], preferred_element_type=F32)
        x2 = x1_res + mo
        ms2 = jnp.mean(x2 * x2, axis=-1, keepdims=True)
        ybuf[slot] = x2 * lax.rsqrt(ms2 + EPS) * fw_ref[...]
        scatter(i, slot)

    @pl.when(i == n - 1)
    def _():
        wait_gather(1 - slot)

        @pl.when(i >= 1)
        def _():
            wait_scatter(jnp.maximum(i - 1, 0), 1 - slot)
        wait_scatter(i, slot)


def _moe(xg, inv, tile_group, n_valid, wg, wu, wd, ln2, fw, tmg):
    t = xg.shape[0]
    n_tiles = tile_group.shape[0]
    gcols = EXPERTS_PER_GROUP * D_EXPERT
    const = lambda i, inv, tg, nv: (0, 0)
    by_group = lambda i, inv, tg, nv: (tg[i], 0, 0)
    return pl.pallas_call(
        functools.partial(_moe_kernel, tmg),
        grid_spec=pltpu.PrefetchScalarGridSpec(
            num_scalar_prefetch=3,
            grid=(n_tiles,),
            in_specs=[
                pl.BlockSpec(memory_space=pltpu.HBM),
                pl.BlockSpec((EXPERTS_PER_GROUP, D_MODEL, D_EXPERT), by_group),
                pl.BlockSpec((EXPERTS_PER_GROUP, D_MODEL, D_EXPERT), by_group),
                pl.BlockSpec((None, gcols, D_MODEL), by_group),
                pl.BlockSpec((1, D_MODEL), const),
                pl.BlockSpec((1, D_MODEL), const),
            ],
            out_specs=pl.BlockSpec(memory_space=pltpu.HBM),
            scratch_shapes=[
                pltpu.VMEM((2, tmg, XG_COLS), F32),
                pltpu.VMEM((2, tmg, D_MODEL), F32),
                pltpu.SemaphoreType.DMA((2,)),
                pltpu.SemaphoreType.DMA((2,)),
            ],
        ),
        out_shape=jax.ShapeDtypeStruct((t, D_MODEL), F32),
        compiler_params=pltpu.CompilerParams(
            dimension_semantics=("arbitrary",), vmem_limit_bytes=VMEM_LIMIT),
        name="moe",
    )(inv, tile_group, n_valid, xg, wg, wu, wd, ln2, fw)


def _invperm_kernel(pos_ref, lo_ref, hi_ref, inv_ref):
    def clear(i, carry):
        inv_ref[i] = 0
        return carry
    for k in range(lo_ref.shape[0]):
        lax.fori_loop(lo_ref[k], hi_ref[k], clear, 0)

    def place(t, carry):
        inv_ref[pos_ref[t]] = t
        return carry
    lax.fori_loop(0, pos_ref.shape[0], place, 0, unroll=8)


def _invperm(pos, pad_lo, pad_hi, n):
    smem = pl.BlockSpec(memory_space=pltpu.SMEM)
    return pl.pallas_call(
        _invperm_kernel,
        in_specs=[smem, smem, smem],
        out_specs=smem,
        out_shape=jax.ShapeDtypeStruct((n,), jnp.int32),
        name="invperm",
    )(pos, pad_lo, pad_hi)


def _route(xg, tmg):
    t = xg.shape[0]
    g = xg[:, D_MODEL].astype(jnp.int32)
    rank = xg[:, D_MODEL + 1].astype(jnp.int32)
    counts = jnp.sum(g[:, None] == jnp.arange(N_EGROUPS, dtype=jnp.int32)[None, :], axis=0, dtype=jnp.int32)
    tiles_per = (counts + tmg - 1) // tmg
    tile_end = jnp.cumsum(tiles_per)
    tile_base = tile_end - tiles_per
    n_tiles = t // tmg + N_EGROUPS
    pos = tile_base[g] * tmg + rank
    pad_lo = jnp.concatenate([tile_base * tmg + counts, tile_end[-1:] * tmg]).astype(jnp.int32)
    pad_hi = jnp.concatenate([tile_end * tmg, jnp.full((1,), n_tiles * tmg, jnp.int32)]).astype(jnp.int32)
    inv = _invperm(pos, pad_lo, pad_hi, n_tiles * tmg)
    ti = jnp.arange(n_tiles, dtype=jnp.int32)
    tg = jnp.minimum(jnp.sum(ti[:, None] >= tile_end[None, :], axis=1, dtype=jnp.int32), N_EGROUPS - 1)
    nv = jnp.clip(counts[tg] - (ti - tile_base[tg]) * tmg, 0, tmg)
    nv = jnp.where(ti < tile_end[-1], nv, 0).astype(jnp.int32)
    return inv, tg, nv


def _pad_lanes(v, n=LANES):
    v = v.reshape(1, -1).astype(F32)
    return jnp.pad(v, ((0, 0), (0, n - v.shape[1])))


def kernel(x_prompt, x_sample, state_ssm, state_ssd_conv, state_short_conv, meta_tokens, ln1_w, w_in,
           ssd_conv_w, ssd_conv_b, ssd_dt_bias, ssd_A_log, ssd_D, ssd_norm_w, w_ssd_out, sc_conv_w,
           w_sc_out, w_o, ln2_w, w_rg, b_rg, w_re, b_re, w_gate, w_up, w_down, final_norm_w):
    assert ln1_w.shape[0] == 1, "single-layer trunk"
    bp, sp, _ = x_prompt.shape
    bs, ss, _ = x_sample.shape
    l = 0

    wt = jnp.swapaxes(w_in[l], 0, 1)
    wdt = jnp.pad(wt[A_COLS:A_COLS + SSD_HEADS], ((0, LANES - SSD_HEADS), (0, 0))).astype(BF16)
    lnw = ln1_w[l].reshape(1, D_MODEL)
    prm = (ssd_conv_w[l], ssd_conv_b[l].reshape(1, -1), sc_conv_w[l],
           _pad_lanes(ssd_dt_bias[l]), _pad_lanes(ssd_A_log[l]),
           jnp.repeat(ssd_D[l].astype(F32), SSD_HEAD_DIM).reshape(1, SSD_INNER),
           ssd_norm_w[l].reshape(1, SSD_INNER))
    wa = w_ssd_out[l].astype(BF16)
    ws = w_sc_out[l].astype(BF16)
    wo = w_o[l].astype(BF16)
    ln2 = ln2_w[l].reshape(1, D_MODEL)
    wr = jnp.pad(jnp.concatenate([w_rg[l], w_re[l]], axis=1),
                 ((0, 0), (0, LANES - N_EGROUPS - N_EXPERTS))).astype(BF16)
    br = _pad_lanes(jnp.concatenate([b_rg[l], b_re[l]]))
    gcols = EXPERTS_PER_GROUP * D_EXPERT
    fw = final_norm_w.reshape(1, D_MODEL)
    halves = 2 * N_EXPERTS
    cast32 = (w_gate[l].reshape(halves, D_MODEL // 2, D_EXPERT),
              w_up[l].reshape(halves, D_MODEL // 2, D_EXPERT),
              w_down[l].reshape(halves, D_EXPERT // 2, D_MODEL))

    def mlp_tail(u, v, g, x, tmg):
        xg = _merge(u, v, g, x, wa, ws, wo, ln2, wr, br, tm=1024)
        inv, tg, nv = _route(xg, tmg)
        return _moe(xg, inv, tg, nv, wg, wu, wd, ln2, fw, tmg)

    q = LANES
    xm = jnp.pad(meta_tokens.astype(F32), ((q - N_META, 0), (0, 0)))
    xs = x_sample.reshape(bs * ss, D_MODEL)
    (sproj, sdt), (mproj, mdt) = _inproj(xs, lnw, wt, wdt, tm=bs * ss, extra=xm)
    zeros_hist = jnp.zeros((1, SSD_CONV - 1, SSD_CONV_DIM), F32)
    zeros_sch = jnp.zeros((1, SC_CONV - 1, SC_WIDTH), F32)
    zeros_st = jnp.zeros((1, SSD_INNER, SSD_STATE), F32)
    _, _, m_st, m_cs, m_scs = _seqmix(mproj, mdt, zeros_hist, zeros_sch, zeros_st, prm,
                                      nb=1, nc=1, q=q, ns=1, pad_rows=q - N_META, init_bcast=True)

    xp = x_prompt.reshape(bp * sp, D_MODEL)
    (pproj, pdt), (wg, wu, wd) = _inproj(xp, lnw, wt, wdt, tm=2048, cast=cast32)
    wg = wg.reshape(N_EXPERTS, D_MODEL, D_EXPERT)
    wu = wu.reshape(N_EXPERTS, D_MODEL, D_EXPERT)
    wd = wd.reshape(N_EGROUPS, gcols, D_MODEL)
    pu, pv, p_st, p_cs, p_scs = _seqmix(pproj, pdt, m_cs, m_scs, m_st, prm, nb=bp, nc=sp // q, q=q,
                                        ns=1, pad_rows=0, init_bcast=True, cps=4)
    y_prompt = mlp_tail(pu, pv, pproj, xp, MOE_TILE).reshape(bp, sp, D_MODEL)

    ns = 8
    su, sv, s_st, s_cs, s_scs = _seqmix(sproj, sdt, state_ssd_conv[l], state_short_conv[l],
                                        state_ssm[l].reshape(bs, SSD_INNER, SSD_STATE), prm,
                                        nb=bs // ns, nc=1, q=ns * ss, ns=ns, pad_rows=0, init_bcast=False)
    y_sample = mlp_tail(su, sv, sproj, xs, MOE_TILE_SMALL).reshape(bs, ss, D_MODEL)

    hshape = (SSD_HEADS, SSD_HEAD_DIM, SSD_STATE)
    return (y_prompt, y_sample,
            p_st.reshape(1, bp, *hshape), p_cs[None], p_scs[None],
            s_st.reshape(1, bs, *hshape), s_cs[None], s_scs[None])
```

```python
import functools

import jax
import jax.numpy as jnp
import numpy as np
from jax import lax
from jax.experimental import pallas as pl
from jax.experimental.pallas import tpu as pltpu

F32 = jnp.float32
BF16 = jnp.bfloat16

D_MODEL = 1024
N_META = 16
SSD_INNER = 2048
SSD_HEAD_DIM = 64
SSD_HEADS = 32
SSD_GROUPS = 4
SSD_HPG = 8
SSD_STATE = 128
SSD_CONV = 4
SSD_CONV_DIM = 3072
SC_WIDTH = 1024
SC_CONV = 3
N_EGROUPS = 4
EXPERTS_PER_GROUP = 8
N_EXPERTS = 32
D_EXPERT = 256
EPS = 1e-6

LANES = 128
SUBLANES = 8
GROUP_COLS = SSD_HPG * SSD_HEAD_DIM
A_COLS = SSD_INNER + SSD_CONV_DIM
S_COLS = 3 * SC_WIDTH
G_COLS = 2 * D_MODEL
PROJ_COLS = A_COLS + S_COLS + G_COLS
XG_COLS = D_MODEL + LANES
MOE_TILE = 512
MOE_TILE_SMALL = 128
RANK_BLOCK = 512
NEG_BIG = -1e30
LOG2E = 1.4426950408889634
VMEM_LIMIT = 56 * 1024 * 1024


def _nt_dot(a, b):
    return lax.dot_general(a, b, (((1,), (1,)), ((), ())), preferred_element_type=F32)


def _dot01(m01_bf16, x):
    hi = x.astype(BF16)
    r1 = x - hi.astype(F32)
    mid = r1.astype(BF16)
    lo = (r1 - mid.astype(F32)).astype(BF16)
    out = jnp.dot(m01_bf16, hi, preferred_element_type=F32)
    out = out + jnp.dot(m01_bf16, mid, preferred_element_type=F32)
    return out + jnp.dot(m01_bf16, lo, preferred_element_type=F32)


def _softplus(x):
    return jnp.maximum(x, 0.0) + jnp.log1p(jnp.exp(-jnp.abs(x)))


def _sigmoid(x):
    return 1.0 / (1.0 + jnp.exp(-x))


def _inproj_kernel(n_sets, n_cast, cast_blocks, *refs):
    x_refs = refs[:n_sets]
    lnw_ref, wt_ref, wdt_ref = refs[n_sets:n_sets + 3]
    cast_in = refs[n_sets + 3:n_sets + 3 + n_cast]
    outs = refs[n_sets + 3 + n_cast:n_sets + 3 + n_cast + 2 * n_sets]
    cast_out = refs[n_sets + 3 + n_cast + 2 * n_sets:n_sets + 3 + 2 * n_cast + 2 * n_sets]
    h_refs = refs[n_sets + 3 + 2 * n_cast + 2 * n_sets:]

    if n_cast:
        step = pl.program_id(0) * pl.num_programs(1) + pl.program_id(1)

        @pl.when(step < cast_blocks)
        def _():
            for src, dst in zip(cast_in, cast_out):
                dst[...] = src[...].astype(BF16)

    @pl.when(pl.program_id(1) == 0)
    def _():
        for k in range(n_sets):
            x = x_refs[k][...]
            ms = jnp.mean(x * x, axis=-1, keepdims=True)
            h = (x * lax.rsqrt(ms + EPS) * lnw_ref[...]).astype(BF16)
            h_refs[k][...] = h
            outs[2 * k + 1][...] = _nt_dot(h, wdt_ref[...])

    w = wt_ref[...].astype(BF16)
    for k in range(n_sets):
        outs[2 * k][...] = _nt_dot(h_refs[k][...], w).astype(BF16)


def _inproj(x, lnw, wt, wdt, tm, tn=1024, extra=None, cast=()):
    t = x.shape[0]
    n_a = A_COLS // tn
    nj = PROJ_COLS // tn
    cast_blocks = cast[0].shape[0] if cast else 0
    assert all(c.shape[0] == cast_blocks for c in cast) and cast_blocks <= (t // tm) * nj

    def cast_idx(i, j):
        return (jnp.minimum(i * nj + j, cast_blocks - 1), 0, 0)

    cast_specs = [pl.BlockSpec((1,) + c.shape[1:], cast_idx) for c in cast]

    def w_rows(i, j):
        return (pl.multiple_of(jnp.where(j < n_a, j * tn, j * tn + SSD_HEADS), SSD_HEADS), 0)

    sets = [(x, tm, lambda i: i)]
    if extra is not None:
        assert t == tm, "extra rows are recomputed per row tile"
        sets.append((extra, extra.shape[0], lambda i: 0))
    x_specs, out_specs, out_shape, scratch = [], [], [], []
    for xs_, rows, ri in sets:
        x_specs.append(pl.BlockSpec((rows, D_MODEL), lambda i, j, ri=ri: (ri(i), 0)))
        out_specs += [pl.BlockSpec((rows, tn), lambda i, j, ri=ri: (ri(i), j)),
                      pl.BlockSpec((rows, LANES), lambda i, j, ri=ri: (ri(i), 0))]
        out_shape += [jax.ShapeDtypeStruct((xs_.shape[0], PROJ_COLS), BF16),
                      jax.ShapeDtypeStruct((xs_.shape[0], LANES), F32)]
        scratch.append(pltpu.VMEM((rows, D_MODEL), BF16))
    res = pl.pallas_call(
        functools.partial(_inproj_kernel, len(sets), len(cast), cast_blocks),
        grid=(t // tm, nj),
        in_specs=x_specs + [
            pl.BlockSpec((1, D_MODEL), lambda i, j: (0, 0)),
            pl.BlockSpec((pl.Element(tn), pl.Element(D_MODEL)), w_rows),
            pl.BlockSpec((LANES, D_MODEL), lambda i, j: (0, 0)),
        ] + cast_specs,
        out_specs=out_specs + cast_specs,
        out_shape=out_shape + [jax.ShapeDtypeStruct(c.shape, BF16) for c in cast],
        scratch_shapes=scratch,
        compiler_params=pltpu.CompilerParams(
            dimension_semantics=("arbitrary", "arbitrary"), vmem_limit_bytes=VMEM_LIMIT),
        name="inproj",
    )(*[st[0] for st in sets], lnw, wt, wdt, *cast)
    n_proj = 2 * len(sets)
    projs = tuple(tuple(res[2 * k:2 * k + 2]) for k in range(len(sets)))
    projs = projs if extra is not None else projs[0]
    return (projs, tuple(res[n_proj:])) if cast else projs


def halo_rows(ns):
    return max(2 * SUBLANES, ns * SUBLANES)


def _shift_matrices(q, ns, taps):
    L = q // ns
    hb = halo_rows(ns)
    r = np.arange(q)[:, None]
    col = np.arange(3 * hb + q)[None, :]
    seq, t = r // L, r % L
    mats = []
    for k in range(taps - 1):
        d = taps - 1 - k
        from_cur = (col >= 3 * hb) & (col - 3 * hb == r - d) & (t >= d)
        from_hist = (col < 3 * hb) & (col % hb == seq * SUBLANES + SUBLANES - d + t) & (t < d)
        mats.append(from_cur | from_hist)
    return jnp.asarray(np.stack(mats), dtype=BF16)


def _seqmix_kernel(Q, NS, CPS, one_step, pad_rows,
                   a_blk, sb_blk, sc_blk, sh_blk, dt_blk, hist_ref, schist_ref, st0_ref, shift_ref,
                   cw_ref, cb_ref, scw_ref, dtb_ref, alog_ref, drow_ref, nw_ref,
                   u_blk, v_blk, st_ref, cs_ref, scs_ref,
                   halo, scpad, xbc_s, y_s, xst_s):
    @pl.when(pl.program_id(1) == 0)
    def _():
        halo[...] = jnp.zeros_like(halo)
        for s in range(NS):
            halo[(s + 1) * SUBLANES - (SSD_CONV - 1):(s + 1) * SUBLANES, :] = hist_ref[s]
            scpad[s, SUBLANES - (SC_CONV - 1):SUBLANES, :] = schist_ref[s]
        if not one_step:
            st_ref[...] = st0_ref[...]

    src_ref = st0_ref if one_step else st_ref
    shared = (shift_ref, cw_ref, cb_ref, scw_ref, dtb_ref, alog_ref, drow_ref, nw_ref,
              src_ref, st_ref, cs_ref, scs_ref, halo, scpad, xbc_s, y_s, xst_s)
    row_blocks = (a_blk, sb_blk, sc_blk, sh_blk, dt_blk, u_blk, v_blk)
    if CPS == 1:
        _seqmix_chunk(Q, NS, pad_rows, *row_blocks, *shared)
    else:
        def body(k, carry):
            rows = pl.ds(pl.multiple_of(k * Q, Q), Q)
            _seqmix_chunk(Q, NS, pad_rows, *[r.at[rows] for r in row_blocks], *shared)
            return carry
        lax.fori_loop(0, CPS, body, 0)


def _seqmix_chunk(Q, NS, pad_rows, a_ref, sb_ref, sc_ref, sh_ref, dt_ref, u_ref, v_ref,
                  shift_ref, cw_ref, cb_ref, scw_ref, dtb_ref, alog_ref, drow_ref, nw_ref,
                  src_ref, st_ref, cs_ref, scs_ref, halo, scpad, xbc_s, y_s, xst_s):
    L = Q // NS
    lg = L.bit_length() - 1

    CW = 256
    hb = halo_rows(NS)
    for cc in range(0, SSD_CONV_DIM, CW):
        raw_b = a_ref[:, SSD_INNER + cc:SSD_INNER + cc + CW]
        hl = halo[:, cc:cc + CW]
        if hb > hl.shape[0]:
            hl = jnp.concatenate([hl, jnp.zeros((hb - hl.shape[0], CW), F32)], axis=0)
        h_hi = hl.astype(BF16)
        h_r = hl - h_hi.astype(F32)
        h_mid = h_r.astype(BF16)
        h_lo = (h_r - h_mid.astype(F32)).astype(BF16)
        ext = jnp.concatenate([h_hi, h_mid, h_lo, raw_b], axis=0)
        raw = raw_b.astype(F32)
        acc = raw * cw_ref[3:4, cc:cc + CW] + cb_ref[:, cc:cc + CW]
        for k in range(SSD_CONV - 1):
            acc = acc + jnp.dot(shift_ref[k], ext, preferred_element_type=F32) * cw_ref[k:k + 1, cc:cc + CW]
        xbc_s[:, cc:cc + CW] = acc * _sigmoid(acc)
        for s in range(NS):
            halo[s * SUBLANES:(s + 1) * SUBLANES, cc:cc + CW] = raw[(s + 1) * L - SUBLANES:(s + 1) * L]
    for s in range(NS):
        cs_ref[s] = halo[(s + 1) * SUBLANES - (SSD_CONV - 1):(s + 1) * SUBLANES, :]

    for cc in range(0, SC_WIDTH, CW):
        scb = sb_ref[:, cc:cc + CW].astype(F32)
        ch_all = sc_ref[:, cc:cc + CW].astype(F32) * sh_ref[:, cc:cc + CW].astype(F32)
        for s in range(NS):
            ch = ch_all[s * L:(s + 1) * L]
            scpad[s, SUBLANES:SUBLANES + L, cc:cc + CW] = ch
            acc = ch * scw_ref[2:3, cc:cc + CW]
            for k in range(SC_CONV - 1):
                acc = acc + scpad[s, 6 + k:6 + k + L, cc:cc + CW] * scw_ref[k:k + 1, cc:cc + CW]
            v_ref[s * L:(s + 1) * L, cc:cc + CW] = (scb[s * L:(s + 1) * L] * acc).astype(BF16)
    for s in range(NS):
        scs_ref[s] = scpad[s, L + 6:L + 8, :]
        scpad[s, 0:SUBLANES, :] = scpad[s, L:L + SUBLANES, :]

    def padrows(x):
        if Q == LANES:
            return x
        return jnp.concatenate([x, jnp.zeros((LANES - Q, x.shape[1]), x.dtype)], axis=0)

    li = lax.broadcasted_iota(jnp.int32, (Q, Q), 0)
    si = lax.broadcasted_iota(jnp.int32, (Q, Q), 1)
    same = (li >> lg) == (si >> lg)
    causal = jnp.logical_and(same, si <= li)
    tril01 = jnp.where(causal, 1.0, 0.0).astype(BF16)
    same01 = jnp.where(same, 1.0, 0.0).astype(BF16)

    dt = _softplus(dt_ref[...] + dtb_ref[...])
    if pad_rows:
        ri = lax.broadcasted_iota(jnp.int32, (Q, LANES), 0)
        dt = jnp.where(ri >= pad_rows, dt, 0.0)
    da = dt * (-jnp.exp(alog_ref[...]))
    acum = _dot01(tril01, da)
    tot = _dot01(same01, da)
    acum_t = padrows(acum).T
    tot_t = padrows(tot).T
    dt_t = padrows(dt).T
    w_t = jnp.exp(tot_t - acum_t) * dt_t
    acum2 = acum * LOG2E
    rowq_t = acum_t * LOG2E - jnp.log2(dt_t)

    left_head = lax.broadcasted_iota(jnp.int32, (Q, LANES), 1) < SSD_HEAD_DIM
    rowseq = lax.broadcasted_iota(jnp.int32, (Q, GROUP_COLS), 0) >> lg
    rowseq_p = lax.broadcasted_iota(jnp.int32, (LANES, LANES), 0) >> lg

    for g in range(SSD_GROUPS):
        b_g = xbc_s[:, SSD_INNER + g * SSD_STATE:SSD_INNER + (g + 1) * SSD_STATE]
        c_g = xbc_s[:, SSD_INNER + GROUP_COLS + g * SSD_STATE:SSD_INNER + GROUP_COLS + (g + 1) * SSD_STATE]
        b_gb = b_g.astype(BF16)
        c_gb = c_g.astype(BF16)
        cbm = _nt_dot(c_gb, b_gb)
        yo = None
        for s in range(NS):
            h_s = src_ref[s, g * GROUP_COLS:(g + 1) * GROUP_COLS, :].astype(BF16)
            yo_s = _nt_dot(c_gb, h_s)
            yo = yo_s if yo is None else jnp.where(rowseq == s, yo_s, yo)
        for rp in range(SSD_HPG // 2):
            h0 = g * SSD_HPG + 2 * rp
            cols = slice(h0 * SSD_HEAD_DIM, (h0 + 2) * SSD_HEAD_DIM)
            colbs, w_pair = [], []
            for h in (h0, h0 + 1):
                colb = jnp.broadcast_to(acum2[:, h:h + 1], (Q, LANES))
                rowb = jnp.broadcast_to(rowq_t[h:h + 1, 0:Q], (Q, Q))
                dec_dt = jnp.exp2(jnp.where(causal, colb[:, 0:Q] - rowb, NEG_BIG))
                w_pair.append((cbm * dec_dt).astype(BF16))
                colbs.append(colb)
            x_p = xbc_s[:, cols]
            rhs = jnp.concatenate([jnp.where(left_head, x_p, 0.0).astype(BF16),
                                   jnp.where(left_head, 0.0, x_p).astype(BF16)], axis=0)
            yd = jnp.dot(jnp.concatenate(w_pair, axis=1), rhs, preferred_element_type=F32)
            ecol = jnp.exp2(jnp.where(left_head, colbs[0], colbs[1]))
            y_s[:, cols] = yd + yo[:, rp * LANES:(rp + 1) * LANES] * ecol + drow_ref[:, cols] * x_p

    for jb in range(SSD_INNER // LANES):
        xst_s[jb * LANES:(jb + 1) * LANES, :] = padrows(xbc_s[:, jb * LANES:(jb + 1) * LANES]).T
    for s in range(NS):
        da_b = jnp.exp(jnp.broadcast_to(tot_t[:, s * L:s * L + 1], (LANES, LANES)))
        for g in range(SSD_GROUPS):
            b_p = padrows(xbc_s[:, SSD_INNER + g * SSD_STATE:SSD_INNER + (g + 1) * SSD_STATE])
            if NS > 1:
                b_p = jnp.where(rowseq_p == s, b_p, 0.0)
            pieces = []
            for r in range(SSD_HPG):
                h = g * SSD_HPG + r
                pieces.append(xst_s[h * SSD_HEAD_DIM:(h + 1) * SSD_HEAD_DIM, :] * w_t[h:h + 1, :])
            xw_t = jnp.concatenate(pieces, axis=0).astype(BF16)
            upd = jnp.dot(xw_t, b_p.astype(BF16), preferred_element_type=F32)
            for r in range(SSD_HPG):
                h = g * SSD_HPG + r
                rows = slice(h * SSD_HEAD_DIM, (h + 1) * SSD_HEAD_DIM)
                dec_h = jnp.broadcast_to(da_b[h:h + 1, :], (SSD_HEAD_DIM, SSD_STATE))
                st_ref[s, rows, :] = dec_h * src_ref[s, rows, :] + upd[r * SSD_HEAD_DIM:(r + 1) * SSD_HEAD_DIM, :]

    for g in range(SSD_GROUPS):
        cols = slice(g * GROUP_COLS, (g + 1) * GROUP_COLS)
        z = a_ref[:, cols].astype(F32)
        ug = y_s[:, cols] * (z * _sigmoid(z))
        ms = jnp.mean(ug * ug, axis=-1, keepdims=True)
        u_ref[:, cols] = (ug * lax.rsqrt(ms + EPS) * nw_ref[:, cols]).astype(BF16)


def _seqmix(proj, pdt, hist, schist, st0, prm, *, nb, nc, q, ns, pad_rows, init_bcast, cps=1):
    L = q // ns
    cw, cb, scw, dtb, alog, drow, nw = prm
    steps = nc // cps
    rows = q * cps
    sc_block0 = A_COLS // SC_WIDTH

    def tok(b, c):
        return (b * steps + c, 0)

    def init3(b, c):
        return (0 if init_bcast else b, 0, 0)

    def const2(b, c):
        return (0, 0)

    def per_b(b, c):
        return (b, 0, 0)

    n_seq = nb * ns
    shifts = _shift_matrices(q, ns, SSD_CONV)
    sc_specs = [pl.BlockSpec((rows, SC_WIDTH), lambda b, c, k=k: (b * steps + c, sc_block0 + k))
                for k in range(3)]
    return pl.pallas_call(
        functools.partial(_seqmix_kernel, q, ns, cps, steps == 1 and cps == 1, pad_rows),
        grid=(nb, steps),
        in_specs=[pl.BlockSpec((rows, A_COLS), tok)] + sc_specs + [
            pl.BlockSpec((rows, LANES), tok),
            pl.BlockSpec((ns, SSD_CONV - 1, SSD_CONV_DIM), init3),
            pl.BlockSpec((ns, SC_CONV - 1, SC_WIDTH), init3),
            pl.BlockSpec((ns, SSD_INNER, SSD_STATE), init3),
            pl.BlockSpec(shifts.shape, lambda b, c: (0, 0, 0)),
            pl.BlockSpec((SSD_CONV, SSD_CONV_DIM), const2),
            pl.BlockSpec((1, SSD_CONV_DIM), const2),
            pl.BlockSpec((SC_CONV, SC_WIDTH), const2),
            pl.BlockSpec((1, LANES), const2),
            pl.BlockSpec((1, LANES), const2),
            pl.BlockSpec((1, SSD_INNER), const2),
            pl.BlockSpec((1, SSD_INNER), const2),
        ],
        out_specs=[
            pl.BlockSpec((rows, SSD_INNER), tok),
            pl.BlockSpec((rows, SC_WIDTH), tok),
            pl.BlockSpec((ns, SSD_INNER, SSD_STATE), per_b),
            pl.BlockSpec((ns, SSD_CONV - 1, SSD_CONV_DIM), per_b),
            pl.BlockSpec((ns, SC_CONV - 1, SC_WIDTH), per_b),
        ],
        out_shape=[
            jax.ShapeDtypeStruct((nb * nc * q, SSD_INNER), BF16),
            jax.ShapeDtypeStruct((nb * nc * q, SC_WIDTH), BF16),
            jax.ShapeDtypeStruct((n_seq, SSD_INNER, SSD_STATE), F32),
            jax.ShapeDtypeStruct((n_seq, SSD_CONV - 1, SSD_CONV_DIM), F32),
            jax.ShapeDtypeStruct((n_seq, SC_CONV - 1, SC_WIDTH), F32),
        ],
        scratch_shapes=[
            pltpu.VMEM((ns * SUBLANES, SSD_CONV_DIM), F32),
            pltpu.VMEM((ns, SUBLANES + L, SC_WIDTH), F32),
            pltpu.VMEM((q, SSD_CONV_DIM), F32),
            pltpu.VMEM((q, SSD_INNER), F32),
            pltpu.VMEM((SSD_INNER, LANES), F32),
        ],
        compiler_params=pltpu.CompilerParams(
            dimension_semantics=("arbitrary", "arbitrary"), vmem_limit_bytes=VMEM_LIMIT),
        name="seqmix",
    )(proj, proj, proj, proj, pdt, hist, schist, st0, shifts, cw, cb, scw, dtb, alog, drow, nw)


def _merge_kernel(u_ref, v_ref, g_ref, x_ref, wa_ref, ws_ref, wo_ref, ln2_ref, wr_ref, br_ref,
                  xg_ref, cnt_s):
    i = pl.program_id(0)

    @pl.when(i == 0)
    def _():
        cnt_s[...] = jnp.zeros_like(cnt_s)

    y_ssd = jnp.dot(u_ref[...], wa_ref[...], preferred_element_type=F32)
    y_sc = jnp.dot(v_ref[...], ws_ref[...], preferred_element_type=F32)
    g1 = g_ref[:, 0:D_MODEL].astype(F32)
    g2 = g_ref[:, D_MODEL:2 * D_MODEL].astype(F32)
    merged = (_sigmoid(g1) * y_ssd + _sigmoid(g2) * y_sc).astype(BF16)
    x1 = x_ref[...] + jnp.dot(merged, wo_ref[...], preferred_element_type=F32)
    xg_ref[:, 0:D_MODEL] = x1
    ms = jnp.mean(x1 * x1, axis=-1, keepdims=True)
    h2 = (x1 * lax.rsqrt(ms + EPS) * ln2_ref[...]).astype(BF16)

    logits = jnp.dot(h2, wr_ref[...], preferred_element_type=F32) + br_ref[...]
    lane = lax.broadcasted_iota(jnp.int32, logits.shape, 1).astype(F32)
    big = float(LANES)
    gl = jnp.where(lane < N_EGROUPS, logits, NEG_BIG)
    gmax = jnp.max(gl, axis=-1, keepdims=True)
    g_sel = jnp.min(jnp.where(gl == gmax, lane, big), axis=-1, keepdims=True)
    gsum = jnp.sum(jnp.exp(gl - gmax), axis=-1, keepdims=True)
    g_prob = 1.0 / gsum
    lo = N_EGROUPS + EXPERTS_PER_GROUP * g_sel
    emask = jnp.logical_and(lane >= lo, lane < lo + EXPERTS_PER_GROUP)
    el = jnp.where(emask, logits, NEG_BIG)
    m1 = jnp.max(el, axis=-1, keepdims=True)
    e = jnp.where(emask, jnp.exp(el - m1), -1.0)
    i1 = jnp.min(jnp.where(e == 1.0, lane, big), axis=-1, keepdims=True)
    e_rest = jnp.where(lane == i1, -1.0, e)
    e2 = jnp.max(e_rest, axis=-1, keepdims=True)
    i2 = jnp.min(jnp.where(e_rest == e2, lane, big), axis=-1, keepdims=True)
    denom = 1.0 + e2
    w1 = g_prob / denom
    w2 = g_prob * e2 / denom
    gates = jnp.where(lane == i1, w1, 0.0) + jnp.where(lane == i2, w2, 0.0)

    tm = logits.shape[0]
    rb = min(tm, RANK_BLOCK)
    onehot = jnp.where(lane == g_sel, 1.0, 0.0).astype(BF16)
    ri = lax.broadcasted_iota(jnp.int32, (rb, rb), 0)
    ci = lax.broadcasted_iota(jnp.int32, (rb, rb), 1)
    tril01 = jnp.where(ci <= ri, 1.0, 0.0).astype(BF16)
    carry = cnt_s[...]
    incl_blocks = []
    for r0 in range(0, tm, rb):
        incl_b = jnp.dot(tril01, onehot[r0:r0 + rb], preferred_element_type=F32) + carry
        carry = incl_b[rb - 1:rb, :]
        incl_blocks.append(incl_b)
    cnt_s[...] = carry
    incl = incl_blocks[0] if len(incl_blocks) == 1 else jnp.concatenate(incl_blocks, axis=0)
    rank = jnp.sum(jnp.where(lane == g_sel, incl - 1.0, 0.0), axis=-1, keepdims=True)
    xg_ref[:, D_MODEL:D_MODEL + LANES] = jnp.where(
        lane == 0, g_sel.astype(F32), jnp.where(lane == 1, rank, gates))


def _merge(u, v, g, x, wa, ws, wo, ln2, wr, br, tm):
    t = x.shape[0]
    row = lambda i: (i, 0)
    const = lambda i: (0, 0)
    return pl.pallas_call(
        _merge_kernel,
        grid=(t // tm,),
        in_specs=[
            pl.BlockSpec((tm, SSD_INNER), row),
            pl.BlockSpec((tm, SC_WIDTH), row),
            pl.BlockSpec((tm, G_COLS), lambda i: (i, (A_COLS + S_COLS) // G_COLS)),
            pl.BlockSpec((tm, D_MODEL), row),
            pl.BlockSpec((SSD_INNER, D_MODEL), const, pipeline_mode=pl.Buffered(1)),
            pl.BlockSpec((SC_WIDTH, D_MODEL), const, pipeline_mode=pl.Buffered(1)),
            pl.BlockSpec((D_MODEL, D_MODEL), const, pipeline_mode=pl.Buffered(1)),
            pl.BlockSpec((1, D_MODEL), const),
            pl.BlockSpec((D_MODEL, LANES), const),
            pl.BlockSpec((1, LANES), const),
        ],
        out_specs=pl.BlockSpec((tm, XG_COLS), row),
        out_shape=jax.ShapeDtypeStruct((t, XG_COLS), F32),
        scratch_shapes=[pltpu.VMEM((1, LANES), F32)],
        compiler_params=pltpu.CompilerParams(
            dimension_semantics=("arbitrary",), vmem_limit_bytes=VMEM_LIMIT),
        name="merge",
    )(u, v, g, x, wa, ws, wo, ln2, wr, br)


def _moe_kernel(tmg, inv_ref, tg_ref, nv_ref, xg_hbm, wg_ref, wu_ref, wd_ref, ln2_ref, fw_ref,
                y_hbm, xbuf, ybuf, sem_in, sem_out):
    i = pl.program_id(0)
    n = pl.num_programs(0)
    slot = i % 2

    def row_in(tok, r, slt):
        return pltpu.make_async_copy(xg_hbm.at[pl.ds(tok, 1)], xbuf.at[slt, pl.ds(r, 1)], sem_in.at[slt])

    def row_out(tok, r, slt):
        return pltpu.make_async_copy(ybuf.at[slt, pl.ds(r, 1)], y_hbm.at[pl.ds(tok, 1)], sem_out.at[slt])

    unroll = 8

    def gather(tile, slt):
        def body(r, carry):
            row_in(inv_ref[tile * tmg + r], r, slt).start()
            return carry
        lax.fori_loop(0, tmg, body, 0, unroll=unroll)

    def wait_gather(slt):
        pltpu.make_async_copy(xg_hbm.at[pl.ds(0, tmg)], xbuf.at[slt], sem_in.at[slt]).wait()

    def scatter(tile, slt):
        def body(r, carry):
            row_out(inv_ref[tile * tmg + r], r, slt).start()
            return carry
        full = nv_ref[tile] == tmg

        @pl.when(full)
        def _():
            for r in range(tmg):
                body(r, 0)

        @pl.when(jnp.logical_not(full))
        def _():
            lax.fori_loop(0, nv_ref[tile], body, 0)

    def wait_scatter(tile, slt):
        def body(r, carry):
            row_out(0, r, slt).wait()
            return carry
        full = nv_ref[tile] == tmg

        @pl.when(full)
        def _():
            pltpu.make_async_copy(ybuf.at[slt], y_hbm.at[pl.ds(0, tmg)], sem_out.at[slt]).wait()

        @pl.when(jnp.logical_not(full))
        def _():
            lax.fori_loop(0, nv_ref[tile], body, 0)

    @pl.when(i == 0)
    def _():
        gather(0, 0)

    nxt = jnp.minimum(i + 1, n - 1)

    @pl.when(i >= 2)
    def _():
        wait_scatter(jnp.maximum(i - 2, 0), slot)

    @pl.when(nv_ref[i] == 0)
    def _():
        gather(nxt, 1 - slot)
        wait_gather(slot)

    @pl.when(nv_ref[i] > 0)
    def _():
        wait_gather(slot)
        x1 = xbuf[slot, :, 0:D_MODEL]
        for r in range(tmg // 2):
            row_in(inv_ref[nxt * tmg + r], r, 1 - slot).start()
        gate = xbuf[slot, :, D_MODEL:D_MODEL + LANES]
        for r in range(tmg // 2, tmg):
            row_in(inv_ref[nxt * tmg + r], r, 1 - slot).start()
        x1_res = xbuf[slot, :, 0:D_MODEL]
        ms = jnp.mean(x1 * x1, axis=-1, keepdims=True)
        h2 = (x1 * lax.rsqrt(ms + EPS) * ln2_ref[...]).astype(BF16)
        lane = lax.broadcasted_iota(jnp.int32, gate.shape, 1)
        first = N_EGROUPS + EXPERTS_PER_GROUP * tg_ref[i]
        pieces = []
        for e in range(EXPERTS_PER_GROUP):
            a = jnp.dot(h2, wg_ref[e], preferred_element_type=F32)
            up = jnp.dot(h2, wu_ref[e], preferred_element_type=F32)
            ge = jnp.sum(jnp.where(lane == first + e, gate, 0.0), axis=-1, keepdims=True)
            pieces.append((a * _sigmoid(a) * up * ge).astype(BF16))
        mo = jnp.dot(jnp.concatenate(pieces, axis=1), wd_ref[...], preferred_element_type=F32)
        x2 = x1_res + mo
        ms2 = jnp.mean(x2 * x2, axis=-1, keepdims=True)
        ybuf[slot] = x2 * lax.rsqrt(ms2 + EPS) * fw_ref[...]
        scatter(i, slot)

    @pl.when(i == n - 1)
    def _():
        wait_gather(1 - slot)

        @pl.when(i >= 1)
        def _():
            wait_scatter(jnp.maximum(i - 1, 0), 1 - slot)
        wait_scatter(i, slot)


def _moe(xg, inv, tile_group, n_valid, wg, wu, wd, ln2, fw, tmg):
    t = xg.shape[0]
    n_tiles = tile_group.shape[0]
    gcols = EXPERTS_PER_GROUP * D_EXPERT
    const = lambda i, inv, tg, nv: (0, 0)
    by_group = lambda i, inv, tg, nv: (tg[i], 0, 0)
    return pl.pallas_call(
        functools.partial(_moe_kernel, tmg),
        grid_spec=pltpu.PrefetchScalarGridSpec(
            num_scalar_prefetch=3,
            grid=(n_tiles,),
            in_specs=[
                pl.BlockSpec(memory_space=pltpu.HBM),
                pl.BlockSpec((EXPERTS_PER_GROUP, D_MODEL, D_EXPERT), by_group),
                pl.BlockSpec((EXPERTS_PER_GROUP, D_MODEL, D_EXPERT), by_group),
                pl.BlockSpec((None, gcols, D_MODEL), by_group),
                pl.BlockSpec((1, D_MODEL), const),
                pl.BlockSpec((1, D_MODEL), const),
            ],
            out_specs=pl.BlockSpec(memory_space=pltpu.HBM),
            scratch_shapes=[
                pltpu.VMEM((2, tmg, XG_COLS), F32),
                pltpu.VMEM((2, tmg, D_MODEL), F32),
                pltpu.SemaphoreType.DMA((2,)),
                pltpu.SemaphoreType.DMA((2,)),
            ],
        ),
        out_shape=jax.ShapeDtypeStruct((t, D_MODEL), F32),
        compiler_params=pltpu.CompilerParams(
            dimension_semantics=("arbitrary",), vmem_limit_bytes=VMEM_LIMIT),
        name="moe",
    )(inv, tile_group, n_valid, xg, wg, wu, wd, ln2, fw)


def _invperm_kernel(pos_ref, lo_ref, hi_ref, inv_ref):
    def clear(i, carry):
        inv_ref[i] = 0
        return carry
    for k in range(lo_ref.shape[0]):
        lax.fori_loop(lo_ref[k], hi_ref[k], clear, 0)

    def place(t, carry):
        inv_ref[pos_ref[t]] = t
        return carry
    lax.fori_loop(0, pos_ref.shape[0], place, 0, unroll=8)


def _invperm(pos, pad_lo, pad_hi, n):
    smem = pl.BlockSpec(memory_space=pltpu.SMEM)
    return pl.pallas_call(
        _invperm_kernel,
        in_specs=[smem, smem, smem],
        out_specs=smem,
        out_shape=jax.ShapeDtypeStruct((n,), jnp.int32),
        name="invperm",
    )(pos, pad_lo, pad_hi)


def _route(xg, tmg):
    t = xg.shape[0]
    g = xg[:, D_MODEL].astype(jnp.int32)
    rank = xg[:, D_MODEL + 1].astype(jnp.int32)
    counts = jnp.sum(g[:, None] == jnp.arange(N_EGROUPS, dtype=jnp.int32)[None, :], axis=0, dtype=jnp.int32)
    tiles_per = (counts + tmg - 1) // tmg
    tile_end = jnp.cumsum(tiles_per)
    tile_base = tile_end - tiles_per
    n_tiles = t // tmg + N_EGROUPS
    pos = tile_base[g] * tmg + rank
    pad_lo = jnp.concatenate([tile_base * tmg + counts, tile_end[-1:] * tmg]).astype(jnp.int32)
    pad_hi = jnp.concatenate([tile_end * tmg, jnp.full((1,), n_tiles * tmg, jnp.int32)]).astype(jnp.int32)
    inv = _invperm(pos, pad_lo, pad_hi, n_tiles * tmg)
    ti = jnp.arange(n_tiles, dtype=jnp.int32)
    tg = jnp.minimum(jnp.sum(ti[:, None] >= tile_end[None, :], axis=1, dtype=jnp.int32), N_EGROUPS - 1)
    nv = jnp.clip(counts[tg] - (ti - tile_base[tg]) * tmg, 0, tmg)
    nv = jnp.where(ti < tile_end[-1], nv, 0).astype(jnp.int32)
    return inv, tg, nv


def _pad_lanes(v, n=LANES):
    v = v.reshape(1, -1).astype(F32)
    return jnp.pad(v, ((0, 0), (0, n - v.shape[1])))


def kernel(x_prompt, x_sample, state_ssm, state_ssd_conv, state_short_conv, meta_tokens, ln1_w, w_in,
           ssd_conv_w, ssd_conv_b, ssd_dt_bias, ssd_A_log, ssd_D, ssd_norm_w, w_ssd_out, sc_conv_w,
           w_sc_out, w_o, ln2_w, w_rg, b_rg, w_re, b_re, w_gate, w_up, w_down, final_norm_w):
    assert ln1_w.shape[0] == 1, "single-layer trunk"
    bp, sp, _ = x_prompt.shape
    bs, ss, _ = x_sample.shape
    l = 0

    wt = jnp.swapaxes(w_in[l], 0, 1)
    wdt = jnp.pad(wt[A_COLS:A_COLS + SSD_HEADS], ((0, LANES - SSD_HEADS), (0, 0))).astype(BF16)
    lnw = ln1_w[l].reshape(1, D_MODEL)
    prm = (ssd_conv_w[l], ssd_conv_b[l].reshape(1, -1), sc_conv_w[l],
           _pad_lanes(ssd_dt_bias[l]), _pad_lanes(ssd_A_log[l]),
           jnp.repeat(ssd_D[l].astype(F32), SSD_HEAD_DIM).reshape(1, SSD_INNER),
           ssd_norm_w[l].reshape(1, SSD_INNER))
    wa = w_ssd_out[l].astype(BF16)
    ws = w_sc_out[l].astype(BF16)
    wo = w_o[l].astype(BF16)
    ln2 = ln2_w[l].reshape(1, D_MODEL)
    wr = jnp.pad(jnp.concatenate([w_rg[l], w_re[l]], axis=1),
                 ((0, 0), (0, LANES - N_EGROUPS - N_EXPERTS))).astype(BF16)
    br = _pad_lanes(jnp.concatenate([b_rg[l], b_re[l]]))
    gcols = EXPERTS_PER_GROUP * D_EXPERT
    fw = final_norm_w.reshape(1, D_MODEL)
    halves = 2 * N_EXPERTS
    cast32 = (w_gate[l].reshape(halves, D_MODEL // 2, D_EXPERT),
              w_up[l].reshape(halves, D_MODEL // 2, D_EXPERT),
              w_down[l].reshape(halves, D_EXPERT // 2, D_MODEL))

    def mlp_tail(u, v, g, x, tmg):
        xg = _merge(u, v, g, x, wa, ws, wo, ln2, wr, br, tm=1024)
        inv, tg, nv = _route(xg, tmg)
        return _moe(xg, inv, tg, nv, wg, wu, wd, ln2, fw, tmg)

    q = LANES
    xm = jnp.pad(meta_tokens.astype(F32), ((q - N_META, 0), (0, 0)))
    xs = x_sample.reshape(bs * ss, D_MODEL)
    (sproj, sdt), (mproj, mdt) = _inproj(xs, lnw, wt, wdt, tm=bs * ss, extra=xm)
    zeros_hist = jnp.zeros((1, SSD_CONV - 1, SSD_CONV_DIM), F32)
    zeros_sch = jnp.zeros((1, SC_CONV - 1, SC_WIDTH), F32)
    zeros_st = jnp.zeros((1, SSD_INNER, SSD_STATE), F32)
    _, _, m_st, m_cs, m_scs = _seqmix(mproj, mdt, zeros_hist, zeros_sch, zeros_st, prm,
                                      nb=1, nc=1, q=q, ns=1, pad_rows=q - N_META, init_bcast=True)

    xp = x_prompt.reshape(bp * sp, D_MODEL)
    (pproj, pdt), (wg, wu, wd) = _inproj(xp, lnw, wt, wdt, tm=2048, cast=cast32)
    wg = wg.reshape(N_EXPERTS, D_MODEL, D_EXPERT)
    wu = wu.reshape(N_EXPERTS, D_MODEL, D_EXPERT)
    wd = wd.reshape(N_EGROUPS, gcols, D_MODEL)
    pu, pv, p_st, p_cs, p_scs = _seqmix(pproj, pdt, m_cs, m_scs, m_st, prm, nb=bp, nc=sp // q, q=q,
                                        ns=1, pad_rows=0, init_bcast=True, cps=4)
    y_prompt = mlp_tail(pu, pv, pproj, xp, MOE_TILE).reshape(bp, sp, D_MODEL)

    ns = 8
    su, sv, s_st, s_cs, s_scs = _seqmix(sproj, sdt, state_ssd_conv[l], state_short_conv[l],
                                        state_ssm[l].reshape(bs, SSD_INNER, SSD_STATE), prm,
                                        nb=bs // ns, nc=1, q=ns * ss, ns=ns, pad_rows=0, init_bcast=False)
    y_sample = mlp_tail(su, sv, sproj, xs, MOE_TILE_SMALL).reshape(bs, ss, D_MODEL)

    hshape = (SSD_HEADS, SSD_HEAD_DIM, SSD_STATE)
    return (y_prompt, y_sample,
            p_st.reshape(1, bp, *hshape), p_cs[None], p_scs[None],
            s_st.reshape(1, bs, *hshape), s_cs[None], s_scs[None])
```

```python
import functools

import jax
import jax.numpy as jnp
import numpy as np
from jax import lax
from jax.experimental import pallas as pl
from jax.experimental.pallas import tpu as pltpu

F32 = jnp.float32
BF16 = jnp.bfloat16

D_MODEL = 1024
N_META = 16
SSD_INNER = 2048
SSD_HEAD_DIM = 64
SSD_HEADS = 32
SSD_GROUPS = 4
SSD_HPG = 8
SSD_STATE = 128
SSD_CONV = 4
SSD_CONV_DIM = 3072
SC_WIDTH = 1024
SC_CONV = 3
N_EGROUPS = 4
EXPERTS_PER_GROUP = 8
N_EXPERTS = 32
D_EXPERT = 256
EPS = 1e-6

LANES = 128
SUBLANES = 8
GROUP_COLS = SSD_HPG * SSD_HEAD_DIM
A_COLS = SSD_INNER + SSD_CONV_DIM
S_COLS = 3 * SC_WIDTH
G_COLS = 2 * D_MODEL
PROJ_COLS = A_COLS + S_COLS + G_COLS
XG_COLS = D_MODEL + LANES
MOE_TILE = 256
MOE_TILE_SMALL = 128
RANK_BLOCK = 512
NEG_BIG = -1e30
LOG2E = 1.4426950408889634
VMEM_LIMIT = 56 * 1024 * 1024


def _nt_dot(a, b):
    return lax.dot_general(a, b, (((1,), (1,)), ((), ())), preferred_element_type=F32)


def _dot01(m01_bf16, x):
    hi = x.astype(BF16)
    r1 = x - hi.astype(F32)
    mid = r1.astype(BF16)
    lo = (r1 - mid.astype(F32)).astype(BF16)
    out = jnp.dot(m01_bf16, hi, preferred_element_type=F32)
    out = out + jnp.dot(m01_bf16, mid, preferred_element_type=F32)
    return out + jnp.dot(m01_bf16, lo, preferred_element_type=F32)


def _softplus(x):
    return jnp.maximum(x, 0.0) + jnp.log1p(jnp.exp(-jnp.abs(x)))


def _sigmoid(x):
    return 1.0 / (1.0 + jnp.exp2(x * (-LOG2E)))


def _inproj_kernel(n_sets, n_cast, cast_blocks, *refs):
    x_refs = refs[:n_sets]
    lnw_ref, wt_ref, wdt_ref = refs[n_sets:n_sets + 3]
    cast_in = refs[n_sets + 3:n_sets + 3 + n_cast]
    outs = refs[n_sets + 3 + n_cast:n_sets + 3 + n_cast + 2 * n_sets]
    cast_out = refs[n_sets + 3 + n_cast + 2 * n_sets:n_sets + 3 + 2 * n_cast + 2 * n_sets]
    h_refs = refs[n_sets + 3 + 2 * n_cast + 2 * n_sets:]

    if n_cast:
        step = pl.program_id(0) * pl.num_programs(1) + pl.program_id(1)

        @pl.when(step < cast_blocks)
        def _():
            for src, dst in zip(cast_in, cast_out):
                dst[...] = src[...].astype(BF16)

    @pl.when(pl.program_id(1) == 0)
    def _():
        for k in range(n_sets):
            x = x_refs[k][...]
            ms = jnp.mean(x * x, axis=-1, keepdims=True)
            h = (x * lax.rsqrt(ms + EPS) * lnw_ref[...]).astype(BF16)
            h_refs[k][...] = h
            outs[2 * k + 1][...] = _nt_dot(h, wdt_ref[...])

    w = wt_ref[...].astype(BF16)
    for k in range(n_sets):
        outs[2 * k][...] = _nt_dot(h_refs[k][...], w).astype(BF16)


def _inproj(x, lnw, wt, wdt, tm, tn=1024, extra=None, cast=(), cast_dst=()):
    t = x.shape[0]
    n_a = A_COLS // tn
    nj = PROJ_COLS // tn
    cast_blocks = cast[0].shape[0] if cast else 0
    assert all(c.shape[0] == cast_blocks for c in cast) and cast_blocks <= (t // tm) * nj

    def cast_block(i, j):
        return jnp.minimum(i * nj + j, cast_blocks - 1)

    cast_specs = [pl.BlockSpec((1,) + c.shape[1:], lambda i, j: (cast_block(i, j), 0, 0)) for c in cast]
    cast_out_specs, cast_out_shapes = [], []
    for k, c in enumerate(cast):
        dst = cast_dst[k] if k < len(cast_dst) else None
        if dst is None:
            cast_out_specs.append(cast_specs[k])
            cast_out_shapes.append(jax.ShapeDtypeStruct(c.shape, BF16))
        else:
            shape, place = dst
            cast_out_specs.append(pl.BlockSpec((1,) + c.shape[1:],
                                               lambda i, j, place=place: place(cast_block(i, j))))
            cast_out_shapes.append(jax.ShapeDtypeStruct(shape, BF16))

    def w_rows(i, j):
        return (pl.multiple_of(jnp.where(j < n_a, j * tn, j * tn + SSD_HEADS), SSD_HEADS), 0)

    sets = [(x, tm, lambda i: i)]
    if extra is not None:
        assert t == tm, "extra rows are recomputed per row tile"
        sets.append((extra, extra.shape[0], lambda i: 0))
    x_specs, out_specs, out_shape, scratch = [], [], [], []
    for xs_, rows, ri in sets:
        x_specs.append(pl.BlockSpec((rows, D_MODEL), lambda i, j, ri=ri: (ri(i), 0)))
        out_specs += [pl.BlockSpec((rows, tn), lambda i, j, ri=ri: (ri(i), j)),
                      pl.BlockSpec((rows, LANES), lambda i, j, ri=ri: (ri(i), 0))]
        out_shape += [jax.ShapeDtypeStruct((xs_.shape[0], PROJ_COLS), BF16),
                      jax.ShapeDtypeStruct((xs_.shape[0], LANES), F32)]
        scratch.append(pltpu.VMEM((rows, D_MODEL), BF16))
    res = pl.pallas_call(
        functools.partial(_inproj_kernel, len(sets), len(cast), cast_blocks),
        grid=(t // tm, nj),
        in_specs=x_specs + [
            pl.BlockSpec((1, D_MODEL), lambda i, j: (0, 0)),
            pl.BlockSpec((pl.Element(tn), pl.Element(D_MODEL)), w_rows),
            pl.BlockSpec((LANES, D_MODEL), lambda i, j: (0, 0)),
        ] + cast_specs,
        out_specs=out_specs + cast_out_specs,
        out_shape=out_shape + cast_out_shapes,
        scratch_shapes=scratch,
        compiler_params=pltpu.CompilerParams(
            dimension_semantics=("arbitrary", "arbitrary"), vmem_limit_bytes=VMEM_LIMIT),
        name="inproj",
    )(*[st[0] for st in sets], lnw, wt, wdt, *cast)
    n_proj = 2 * len(sets)
    projs = tuple(tuple(res[2 * k:2 * k + 2]) for k in range(len(sets)))
    projs = projs if extra is not None else projs[0]
    return (projs, tuple(res[n_proj:])) if cast else projs


def halo_rows(ns):
    return max(2 * SUBLANES, ns * SUBLANES)


def _shift_matrices(q, ns, taps):
    L = q // ns
    hb = halo_rows(ns)
    r = np.arange(q)[:, None]
    col = np.arange(3 * hb + q)[None, :]
    seq, t = r // L, r % L
    mats = []
    for k in range(taps - 1):
        d = taps - 1 - k
        from_cur = (col >= 3 * hb) & (col - 3 * hb == r - d) & (t >= d)
        from_hist = (col < 3 * hb) & (col % hb == seq * SUBLANES + SUBLANES - d + t) & (t < d)
        mats.append(from_cur | from_hist)
    return jnp.asarray(np.stack(mats), dtype=BF16)


def _seqmix_kernel(Q, NS, CPS, one_step, pad_rows,
                   a_blk, sb_blk, sc_blk, sh_blk, dt_blk, hist_ref, schist_ref, st0_ref, shift_ref,
                   cw_ref, cb_ref, scw_ref, dtb_ref, alog_ref, drow_ref, nw_ref,
                   u_blk, v_blk, st_ref, cs_ref, scs_ref,
                   halo, scpad, xbc_s, y_s, xst_s):
    @pl.when(pl.program_id(1) == 0)
    def _():
        halo[...] = jnp.zeros_like(halo)
        for s in range(NS):
            halo[(s + 1) * SUBLANES - (SSD_CONV - 1):(s + 1) * SUBLANES, :] = hist_ref[s]
            scpad[s, SUBLANES - (SC_CONV - 1):SUBLANES, :] = schist_ref[s]
        if not one_step:
            st_ref[...] = st0_ref[...]

    src_ref = st0_ref if one_step else st_ref
    shared = (shift_ref, cw_ref, cb_ref, scw_ref, dtb_ref, alog_ref, drow_ref, nw_ref,
              src_ref, st_ref, cs_ref, scs_ref, halo, scpad, xbc_s, y_s, xst_s)
    row_blocks = (a_blk, sb_blk, sc_blk, sh_blk, dt_blk, u_blk, v_blk)
    if CPS == 1:
        _seqmix_chunk(Q, NS, pad_rows, *row_blocks, *shared)
    else:
        def body(k, carry):
            rows = pl.ds(pl.multiple_of(k * Q, Q), Q)
            _seqmix_chunk(Q, NS, pad_rows, *[r.at[rows] for r in row_blocks], *shared)
            return carry
        lax.fori_loop(0, CPS, body, 0)


def _seqmix_chunk(Q, NS, pad_rows, a_ref, sb_ref, sc_ref, sh_ref, dt_ref, u_ref, v_ref,
                  shift_ref, cw_ref, cb_ref, scw_ref, dtb_ref, alog_ref, drow_ref, nw_ref,
                  src_ref, st_ref, cs_ref, scs_ref, halo, scpad, xbc_s, y_s, xst_s):
    L = Q // NS
    lg = L.bit_length() - 1

    CW = 256
    hb = halo_rows(NS)
    for cc in range(0, SSD_CONV_DIM, CW):
        raw_b = a_ref[:, SSD_INNER + cc:SSD_INNER + cc + CW]
        hl = halo[:, cc:cc + CW]
        if hb > hl.shape[0]:
            hl = jnp.concatenate([hl, jnp.zeros((hb - hl.shape[0], CW), F32)], axis=0)
        h_hi = hl.astype(BF16)
        h_r = hl - h_hi.astype(F32)
        h_mid = h_r.astype(BF16)
        h_lo = (h_r - h_mid.astype(F32)).astype(BF16)
        ext = jnp.concatenate([h_hi, h_mid, h_lo, raw_b], axis=0)
        raw = raw_b.astype(F32)
        acc = raw * cw_ref[3:4, cc:cc + CW] + cb_ref[:, cc:cc + CW]
        for k in range(SSD_CONV - 1):
            acc = acc + jnp.dot(shift_ref[k], ext, preferred_element_type=F32) * cw_ref[k:k + 1, cc:cc + CW]
        xbc_s[:, cc:cc + CW] = acc * _sigmoid(acc)
        for s in range(NS):
            halo[s * SUBLANES:(s + 1) * SUBLANES, cc:cc + CW] = raw[(s + 1) * L - SUBLANES:(s + 1) * L]
    for s in range(NS):
        cs_ref[s] = halo[(s + 1) * SUBLANES - (SSD_CONV - 1):(s + 1) * SUBLANES, :]

    for cc in range(0, SC_WIDTH, CW):
        scb = sb_ref[:, cc:cc + CW].astype(F32)
        ch_all = sc_ref[:, cc:cc + CW].astype(F32) * sh_ref[:, cc:cc + CW].astype(F32)
        for s in range(NS):
            ch = ch_all[s * L:(s + 1) * L]
            scpad[s, SUBLANES:SUBLANES + L, cc:cc + CW] = ch
            acc = ch * scw_ref[2:3, cc:cc + CW]
            for k in range(SC_CONV - 1):
                acc = acc + scpad[s, 6 + k:6 + k + L, cc:cc + CW] * scw_ref[k:k + 1, cc:cc + CW]
            v_ref[s * L:(s + 1) * L, cc:cc + CW] = (scb[s * L:(s + 1) * L] * acc).astype(BF16)
    for s in range(NS):
        scs_ref[s] = scpad[s, L + 6:L + 8, :]
        scpad[s, 0:SUBLANES, :] = scpad[s, L:L + SUBLANES, :]

    def padrows(x):
        if Q == LANES:
            return x
        return jnp.concatenate([x, jnp.zeros((LANES - Q, x.shape[1]), x.dtype)], axis=0)

    li = lax.broadcasted_iota(jnp.int32, (Q, Q), 0)
    si = lax.broadcasted_iota(jnp.int32, (Q, Q), 1)
    same = (li >> lg) == (si >> lg)
    causal = jnp.logical_and(same, si <= li)
    tril01 = jnp.where(causal, 1.0, 0.0).astype(BF16)
    same01 = jnp.where(same, 1.0, 0.0).astype(BF16)

    dt = _softplus(dt_ref[...] + dtb_ref[...])
    if pad_rows:
        ri = lax.broadcasted_iota(jnp.int32, (Q, LANES), 0)
        dt = jnp.where(ri >= pad_rows, dt, 0.0)
    da = dt * (-jnp.exp(alog_ref[...]))
    acum = _dot01(tril01, da)
    tot = _dot01(same01, da)
    acum_t = padrows(acum).T
    tot_t = padrows(tot).T
    dt_t = padrows(dt).T
    w_t = jnp.exp(tot_t - acum_t) * dt_t
    acum2 = acum * LOG2E
    rowq_t = acum_t * LOG2E - jnp.log2(dt_t)

    left_head = lax.broadcasted_iota(jnp.int32, (Q, LANES), 1) < SSD_HEAD_DIM
    rowseq = lax.broadcasted_iota(jnp.int32, (Q, GROUP_COLS), 0) >> lg
    rowseq_p = lax.broadcasted_iota(jnp.int32, (LANES, LANES), 0) >> lg

    for g in range(SSD_GROUPS):
        b_g = xbc_s[:, SSD_INNER + g * SSD_STATE:SSD_INNER + (g + 1) * SSD_STATE]
        c_g = xbc_s[:, SSD_INNER + GROUP_COLS + g * SSD_STATE:SSD_INNER + GROUP_COLS + (g + 1) * SSD_STATE]
        b_gb = b_g.astype(BF16)
        c_gb = c_g.astype(BF16)
        cbm = _nt_dot(c_gb, b_gb)
        yo = None
        for s in range(NS):
            h_s = src_ref[s, g * GROUP_COLS:(g + 1) * GROUP_COLS, :].astype(BF16)
            yo_s = _nt_dot(c_gb, h_s)
            yo = yo_s if yo is None else jnp.where(rowseq == s, yo_s, yo)
        for rp in range(SSD_HPG // 2):
            h0 = g * SSD_HPG + 2 * rp
            cols = slice(h0 * SSD_HEAD_DIM, (h0 + 2) * SSD_HEAD_DIM)
            colbs, w_pair = [], []
            for h in (h0, h0 + 1):
                colb = jnp.broadcast_to(acum2[:, h:h + 1], (Q, LANES))
                rowb = jnp.broadcast_to(rowq_t[h:h + 1, 0:Q], (Q, Q))
                dec_dt = jnp.exp2(jnp.where(causal, colb[:, 0:Q] - rowb, NEG_BIG))
                w_pair.append((cbm * dec_dt).astype(BF16))
                colbs.append(colb)
            x_p = xbc_s[:, cols]
            rhs = jnp.concatenate([jnp.where(left_head, x_p, 0.0).astype(BF16),
                                   jnp.where(left_head, 0.0, x_p).astype(BF16)], axis=0)
            yd = jnp.dot(jnp.concatenate(w_pair, axis=1), rhs, preferred_element_type=F32)
            ecol = jnp.exp2(jnp.where(left_head, colbs[0], colbs[1]))
            y_s[:, cols] = yd + yo[:, rp * LANES:(rp + 1) * LANES] * ecol + drow_ref[:, cols] * x_p

    for jb in range(SSD_INNER // LANES):
        xst_s[jb * LANES:(jb + 1) * LANES, :] = padrows(xbc_s[:, jb * LANES:(jb + 1) * LANES]).T
    for s in range(NS):
        da_b = jnp.exp(jnp.broadcast_to(tot_t[:, s * L:s * L + 1], (LANES, LANES)))
        for g in range(SSD_GROUPS):
            b_p = padrows(xbc_s[:, SSD_INNER + g * SSD_STATE:SSD_INNER + (g + 1) * SSD_STATE])
            if NS > 1:
                b_p = jnp.where(rowseq_p == s, b_p, 0.0)
            pieces = []
            for r in range(SSD_HPG):
                h = g * SSD_HPG + r
                pieces.append(xst_s[h * SSD_HEAD_DIM:(h + 1) * SSD_HEAD_DIM, :] * w_t[h:h + 1, :])
            xw_t = jnp.concatenate(pieces, axis=0).astype(BF16)
            upd = jnp.dot(xw_t, b_p.astype(BF16), preferred_element_type=F32)
            for r in range(SSD_HPG):
                h = g * SSD_HPG + r
                rows = slice(h * SSD_HEAD_DIM, (h + 1) * SSD_HEAD_DIM)
                dec_h = jnp.broadcast_to(da_b[h:h + 1, :], (SSD_HEAD_DIM, SSD_STATE))
                st_ref[s, rows, :] = dec_h * src_ref[s, rows, :] + upd[r * SSD_HEAD_DIM:(r + 1) * SSD_HEAD_DIM, :]

    for g in range(SSD_GROUPS):
        cols = slice(g * GROUP_COLS, (g + 1) * GROUP_COLS)
        z = a_ref[:, cols].astype(F32)
        ug = y_s[:, cols] * (z * _sigmoid(z))
        ms = jnp.mean(ug * ug, axis=-1, keepdims=True)
        u_ref[:, cols] = (ug * lax.rsqrt(ms + EPS) * nw_ref[:, cols]).astype(BF16)


def _seqmix(proj, pdt, hist, schist, st0, prm, *, nb, nc, q, ns, pad_rows, init_bcast, cps=1):
    L = q // ns
    cw, cb, scw, dtb, alog, drow, nw = prm
    steps = nc // cps
    rows = q * cps
    sc_block0 = A_COLS // SC_WIDTH

    def tok(b, c):
        return (b * steps + c, 0)

    def init3(b, c):
        return (0 if init_bcast else b, 0, 0)

    def const2(b, c):
        return (0, 0)

    def per_b(b, c):
        return (b, 0, 0)

    n_seq = nb * ns
    shifts = _shift_matrices(q, ns, SSD_CONV)
    sc_specs = [pl.BlockSpec((rows, SC_WIDTH), lambda b, c, k=k: (b * steps + c, sc_block0 + k))
                for k in range(3)]
    return pl.pallas_call(
        functools.partial(_seqmix_kernel, q, ns, cps, steps == 1 and cps == 1, pad_rows),
        grid=(nb, steps),
        in_specs=[pl.BlockSpec((rows, A_COLS), tok)] + sc_specs + [
            pl.BlockSpec((rows, LANES), tok),
            pl.BlockSpec((ns, SSD_CONV - 1, SSD_CONV_DIM), init3),
            pl.BlockSpec((ns, SC_CONV - 1, SC_WIDTH), init3),
            pl.BlockSpec((ns, SSD_INNER, SSD_STATE), init3),
            pl.BlockSpec(shifts.shape, lambda b, c: (0, 0, 0)),
            pl.BlockSpec((SSD_CONV, SSD_CONV_DIM), const2),
            pl.BlockSpec((1, SSD_CONV_DIM), const2),
            pl.BlockSpec((SC_CONV, SC_WIDTH), const2),
            pl.BlockSpec((1, LANES), const2),
            pl.BlockSpec((1, LANES), const2),
            pl.BlockSpec((1, SSD_INNER), const2),
            pl.BlockSpec((1, SSD_INNER), const2),
        ],
        out_specs=[
            pl.BlockSpec((rows, SSD_INNER), tok),
            pl.BlockSpec((rows, SC_WIDTH), tok),
            pl.BlockSpec((ns, SSD_INNER, SSD_STATE), per_b),
            pl.BlockSpec((ns, SSD_CONV - 1, SSD_CONV_DIM), per_b),
            pl.BlockSpec((ns, SC_CONV - 1, SC_WIDTH), per_b),
        ],
        out_shape=[
            jax.ShapeDtypeStruct((nb * nc * q, SSD_INNER), BF16),
            jax.ShapeDtypeStruct((nb * nc * q, SC_WIDTH), BF16),
            jax.ShapeDtypeStruct((n_seq, SSD_INNER, SSD_STATE), F32),
            jax.ShapeDtypeStruct((n_seq, SSD_CONV - 1, SSD_CONV_DIM), F32),
            jax.ShapeDtypeStruct((n_seq, SC_CONV - 1, SC_WIDTH), F32),
        ],
        scratch_shapes=[
            pltpu.VMEM((ns * SUBLANES, SSD_CONV_DIM), F32),
            pltpu.VMEM((ns, SUBLANES + L, SC_WIDTH), F32),
            pltpu.VMEM((q, SSD_CONV_DIM), F32),
            pltpu.VMEM((q, SSD_INNER), F32),
            pltpu.VMEM((SSD_INNER, LANES), F32),
        ],
        compiler_params=pltpu.CompilerParams(
            dimension_semantics=("arbitrary", "arbitrary"), vmem_limit_bytes=VMEM_LIMIT),
        name="seqmix",
    )(proj, proj, proj, proj, pdt, hist, schist, st0, shifts, cw, cb, scw, dtb, alog, drow, nw)


def _merge_kernel(u_ref, v_ref, g_ref, x_ref, wa_ref, ws_ref, wo_ref, ln2_ref, wr_ref, br_ref,
                  xg_ref, cnt_s):
    i = pl.program_id(0)

    @pl.when(i == 0)
    def _():
        cnt_s[...] = jnp.zeros_like(cnt_s)

    y_ssd = jnp.dot(u_ref[...], wa_ref[...], preferred_element_type=F32)
    y_sc = jnp.dot(v_ref[...], ws_ref[...], preferred_element_type=F32)
    g1 = g_ref[:, 0:D_MODEL].astype(F32)
    g2 = g_ref[:, D_MODEL:2 * D_MODEL].astype(F32)
    merged = (_sigmoid(g1) * y_ssd + _sigmoid(g2) * y_sc).astype(BF16)
    x1 = x_ref[...] + jnp.dot(merged, wo_ref[...], preferred_element_type=F32)
    xg_ref[:, 0:D_MODEL] = x1
    ms = jnp.mean(x1 * x1, axis=-1, keepdims=True)
    h2 = (x1 * lax.rsqrt(ms + EPS) * ln2_ref[...]).astype(BF16)

    logits = jnp.dot(h2, wr_ref[...], preferred_element_type=F32) + br_ref[...]
    lane = lax.broadcasted_iota(jnp.int32, logits.shape, 1).astype(F32)
    big = float(LANES)
    gl = jnp.where(lane < N_EGROUPS, logits, NEG_BIG)
    gmax = jnp.max(gl, axis=-1, keepdims=True)
    g_sel = jnp.min(jnp.where(gl == gmax, lane, big), axis=-1, keepdims=True)
    gsum = jnp.sum(jnp.exp(gl - gmax), axis=-1, keepdims=True)
    g_prob = 1.0 / gsum
    lo = N_EGROUPS + EXPERTS_PER_GROUP * g_sel
    emask = jnp.logical_and(lane >= lo, lane < lo + EXPERTS_PER_GROUP)
    el = jnp.where(emask, logits, NEG_BIG)
    m1 = jnp.max(el, axis=-1, keepdims=True)
    e = jnp.where(emask, jnp.exp(el - m1), -1.0)
    i1 = jnp.min(jnp.where(e == 1.0, lane, big), axis=-1, keepdims=True)
    e_rest = jnp.where(lane == i1, -1.0, e)
    e2 = jnp.max(e_rest, axis=-1, keepdims=True)
    i2 = jnp.min(jnp.where(e_rest == e2, lane, big), axis=-1, keepdims=True)
    denom = 1.0 + e2
    w1 = g_prob / denom
    w2 = g_prob * e2 / denom
    gates = jnp.where(lane == i1, w1, 0.0) + jnp.where(lane == i2, w2, 0.0)

    tm = logits.shape[0]
    rb = min(tm, RANK_BLOCK)
    onehot = jnp.where(lane == g_sel, 1.0, 0.0).astype(BF16)
    ri = lax.broadcasted_iota(jnp.int32, (rb, rb), 0)
    ci = lax.broadcasted_iota(jnp.int32, (rb, rb), 1)
    tril01 = jnp.where(ci <= ri, 1.0, 0.0).astype(BF16)
    carry = cnt_s[...]
    incl_blocks = []
    for r0 in range(0, tm, rb):
        incl_b = jnp.dot(tril01, onehot[r0:r0 + rb], preferred_element_type=F32) + carry
        carry = incl_b[rb - 1:rb, :]
        incl_blocks.append(incl_b)
    cnt_s[...] = carry
    incl = incl_blocks[0] if len(incl_blocks) == 1 else jnp.concatenate(incl_blocks, axis=0)
    rank = jnp.sum(jnp.where(lane == g_sel, incl - 1.0, 0.0), axis=-1, keepdims=True)
    xg_ref[:, D_MODEL:D_MODEL + LANES] = jnp.where(
        lane == 0, g_sel.astype(F32), jnp.where(lane == 1, rank, gates))


def _merge(u, v, g, x, wa, ws, wo, ln2, wr, br, tm):
    t = x.shape[0]
    row = lambda i: (i, 0)
    const = lambda i: (0, 0)
    return pl.pallas_call(
        _merge_kernel,
        grid=(t // tm,),
        in_specs=[
            pl.BlockSpec((tm, SSD_INNER), row),
            pl.BlockSpec((tm, SC_WIDTH), row),
            pl.BlockSpec((tm, G_COLS), lambda i: (i, (A_COLS + S_COLS) // G_COLS)),
            pl.BlockSpec((tm, D_MODEL), row),
            pl.BlockSpec((SSD_INNER, D_MODEL), const, pipeline_mode=pl.Buffered(1)),
            pl.BlockSpec((SC_WIDTH, D_MODEL), const, pipeline_mode=pl.Buffered(1)),
            pl.BlockSpec((D_MODEL, D_MODEL), const, pipeline_mode=pl.Buffered(1)),
            pl.BlockSpec((1, D_MODEL), const),
            pl.BlockSpec((D_MODEL, LANES), const),
            pl.BlockSpec((1, LANES), const),
        ],
        out_specs=pl.BlockSpec((tm, XG_COLS), row),
        out_shape=jax.ShapeDtypeStruct((t, XG_COLS), F32),
        scratch_shapes=[pltpu.VMEM((1, LANES), F32)],
        compiler_params=pltpu.CompilerParams(
            dimension_semantics=("arbitrary",), vmem_limit_bytes=VMEM_LIMIT),
        name="merge",
    )(u, v, g, x, wa, ws, wo, ln2, wr, br)


def _moe_kernel(tmg, inv_ref, tg_ref, nv_ref, xg_hbm, wg_ref, wu_ref, wd_ref, ln2_ref, fw_ref,
                y_hbm, xbuf, ybuf, sem_in, sem_out):
    i = pl.program_id(0)
    n = pl.num_programs(0)
    slot = i % 2

    def row_in(tok, r, slt):
        return pltpu.make_async_copy(xg_hbm.at[pl.ds(tok, 1)], xbuf.at[slt, pl.ds(r, 1)], sem_in.at[slt])

    def row_out(tok, r, slt):
        return pltpu.make_async_copy(ybuf.at[slt, pl.ds(r, 1)], y_hbm.at[pl.ds(tok, 1)], sem_out.at[slt])

    unroll = 8

    def gather(tile, slt):
        def body(r, carry):
            row_in(inv_ref[tile * tmg + r], r, slt).start()
            return carry
        lax.fori_loop(0, tmg, body, 0, unroll=unroll)

    def wait_gather(slt):
        pltpu.make_async_copy(xg_hbm.at[pl.ds(0, tmg)], xbuf.at[slt], sem_in.at[slt]).wait()

    def scatter(tile, slt):
        def body(r, carry):
            row_out(inv_ref[tile * tmg + r], r, slt).start()
            return carry
        full = nv_ref[tile] == tmg

        @pl.when(full)
        def _():
            for r in range(tmg):
                body(r, 0)

        @pl.when(jnp.logical_not(full))
        def _():
            lax.fori_loop(0, nv_ref[tile], body, 0)

    def wait_scatter(tile, slt):
        def body(r, carry):
            row_out(0, r, slt).wait()
            return carry
        full = nv_ref[tile] == tmg

        @pl.when(full)
        def _():
            pltpu.make_async_copy(ybuf.at[slt], y_hbm.at[pl.ds(0, tmg)], sem_out.at[slt]).wait()

        @pl.when(jnp.logical_not(full))
        def _():
            lax.fori_loop(0, nv_ref[tile], body, 0)

    @pl.when(i == 0)
    def _():
        gather(0, 0)

    nxt = jnp.minimum(i + 1, n - 1)

    @pl.when(i >= 2)
    def _():
        wait_scatter(jnp.maximum(i - 2, 0), slot)

    @pl.when(nv_ref[i] == 0)
    def _():
        gather(nxt, 1 - slot)
        wait_gather(slot)

    @pl.when(nv_ref[i] > 0)
    def _():
        wait_gather(slot)
        x1 = xbuf[slot, :, 0:D_MODEL]
        for r in range(tmg // 2):
            row_in(inv_ref[nxt * tmg + r], r, 1 - slot).start()
        gate = xbuf[slot, :, D_MODEL:D_MODEL + LANES]
        for r in range(tmg // 2, tmg):
            row_in(inv_ref[nxt * tmg + r], r, 1 - slot).start()
        x1_res = xbuf[slot, :, 0:D_MODEL]
        ms = jnp.mean(x1 * x1, axis=-1, keepdims=True)
        h2 = (x1 * lax.rsqrt(ms + EPS) * ln2_ref[...]).astype(BF16)
        lane = lax.broadcasted_iota(jnp.int32, gate.shape, 1)
        first = N_EGROUPS + EXPERTS_PER_GROUP * tg_ref[i]
        a = jnp.dot(h2, wg_ref[...], preferred_element_type=F32)
        up = jnp.dot(h2, wu_ref[...], preferred_element_type=F32)
        act = a * _sigmoid(a) * up
        pieces = []
        for e in range(EXPERTS_PER_GROUP):
            ge = jnp.sum(jnp.where(lane == first + e, gate, 0.0), axis=-1, keepdims=True)
            pieces.append((act[:, e * D_EXPERT:(e + 1) * D_EXPERT] * ge).astype(BF16))
        mo = jnp.dot(jnp.concatenate(pieces, axis=1), wd_ref[...], preferred_element_type=F32)
        x2 = x1_res + mo
        ms2 = jnp.mean(x2 * x2, axis=-1, keepdims=True)
        ybuf[slot] = x2 * lax.rsqrt(ms2 + EPS) * fw_ref[...]
        scatter(i, slot)

    @pl.when(i == n - 1)
    def _():
        wait_gather(1 - slot)

        @pl.when(i >= 1)
        def _():
            wait_scatter(jnp.maximum(i - 1, 0), 1 - slot)
        wait_scatter(i, slot)


def _moe(xg, inv, tile_group, n_valid, wg, wu, wd, ln2, fw, tmg):
    t = xg.shape[0]
    n_tiles = tile_group.shape[0]
    gcols = EXPERTS_PER_GROUP * D_EXPERT
    const = lambda i, inv, tg, nv: (0, 0)
    by_group = lambda i, inv, tg, nv: (tg[i], 0, 0)
    return pl.pallas_call(
        functools.partial(_moe_kernel, tmg),
        grid_spec=pltpu.PrefetchScalarGridSpec(
            num_scalar_prefetch=3,
            grid=(n_tiles,),
            in_specs=[
                pl.BlockSpec(memory_space=pltpu.HBM),
                pl.BlockSpec((None, D_MODEL, gcols), by_group),
                pl.BlockSpec((None, D_MODEL, gcols), by_group),
                pl.BlockSpec((None, gcols, D_MODEL), by_group),
                pl.BlockSpec((1, D_MODEL), const),
                pl.BlockSpec((1, D_MODEL), const),
            ],
            out_specs=pl.BlockSpec(memory_space=pltpu.HBM),
            scratch_shapes=[
                pltpu.VMEM((2, tmg, XG_COLS), F32),
                pltpu.VMEM((2, tmg, D_MODEL), F32),
                pltpu.SemaphoreType.DMA((2,)),
                pltpu.SemaphoreType.DMA((2,)),
            ],
        ),
        out_shape=jax.ShapeDtypeStruct((t, D_MODEL), F32),
        compiler_params=pltpu.CompilerParams(
            dimension_semantics=("arbitrary",), vmem_limit_bytes=VMEM_LIMIT),
        name="moe",
    )(inv, tile_group, n_valid, xg, wg, wu, wd, ln2, fw)


def _invperm_kernel(pos_ref, lo_ref, hi_ref, inv_ref):
    def clear(i, carry):
        inv_ref[i] = 0
        return carry
    for k in range(lo_ref.shape[0]):
        lax.fori_loop(lo_ref[k], hi_ref[k], clear, 0)

    def place(t, carry):
        inv_ref[pos_ref[t]] = t
        return carry
    lax.fori_loop(0, pos_ref.shape[0], place, 0, unroll=8)


def _invperm(pos, pad_lo, pad_hi, n):
    smem = pl.BlockSpec(memory_space=pltpu.SMEM)
    return pl.pallas_call(
        _invperm_kernel,
        in_specs=[smem, smem, smem],
        out_specs=smem,
        out_shape=jax.ShapeDtypeStruct((n,), jnp.int32),
        name="invperm",
    )(pos, pad_lo, pad_hi)


def _route(xg, tmg):
    t = xg.shape[0]
    g = xg[:, D_MODEL].astype(jnp.int32)
    rank = xg[:, D_MODEL + 1].astype(jnp.int32)
    counts = jnp.sum(g[:, None] == jnp.arange(N_EGROUPS, dtype=jnp.int32)[None, :], axis=0, dtype=jnp.int32)
    tiles_per = (counts + tmg - 1) // tmg
    tile_end = jnp.cumsum(tiles_per)
    tile_base = tile_end - tiles_per
    n_tiles = t // tmg + N_EGROUPS
    pos = tile_base[g] * tmg + rank
    pad_lo = jnp.concatenate([tile_base * tmg + counts, tile_end[-1:] * tmg]).astype(jnp.int32)
    pad_hi = jnp.concatenate([tile_end * tmg, jnp.full((1,), n_tiles * tmg, jnp.int32)]).astype(jnp.int32)
    inv = _invperm(pos, pad_lo, pad_hi, n_tiles * tmg)
    ti = jnp.arange(n_tiles, dtype=jnp.int32)
    tg = jnp.minimum(jnp.sum(ti[:, None] >= tile_end[None, :], axis=1, dtype=jnp.int32), N_EGROUPS - 1)
    nv = jnp.clip(counts[tg] - (ti - tile_base[tg]) * tmg, 0, tmg)
    nv = jnp.where(ti < tile_end[-1], nv, 0).astype(jnp.int32)
    return inv, tg, nv


def _pad_lanes(v, n=LANES):
    v = v.reshape(1, -1).astype(F32)
    return jnp.pad(v, ((0, 0), (0, n - v.shape[1])))


def kernel(x_prompt, x_sample, state_ssm, state_ssd_conv, state_short_conv, meta_tokens, ln1_w, w_in,
           ssd_conv_w, ssd_conv_b, ssd_dt_bias, ssd_A_log, ssd_D, ssd_norm_w, w_ssd_out, sc_conv_w,
           w_sc_out, w_o, ln2_w, w_rg, b_rg, w_re, b_re, w_gate, w_up, w_down, final_norm_w):
    assert ln1_w.shape[0] == 1, "single-layer trunk"
    bp, sp, _ = x_prompt.shape
    bs, ss, _ = x_sample.shape
    l = 0

    wt = jnp.swapaxes(w_in[l], 0, 1)
    wdt = jnp.pad(wt[A_COLS:A_COLS + SSD_HEADS], ((0, LANES - SSD_HEADS), (0, 0))).astype(BF16)
    lnw = ln1_w[l].reshape(1, D_MODEL)
    prm = (ssd_conv_w[l], ssd_conv_b[l].reshape(1, -1), sc_conv_w[l],
           _pad_lanes(ssd_dt_bias[l]), _pad_lanes(ssd_A_log[l]),
           jnp.repeat(ssd_D[l].astype(F32), SSD_HEAD_DIM).reshape(1, SSD_INNER),
           ssd_norm_w[l].reshape(1, SSD_INNER))
    wa = w_ssd_out[l].astype(BF16)
    ws = w_sc_out[l].astype(BF16)
    wo = w_o[l].astype(BF16)
    ln2 = ln2_w[l].reshape(1, D_MODEL)
    wr = jnp.pad(jnp.concatenate([w_rg[l], w_re[l]], axis=1),
                 ((0, 0), (0, LANES - N_EGROUPS - N_EXPERTS))).astype(BF16)
    br = _pad_lanes(jnp.concatenate([b_rg[l], b_re[l]]))
    gcols = EXPERTS_PER_GROUP * D_EXPERT
    fw = final_norm_w.reshape(1, D_MODEL)
    halves = 2 * N_EXPERTS
    cast32 = (w_gate[l].reshape(halves, D_MODEL // 2, D_EXPERT),
              w_up[l].reshape(halves, D_MODEL // 2, D_EXPERT),
              w_down[l].reshape(halves, D_EXPERT // 2, D_MODEL))

    def mlp_tail(u, v, g, x, tmg):
        xg = _merge(u, v, g, x, wa, ws, wo, ln2, wr, br, tm=1024)
        inv, tg, nv = _route(xg, tmg)
        return _moe(xg, inv, tg, nv, wg, wu, wd, ln2, fw, tmg)

    q = LANES
    xm = jnp.pad(meta_tokens.astype(F32), ((q - N_META, 0), (0, 0)))
    xs = x_sample.reshape(bs * ss, D_MODEL)
    (sproj, sdt), (mproj, mdt) = _inproj(xs, lnw, wt, wdt, tm=bs * ss, extra=xm)
    zeros_hist = jnp.zeros((1, SSD_CONV - 1, SSD_CONV_DIM), F32)
    zeros_sch = jnp.zeros((1, SC_CONV - 1, SC_WIDTH), F32)
    zeros_st = jnp.zeros((1, SSD_INNER, SSD_STATE), F32)
    _, _, m_st, m_cs, m_scs = _seqmix(mproj, mdt, zeros_hist, zeros_sch, zeros_st, prm,
                                      nb=1, nc=1, q=q, ns=1, pad_rows=q - N_META, init_bcast=True)

    xp = x_prompt.reshape(bp * sp, D_MODEL)
    def by_group_cols(s):
        e = s // 2
        return (e // EXPERTS_PER_GROUP, s % 2, e % EXPERTS_PER_GROUP)

    wide = ((N_EGROUPS, D_MODEL, gcols), by_group_cols)
    (pproj, pdt), (wg, wu, wd) = _inproj(xp, lnw, wt, wdt, tm=2048, cast=cast32, cast_dst=(wide, wide))
    wd = wd.reshape(N_EGROUPS, gcols, D_MODEL)
    pu, pv, p_st, p_cs, p_scs = _seqmix(pproj, pdt, m_cs, m_scs, m_st, prm, nb=bp, nc=sp // q, q=q,
                                        ns=1, pad_rows=0, init_bcast=True, cps=4)
    y_prompt = mlp_tail(pu, pv, pproj, xp, MOE_TILE).reshape(bp, sp, D_MODEL)

    ns = 8
    su, sv, s_st, s_cs, s_scs = _seqmix(sproj, sdt, state_ssd_conv[l], state_short_conv[l],
                                        state_ssm[l].reshape(bs, SSD_INNER, SSD_STATE), prm,
                                        nb=bs // ns, nc=1, q=ns * ss, ns=ns, pad_rows=0, init_bcast=False)
    y_sample = mlp_tail(su, sv, sproj, xs, MOE_TILE_SMALL).reshape(bs, ss, D_MODEL)

    hshape = (SSD_HEADS, SSD_HEAD_DIM, SSD_STATE)
    return (y_prompt, y_sample,
            p_st.reshape(1, bp, *hshape), p_cs[None], p_scs[None],
            s_st.reshape(1, bs, *hshape), s_cs[None], s_scs[None])
```

```python
import functools

import jax
import jax.numpy as jnp
import numpy as np
from jax import lax
from jax.experimental import pallas as pl
from jax.experimental.pallas import tpu as pltpu

F32 = jnp.float32
BF16 = jnp.bfloat16

D_MODEL = 1024
N_META = 16
SSD_INNER = 2048
SSD_HEAD_DIM = 64
SSD_HEADS = 32
SSD_GROUPS = 4
SSD_HPG = 8
SSD_STATE = 128
SSD_CONV = 4
SSD_CONV_DIM = 3072
SC_WIDTH = 1024
SC_CONV = 3
N_EGROUPS = 4
EXPERTS_PER_GROUP = 8
N_EXPERTS = 32
D_EXPERT = 256
EPS = 1e-6

LANES = 128
SUBLANES = 8
GROUP_COLS = SSD_HPG * SSD_HEAD_DIM
A_COLS = SSD_INNER + SSD_CONV_DIM
S_COLS = 3 * SC_WIDTH
G_COLS = 2 * D_MODEL
PROJ_COLS = A_COLS + S_COLS + G_COLS
XG_COLS = D_MODEL + LANES
MOE_TILE = 256
MOE_TILE_SMALL = 128
RANK_BLOCK = 512
NEG_BIG = -1e30
LOG2E = 1.4426950408889634
VMEM_LIMIT = 56 * 1024 * 1024


def _nt_dot(a, b):
    return lax.dot_general(a, b, (((1,), (1,)), ((), ())), preferred_element_type=F32)


def _dot01(m01_bf16, x):
    hi = x.astype(BF16)
    r1 = x - hi.astype(F32)
    mid = r1.astype(BF16)
    lo = (r1 - mid.astype(F32)).astype(BF16)
    out = jnp.dot(m01_bf16, hi, preferred_element_type=F32)
    out = out + jnp.dot(m01_bf16, mid, preferred_element_type=F32)
    return out + jnp.dot(m01_bf16, lo, preferred_element_type=F32)


def _softplus(x):
    return jnp.maximum(x, 0.0) + jnp.log1p(jnp.exp(-jnp.abs(x)))


def _sigmoid(x):
    return 1.0 / (1.0 + jnp.exp2(x * (-LOG2E)))


def _inproj_kernel(n_sets, n_cast, cast_blocks, *refs):
    x_refs = refs[:n_sets]
    lnw_ref, wt_ref, wdt_ref = refs[n_sets:n_sets + 3]
    cast_in = refs[n_sets + 3:n_sets + 3 + n_cast]
    outs = refs[n_sets + 3 + n_cast:n_sets + 3 + n_cast + 2 * n_sets]
    cast_out = refs[n_sets + 3 + n_cast + 2 * n_sets:n_sets + 3 + 2 * n_cast + 2 * n_sets]
    h_refs = refs[n_sets + 3 + 2 * n_cast + 2 * n_sets:]

    if n_cast:
        step = pl.program_id(0) * pl.num_programs(1) + pl.program_id(1)

        @pl.when(step < cast_blocks)
        def _():
            for src, dst in zip(cast_in, cast_out):
                dst[...] = src[...].astype(BF16)

    @pl.when(pl.program_id(1) == 0)
    def _():
        for k in range(n_sets):
            x = x_refs[k][...]
            ms = jnp.mean(x * x, axis=-1, keepdims=True)
            h = (x * lax.rsqrt(ms + EPS) * lnw_ref[...]).astype(BF16)
            h_refs[k][...] = h
            outs[2 * k + 1][...] = _nt_dot(h, wdt_ref[...])

    w = wt_ref[...].astype(BF16)
    for k in range(n_sets):
        outs[2 * k][...] = _nt_dot(h_refs[k][...], w).astype(BF16)


def _inproj(x, lnw, wt, wdt, tm, tn=1024, extra=None, cast=(), cast_dst=()):
    t = x.shape[0]
    n_a = A_COLS // tn
    nj = PROJ_COLS // tn
    cast_blocks = cast[0].shape[0] if cast else 0
    assert all(c.shape[0] == cast_blocks for c in cast) and cast_blocks <= (t // tm) * nj

    def cast_block(i, j):
        return jnp.minimum(i * nj + j, cast_blocks - 1)

    cast_specs = [pl.BlockSpec((1,) + c.shape[1:], lambda i, j: (cast_block(i, j), 0, 0)) for c in cast]
    cast_out_specs, cast_out_shapes = [], []
    for k, c in enumerate(cast):
        dst = cast_dst[k] if k < len(cast_dst) else None
        if dst is None:
            cast_out_specs.append(cast_specs[k])
            cast_out_shapes.append(jax.ShapeDtypeStruct(c.shape, BF16))
        else:
            shape, place = dst
            cast_out_specs.append(pl.BlockSpec((1,) + c.shape[1:],
                                               lambda i, j, place=place: place(cast_block(i, j))))
            cast_out_shapes.append(jax.ShapeDtypeStruct(shape, BF16))

    def w_rows(i, j):
        return (pl.multiple_of(jnp.where(j < n_a, j * tn, j * tn + SSD_HEADS), SSD_HEADS), 0)

    sets = [(x, tm, lambda i: i)]
    if extra is not None:
        assert t == tm, "extra rows are recomputed per row tile"
        sets.append((extra, extra.shape[0], lambda i: 0))
    x_specs, out_specs, out_shape, scratch = [], [], [], []
    for xs_, rows, ri in sets:
        x_specs.append(pl.BlockSpec((rows, D_MODEL), lambda i, j, ri=ri: (ri(i), 0)))
        out_specs += [pl.BlockSpec((rows, tn), lambda i, j, ri=ri: (ri(i), j)),
                      pl.BlockSpec((rows, LANES), lambda i, j, ri=ri: (ri(i), 0))]
        out_shape += [jax.ShapeDtypeStruct((xs_.shape[0], PROJ_COLS), BF16),
                      jax.ShapeDtypeStruct((xs_.shape[0], LANES), F32)]
        scratch.append(pltpu.VMEM((rows, D_MODEL), BF16))
    res = pl.pallas_call(
        functools.partial(_inproj_kernel, len(sets), len(cast), cast_blocks),
        grid=(t // tm, nj),
        in_specs=x_specs + [
            pl.BlockSpec((1, D_MODEL), lambda i, j: (0, 0)),
            pl.BlockSpec((pl.Element(tn), pl.Element(D_MODEL)), w_rows),
            pl.BlockSpec((LANES, D_MODEL), lambda i, j: (0, 0)),
        ] + cast_specs,
        out_specs=out_specs + cast_out_specs,
        out_shape=out_shape + cast_out_shapes,
        scratch_shapes=scratch,
        compiler_params=pltpu.CompilerParams(
            dimension_semantics=("arbitrary", "arbitrary"), vmem_limit_bytes=VMEM_LIMIT),
        name="inproj",
    )(*[st[0] for st in sets], lnw, wt, wdt, *cast)
    n_proj = 2 * len(sets)
    projs = tuple(tuple(res[2 * k:2 * k + 2]) for k in range(len(sets)))
    projs = projs if extra is not None else projs[0]
    return (projs, tuple(res[n_proj:])) if cast else projs


def halo_rows(ns):
    return max(2 * SUBLANES, ns * SUBLANES)


def _shift_matrices(q, ns, taps):
    L = q // ns
    hb = halo_rows(ns)
    r = np.arange(q)[:, None]
    col = np.arange(3 * hb + q)[None, :]
    seq, t = r // L, r % L
    mats = []
    for k in range(taps - 1):
        d = taps - 1 - k
        from_cur = (col >= 3 * hb) & (col - 3 * hb == r - d) & (t >= d)
        from_hist = (col < 3 * hb) & (col % hb == seq * SUBLANES + SUBLANES - d + t) & (t < d)
        mats.append(from_cur | from_hist)
    return jnp.asarray(np.stack(mats), dtype=BF16)


def _seqmix_kernel(Q, NS, CPS, one_step, pad_rows,
                   a_blk, sb_blk, sc_blk, sh_blk, dt_blk, hist_ref, schist_ref, st0_ref, shift_ref,
                   cw_ref, cb_ref, scw_ref, dtb_ref, alog_ref, drow_ref, nw_ref,
                   u_blk, v_blk, st_ref, cs_ref, scs_ref,
                   halo, scpad, xbc_s, y_s, xst_s):
    @pl.when(pl.program_id(1) == 0)
    def _():
        halo[...] = jnp.zeros_like(halo)
        for s in range(NS):
            halo[(s + 1) * SUBLANES - (SSD_CONV - 1):(s + 1) * SUBLANES, :] = hist_ref[s]
            scpad[s, SUBLANES - (SC_CONV - 1):SUBLANES, :] = schist_ref[s]
        if not one_step:
            st_ref[...] = st0_ref[...]

    src_ref = st0_ref if one_step else st_ref
    shared = (shift_ref, cw_ref, cb_ref, scw_ref, dtb_ref, alog_ref, drow_ref, nw_ref,
              src_ref, st_ref, cs_ref, scs_ref, halo, scpad, xbc_s, y_s, xst_s)
    row_blocks = (a_blk, sb_blk, sc_blk, sh_blk, dt_blk, u_blk, v_blk)
    if CPS == 1:
        _seqmix_chunk(Q, NS, pad_rows, *row_blocks, *shared)
    else:
        def body(k, carry):
            rows = pl.ds(pl.multiple_of(k * Q, Q), Q)
            _seqmix_chunk(Q, NS, pad_rows, *[r.at[rows] for r in row_blocks], *shared)
            return carry
        lax.fori_loop(0, CPS, body, 0)


def _seqmix_chunk(Q, NS, pad_rows, a_ref, sb_ref, sc_ref, sh_ref, dt_ref, u_ref, v_ref,
                  shift_ref, cw_ref, cb_ref, scw_ref, dtb_ref, alog_ref, drow_ref, nw_ref,
                  src_ref, st_ref, cs_ref, scs_ref, halo, scpad, xbc_s, y_s, xst_s):
    L = Q // NS
    lg = L.bit_length() - 1

    CW = 256
    hb = halo_rows(NS)
    for cc in range(0, SSD_CONV_DIM, CW):
        raw_b = a_ref[:, SSD_INNER + cc:SSD_INNER + cc + CW]
        hl = halo[:, cc:cc + CW]
        if hb > hl.shape[0]:
            hl = jnp.concatenate([hl, jnp.zeros((hb - hl.shape[0], CW), F32)], axis=0)
        h_hi = hl.astype(BF16)
        h_r = hl - h_hi.astype(F32)
        h_mid = h_r.astype(BF16)
        h_lo = (h_r - h_mid.astype(F32)).astype(BF16)
        ext = jnp.concatenate([h_hi, h_mid, h_lo, raw_b], axis=0)
        raw = raw_b.astype(F32)
        acc = raw * cw_ref[3:4, cc:cc + CW] + cb_ref[:, cc:cc + CW]
        for k in range(SSD_CONV - 1):
            acc = acc + jnp.dot(shift_ref[k], ext, preferred_element_type=F32) * cw_ref[k:k + 1, cc:cc + CW]
        xbc_s[:, cc:cc + CW] = acc * _sigmoid(acc)
        for s in range(NS):
            halo[s * SUBLANES:(s + 1) * SUBLANES, cc:cc + CW] = raw[(s + 1) * L - SUBLANES:(s + 1) * L]
    for s in range(NS):
        cs_ref[s] = halo[(s + 1) * SUBLANES - (SSD_CONV - 1):(s + 1) * SUBLANES, :]

    for cc in range(0, SC_WIDTH, CW):
        scb = sb_ref[:, cc:cc + CW].astype(F32)
        ch_all = sc_ref[:, cc:cc + CW].astype(F32) * sh_ref[:, cc:cc + CW].astype(F32)
        for s in range(NS):
            ch = ch_all[s * L:(s + 1) * L]
            scpad[s, SUBLANES:SUBLANES + L, cc:cc + CW] = ch
            acc = ch * scw_ref[2:3, cc:cc + CW]
            for k in range(SC_CONV - 1):
                acc = acc + scpad[s, 6 + k:6 + k + L, cc:cc + CW] * scw_ref[k:k + 1, cc:cc + CW]
            v_ref[s * L:(s + 1) * L, cc:cc + CW] = (scb[s * L:(s + 1) * L] * acc).astype(BF16)
    for s in range(NS):
        scs_ref[s] = scpad[s, L + 6:L + 8, :]
        scpad[s, 0:SUBLANES, :] = scpad[s, L:L + SUBLANES, :]

    def padrows(x):
        if Q == LANES:
            return x
        return jnp.concatenate([x, jnp.zeros((LANES - Q, x.shape[1]), x.dtype)], axis=0)

    li = lax.broadcasted_iota(jnp.int32, (Q, Q), 0)
    si = lax.broadcasted_iota(jnp.int32, (Q, Q), 1)
    same = (li >> lg) == (si >> lg)
    causal = jnp.logical_and(same, si <= li)
    tril01 = jnp.where(causal, 1.0, 0.0).astype(BF16)
    same01 = jnp.where(same, 1.0, 0.0).astype(BF16)

    dt = _softplus(dt_ref[...] + dtb_ref[...])
    if pad_rows:
        ri = lax.broadcasted_iota(jnp.int32, (Q, LANES), 0)
        dt = jnp.where(ri >= pad_rows, dt, 0.0)
    da = dt * (-jnp.exp(alog_ref[...]))
    acum = _dot01(tril01, da)
    tot = _dot01(same01, da)
    acum_t = padrows(acum).T
    tot_t = padrows(tot).T
    dt_t = padrows(dt).T
    w_t = jnp.exp(tot_t - acum_t) * dt_t
    acum2 = acum * LOG2E
    rowq_t = acum_t * LOG2E - jnp.log2(dt_t)

    left_head = lax.broadcasted_iota(jnp.int32, (Q, LANES), 1) < SSD_HEAD_DIM
    rowseq = lax.broadcasted_iota(jnp.int32, (Q, GROUP_COLS), 0) >> lg
    rowseq_p = lax.broadcasted_iota(jnp.int32, (LANES, LANES), 0) >> lg

    for g in range(SSD_GROUPS):
        b_g = xbc_s[:, SSD_INNER + g * SSD_STATE:SSD_INNER + (g + 1) * SSD_STATE]
        c_g = xbc_s[:, SSD_INNER + GROUP_COLS + g * SSD_STATE:SSD_INNER + GROUP_COLS + (g + 1) * SSD_STATE]
        b_gb = b_g.astype(BF16)
        c_gb = c_g.astype(BF16)
        cbm = _nt_dot(c_gb, b_gb)
        yo = None
        for s in range(NS):
            h_s = src_ref[s, g * GROUP_COLS:(g + 1) * GROUP_COLS, :].astype(BF16)
            yo_s = _nt_dot(c_gb, h_s)
            yo = yo_s if yo is None else jnp.where(rowseq == s, yo_s, yo)
        for rp in range(SSD_HPG // 2):
            h0 = g * SSD_HPG + 2 * rp
            cols = slice(h0 * SSD_HEAD_DIM, (h0 + 2) * SSD_HEAD_DIM)
            colbs, w_pair = [], []
            for h in (h0, h0 + 1):
                colb = jnp.broadcast_to(acum2[:, h:h + 1], (Q, LANES))
                rowb = jnp.broadcast_to(rowq_t[h:h + 1, 0:Q], (Q, Q))
                dec_dt = jnp.exp2(jnp.where(causal, colb[:, 0:Q] - rowb, NEG_BIG))
                w_pair.append((cbm * dec_dt).astype(BF16))
                colbs.append(colb)
            x_p = xbc_s[:, cols]
            rhs = jnp.concatenate([jnp.where(left_head, x_p, 0.0).astype(BF16),
                                   jnp.where(left_head, 0.0, x_p).astype(BF16)], axis=0)
            yd = jnp.dot(jnp.concatenate(w_pair, axis=1), rhs, preferred_element_type=F32)
            ecol = jnp.exp2(jnp.where(left_head, colbs[0], colbs[1]))
            y_s[:, cols] = yd + yo[:, rp * LANES:(rp + 1) * LANES] * ecol + drow_ref[:, cols] * x_p

    for jb in range(SSD_INNER // LANES):
        xst_s[jb * LANES:(jb + 1) * LANES, :] = padrows(xbc_s[:, jb * LANES:(jb + 1) * LANES]).T
    for s in range(NS):
        da_b = jnp.exp(jnp.broadcast_to(tot_t[:, s * L:s * L + 1], (LANES, LANES)))
        for g in range(SSD_GROUPS):
            b_p = padrows(xbc_s[:, SSD_INNER + g * SSD_STATE:SSD_INNER + (g + 1) * SSD_STATE])
            if NS > 1:
                b_p = jnp.where(rowseq_p == s, b_p, 0.0)
            pieces = []
            for r in range(SSD_HPG):
                h = g * SSD_HPG + r
                pieces.append(xst_s[h * SSD_HEAD_DIM:(h + 1) * SSD_HEAD_DIM, :] * w_t[h:h + 1, :])
            xw_t = jnp.concatenate(pieces, axis=0).astype(BF16)
            upd = jnp.dot(xw_t, b_p.astype(BF16), preferred_element_type=F32)
            for r in range(SSD_HPG):
                h = g * SSD_HPG + r
                rows = slice(h * SSD_HEAD_DIM, (h + 1) * SSD_HEAD_DIM)
                dec_h = jnp.broadcast_to(da_b[h:h + 1, :], (SSD_HEAD_DIM, SSD_STATE))
                st_ref[s, rows, :] = dec_h * src_ref[s, rows, :] + upd[r * SSD_HEAD_DIM:(r + 1) * SSD_HEAD_DIM, :]

    for g in range(SSD_GROUPS):
        cols = slice(g * GROUP_COLS, (g + 1) * GROUP_COLS)
        z = a_ref[:, cols].astype(F32)
        ug = y_s[:, cols] * (z * _sigmoid(z))
        ms = jnp.mean(ug * ug, axis=-1, keepdims=True)
        u_ref[:, cols] = (ug * lax.rsqrt(ms + EPS) * nw_ref[:, cols]).astype(BF16)


def _seqmix(proj, pdt, hist, schist, st0, prm, *, nb, nc, q, ns, pad_rows, init_bcast, cps=1):
    L = q // ns
    cw, cb, scw, dtb, alog, drow, nw = prm
    steps = nc // cps
    rows = q * cps
    sc_block0 = A_COLS // SC_WIDTH

    def tok(b, c):
        return (b * steps + c, 0)

    def init3(b, c):
        return (0 if init_bcast else b, 0, 0)

    def const2(b, c):
        return (0, 0)

    def per_b(b, c):
        return (b, 0, 0)

    n_seq = nb * ns
    shifts = _shift_matrices(q, ns, SSD_CONV)
    sc_specs = [pl.BlockSpec((rows, SC_WIDTH), lambda b, c, k=k: (b * steps + c, sc_block0 + k))
                for k in range(3)]
    return pl.pallas_call(
        functools.partial(_seqmix_kernel, q, ns, cps, steps == 1 and cps == 1, pad_rows),
        grid=(nb, steps),
        in_specs=[pl.BlockSpec((rows, A_COLS), tok)] + sc_specs + [
            pl.BlockSpec((rows, LANES), tok),
            pl.BlockSpec((ns, SSD_CONV - 1, SSD_CONV_DIM), init3),
            pl.BlockSpec((ns, SC_CONV - 1, SC_WIDTH), init3),
            pl.BlockSpec((ns, SSD_INNER, SSD_STATE), init3),
            pl.BlockSpec(shifts.shape, lambda b, c: (0, 0, 0)),
            pl.BlockSpec((SSD_CONV, SSD_CONV_DIM), const2),
            pl.BlockSpec((1, SSD_CONV_DIM), const2),
            pl.BlockSpec((SC_CONV, SC_WIDTH), const2),
            pl.BlockSpec((1, LANES), const2),
            pl.BlockSpec((1, LANES), const2),
            pl.BlockSpec((1, SSD_INNER), const2),
            pl.BlockSpec((1, SSD_INNER), const2),
        ],
        out_specs=[
            pl.BlockSpec((rows, SSD_INNER), tok),
            pl.BlockSpec((rows, SC_WIDTH), tok),
            pl.BlockSpec((ns, SSD_INNER, SSD_STATE), per_b),
            pl.BlockSpec((ns, SSD_CONV - 1, SSD_CONV_DIM), per_b),
            pl.BlockSpec((ns, SC_CONV - 1, SC_WIDTH), per_b),
        ],
        out_shape=[
            jax.ShapeDtypeStruct((nb * nc * q, SSD_INNER), BF16),
            jax.ShapeDtypeStruct((nb * nc * q, SC_WIDTH), BF16),
            jax.ShapeDtypeStruct((n_seq, SSD_INNER, SSD_STATE), F32),
            jax.ShapeDtypeStruct((n_seq, SSD_CONV - 1, SSD_CONV_DIM), F32),
            jax.ShapeDtypeStruct((n_seq, SC_CONV - 1, SC_WIDTH), F32),
        ],
        scratch_shapes=[
            pltpu.VMEM((ns * SUBLANES, SSD_CONV_DIM), F32),
            pltpu.VMEM((ns, SUBLANES + L, SC_WIDTH), F32),
            pltpu.VMEM((q, SSD_CONV_DIM), F32),
            pltpu.VMEM((q, SSD_INNER), F32),
            pltpu.VMEM((SSD_INNER, LANES), F32),
        ],
        compiler_params=pltpu.CompilerParams(
            dimension_semantics=("arbitrary", "arbitrary"), vmem_limit_bytes=VMEM_LIMIT),
        name="seqmix",
    )(proj, proj, proj, proj, pdt, hist, schist, st0, shifts, cw, cb, scw, dtb, alog, drow, nw)


def _merge_kernel(u_ref, v_ref, g_ref, x_ref, wa_ref, ws_ref, wo_ref, ln2_ref, wr_ref, br_ref,
                  xg_ref, cnt_s):
    i = pl.program_id(0)

    @pl.when(i == 0)
    def _():
        cnt_s[...] = jnp.zeros_like(cnt_s)

    y_ssd = jnp.dot(u_ref[...], wa_ref[...], preferred_element_type=F32)
    y_sc = jnp.dot(v_ref[...], ws_ref[...], preferred_element_type=F32)
    g1 = g_ref[:, 0:D_MODEL].astype(F32)
    g2 = g_ref[:, D_MODEL:2 * D_MODEL].astype(F32)
    merged = (_sigmoid(g1) * y_ssd + _sigmoid(g2) * y_sc).astype(BF16)
    x1 = x_ref[...] + jnp.dot(merged, wo_ref[...], preferred_element_type=F32)
    xg_ref[:, 0:D_MODEL] = x1
    ms = jnp.mean(x1 * x1, axis=-1, keepdims=True)
    h2 = (x1 * lax.rsqrt(ms + EPS) * ln2_ref[...]).astype(BF16)

    logits = jnp.dot(h2, wr_ref[...], preferred_element_type=F32) + br_ref[...]
    lane = lax.broadcasted_iota(jnp.int32, logits.shape, 1).astype(F32)
    big = float(LANES)
    gl = jnp.where(lane < N_EGROUPS, logits, NEG_BIG)
    gmax = jnp.max(gl, axis=-1, keepdims=True)
    g_sel = jnp.min(jnp.where(gl == gmax, lane, big), axis=-1, keepdims=True)
    gsum = jnp.sum(jnp.exp(gl - gmax), axis=-1, keepdims=True)
    g_prob = 1.0 / gsum
    lo = N_EGROUPS + EXPERTS_PER_GROUP * g_sel
    emask = jnp.logical_and(lane >= lo, lane < lo + EXPERTS_PER_GROUP)
    el = jnp.where(emask, logits, NEG_BIG)
    m1 = jnp.max(el, axis=-1, keepdims=True)
    e = jnp.where(emask, jnp.exp(el - m1), -1.0)
    i1 = jnp.min(jnp.where(e == 1.0, lane, big), axis=-1, keepdims=True)
    e_rest = jnp.where(lane == i1, -1.0, e)
    e2 = jnp.max(e_rest, axis=-1, keepdims=True)
    i2 = jnp.min(jnp.where(e_rest == e2, lane, big), axis=-1, keepdims=True)
    denom = 1.0 + e2
    w1 = g_prob / denom
    w2 = g_prob * e2 / denom
    gates = jnp.where(lane == i1, w1, 0.0) + jnp.where(lane == i2, w2, 0.0)

    tm = logits.shape[0]
    rb = min(tm, RANK_BLOCK)
    onehot = jnp.where(lane == g_sel, 1.0, 0.0).astype(BF16)
    ri = lax.broadcasted_iota(jnp.int32, (rb, rb), 0)
    ci = lax.broadcasted_iota(jnp.int32, (rb, rb), 1)
    tril01 = jnp.where(ci <= ri, 1.0, 0.0).astype(BF16)
    carry = cnt_s[...]
    incl_blocks = []
    for r0 in range(0, tm, rb):
        incl_b = jnp.dot(tril01, onehot[r0:r0 + rb], preferred_element_type=F32) + carry
        carry = incl_b[rb - 1:rb, :]
        incl_blocks.append(incl_b)
    cnt_s[...] = carry
    incl = incl_blocks[0] if len(incl_blocks) == 1 else jnp.concatenate(incl_blocks, axis=0)
    rank = jnp.sum(jnp.where(lane == g_sel, incl - 1.0, 0.0), axis=-1, keepdims=True)
    xg_ref[:, D_MODEL:D_MODEL + LANES] = jnp.where(
        lane == 0, g_sel.astype(F32), jnp.where(lane == 1, rank, gates))


def _merge(u, v, g, x, wa, ws, wo, ln2, wr, br, tm):
    t = x.shape[0]
    row = lambda i: (i, 0)
    const = lambda i: (0, 0)
    return pl.pallas_call(
        _merge_kernel,
        grid=(t // tm,),
        in_specs=[
            pl.BlockSpec((tm, SSD_INNER), row),
            pl.BlockSpec((tm, SC_WIDTH), row),
            pl.BlockSpec((tm, G_COLS), lambda i: (i, (A_COLS + S_COLS) // G_COLS)),
            pl.BlockSpec((tm, D_MODEL), row),
            pl.BlockSpec((SSD_INNER, D_MODEL), const, pipeline_mode=pl.Buffered(1)),
            pl.BlockSpec((SC_WIDTH, D_MODEL), const, pipeline_mode=pl.Buffered(1)),
            pl.BlockSpec((D_MODEL, D_MODEL), const, pipeline_mode=pl.Buffered(1)),
            pl.BlockSpec((1, D_MODEL), const),
            pl.BlockSpec((D_MODEL, LANES), const),
            pl.BlockSpec((1, LANES), const),
        ],
        out_specs=pl.BlockSpec((tm, XG_COLS), row),
        out_shape=jax.ShapeDtypeStruct((t, XG_COLS), F32),
        scratch_shapes=[pltpu.VMEM((1, LANES), F32)],
        compiler_params=pltpu.CompilerParams(
            dimension_semantics=("arbitrary",), vmem_limit_bytes=VMEM_LIMIT),
        name="merge",
    )(u, v, g, x, wa, ws, wo, ln2, wr, br)


def _moe_kernel(tmg, inv_ref, tg_ref, nv_ref, xg_hbm, wg_ref, wu_ref, wd_ref, ln2_ref, fw_ref,
                y_hbm, xbuf, ybuf, sem_in, sem_out):
    i = pl.program_id(0)
    n = pl.num_programs(0)
    slot = i % 2

    def row_in(tok, r, slt):
        return pltpu.make_async_copy(xg_hbm.at[pl.ds(tok, 1)], xbuf.at[slt, pl.ds(r, 1)], sem_in.at[slt])

    def row_out(tok, r, slt):
        return pltpu.make_async_copy(ybuf.at[slt, pl.ds(r, 1)], y_hbm.at[pl.ds(tok, 1)], sem_out.at[slt])

    unroll = 8

    def gather(tile, slt):
        def body(r, carry):
            row_in(inv_ref[tile * tmg + r], r, slt).start()
            return carry
        lax.fori_loop(0, tmg, body, 0, unroll=unroll)

    def wait_gather(slt):
        pltpu.make_async_copy(xg_hbm.at[pl.ds(0, tmg)], xbuf.at[slt], sem_in.at[slt]).wait()

    def scatter(tile, slt):
        def body(r, carry):
            row_out(inv_ref[tile * tmg + r], r, slt).start()
            return carry
        full = nv_ref[tile] == tmg

        @pl.when(full)
        def _():
            for r in range(tmg):
                body(r, 0)

        @pl.when(jnp.logical_not(full))
        def _():
            lax.fori_loop(0, nv_ref[tile], body, 0)

    def wait_scatter(tile, slt):
        def body(r, carry):
            row_out(0, r, slt).wait()
            return carry
        full = nv_ref[tile] == tmg

        @pl.when(full)
        def _():
            pltpu.make_async_copy(ybuf.at[slt], y_hbm.at[pl.ds(0, tmg)], sem_out.at[slt]).wait()

        @pl.when(jnp.logical_not(full))
        def _():
            lax.fori_loop(0, nv_ref[tile], body, 0)

    @pl.when(i == 0)
    def _():
        gather(0, 0)

    nxt = jnp.minimum(i + 1, n - 1)

    @pl.when(i >= 2)
    def _():
        wait_scatter(jnp.maximum(i - 2, 0), slot)

    @pl.when(nv_ref[i] == 0)
    def _():
        gather(nxt, 1 - slot)
        wait_gather(slot)

    @pl.when(nv_ref[i] > 0)
    def _():
        wait_gather(slot)
        x1 = xbuf[slot, :, 0:D_MODEL]
        for r in range(tmg // 2):
            row_in(inv_ref[nxt * tmg + r], r, 1 - slot).start()
        gate = xbuf[slot, :, D_MODEL:D_MODEL + LANES]
        for r in range(tmg // 2, tmg):
            row_in(inv_ref[nxt * tmg + r], r, 1 - slot).start()
        x1_res = xbuf[slot, :, 0:D_MODEL]
        ms = jnp.mean(x1 * x1, axis=-1, keepdims=True)
        h2 = (x1 * lax.rsqrt(ms + EPS) * ln2_ref[...]).astype(BF16)
        lane = lax.broadcasted_iota(jnp.int32, gate.shape, 1)
        first = N_EGROUPS + EXPERTS_PER_GROUP * tg_ref[i]
        a = jnp.dot(h2, wg_ref[...], preferred_element_type=F32)
        up = jnp.dot(h2, wu_ref[...], preferred_element_type=F32)
        act = a * _sigmoid(a) * up
        pieces = []
        for e in range(EXPERTS_PER_GROUP):
            ge = jnp.sum(jnp.where(lane == first + e, gate, 0.0), axis=-1, keepdims=True)
            pieces.append((act[:, e * D_EXPERT:(e + 1) * D_EXPERT] * ge).astype(BF16))
        mo = jnp.dot(jnp.concatenate(pieces, axis=1), wd_ref[...], preferred_element_type=F32)
        x2 = x1_res + mo
        ms2 = jnp.mean(x2 * x2, axis=-1, keepdims=True)
        ybuf[slot] = x2 * lax.rsqrt(ms2 + EPS) * fw_ref[...]
        scatter(i, slot)

    @pl.when(i == n - 1)
    def _():
        wait_gather(1 - slot)

        @pl.when(i >= 1)
        def _():
            wait_scatter(jnp.maximum(i - 1, 0), 1 - slot)
        wait_scatter(i, slot)


def _moe(xg, inv, tile_group, n_valid, wg, wu, wd, ln2, fw, tmg):
    t = xg.shape[0]
    n_tiles = tile_group.shape[0]
    gcols = EXPERTS_PER_GROUP * D_EXPERT
    const = lambda i, inv, tg, nv: (0, 0)
    by_group = lambda i, inv, tg, nv: (tg[i], 0, 0)
    return pl.pallas_call(
        functools.partial(_moe_kernel, tmg),
        grid_spec=pltpu.PrefetchScalarGridSpec(
            num_scalar_prefetch=3,
            grid=(n_tiles,),
            in_specs=[
                pl.BlockSpec(memory_space=pltpu.HBM),
                pl.BlockSpec((None, D_MODEL, gcols), by_group),
                pl.BlockSpec((None, D_MODEL, gcols), by_group),
                pl.BlockSpec((None, gcols, D_MODEL), by_group),
                pl.BlockSpec((1, D_MODEL), const),
                pl.BlockSpec((1, D_MODEL), const),
            ],
            out_specs=pl.BlockSpec(memory_space=pltpu.HBM),
            scratch_shapes=[
                pltpu.VMEM((2, tmg, XG_COLS), F32),
                pltpu.VMEM((2, tmg, D_MODEL), F32),
                pltpu.SemaphoreType.DMA((2,)),
                pltpu.SemaphoreType.DMA((2,)),
            ],
        ),
        out_shape=jax.ShapeDtypeStruct((t, D_MODEL), F32),
        compiler_params=pltpu.CompilerParams(
            dimension_semantics=("arbitrary",), vmem_limit_bytes=VMEM_LIMIT),
        name="moe",
    )(inv, tile_group, n_valid, xg, wg, wu, wd, ln2, fw)


def _invperm_kernel(pos_ref, lo_ref, hi_ref, inv_ref):
    def clear(i, carry):
        inv_ref[i] = 0
        return carry
    for k in range(lo_ref.shape[0]):
        lax.fori_loop(lo_ref[k], hi_ref[k], clear, 0)

    def place(t, carry):
        inv_ref[pos_ref[t]] = t
        return carry
    lax.fori_loop(0, pos_ref.shape[0], place, 0, unroll=8)


def _invperm(pos, pad_lo, pad_hi, n):
    smem = pl.BlockSpec(memory_space=pltpu.SMEM)
    return pl.pallas_call(
        _invperm_kernel,
        in_specs=[smem, smem, smem],
        out_specs=smem,
        out_shape=jax.ShapeDtypeStruct((n,), jnp.int32),
        name="invperm",
    )(pos, pad_lo, pad_hi)


def _route(xg, tmg):
    t = xg.shape[0]
    g = xg[:, D_MODEL].astype(jnp.int32)
    rank = xg[:, D_MODEL + 1].astype(jnp.int32)
    counts = jnp.sum(g[:, None] == jnp.arange(N_EGROUPS, dtype=jnp.int32)[None, :], axis=0, dtype=jnp.int32)
    tiles_per = (counts + tmg - 1) // tmg
    tile_end = jnp.cumsum(tiles_per)
    tile_base = tile_end - tiles_per
    n_tiles = t // tmg + N_EGROUPS
    pos = tile_base[g] * tmg + rank
    pad_lo = jnp.concatenate([tile_base * tmg + counts, tile_end[-1:] * tmg]).astype(jnp.int32)
    pad_hi = jnp.concatenate([tile_end * tmg, jnp.full((1,), n_tiles * tmg, jnp.int32)]).astype(jnp.int32)
    inv = _invperm(pos, pad_lo, pad_hi, n_tiles * tmg)
    ti = jnp.arange(n_tiles, dtype=jnp.int32)
    tg = jnp.minimum(jnp.sum(ti[:, None] >= tile_end[None, :], axis=1, dtype=jnp.int32), N_EGROUPS - 1)
    nv = jnp.clip(counts[tg] - (ti - tile_base[tg]) * tmg, 0, tmg)
    nv = jnp.where(ti < tile_end[-1], nv, 0).astype(jnp.int32)
    return inv, tg, nv


def _pad_lanes(v, n=LANES):
    v = v.reshape(1, -1).astype(F32)
    return jnp.pad(v, ((0, 0), (0, n - v.shape[1])))


def kernel(x_prompt, x_sample, state_ssm, state_ssd_conv, state_short_conv, meta_tokens, ln1_w, w_in,
           ssd_conv_w, ssd_conv_b, ssd_dt_bias, ssd_A_log, ssd_D, ssd_norm_w, w_ssd_out, sc_conv_w,
           w_sc_out, w_o, ln2_w, w_rg, b_rg, w_re, b_re, w_gate, w_up, w_down, final_norm_w):
    assert ln1_w.shape[0] == 1, "single-layer trunk"
    bp, sp, _ = x_prompt.shape
    bs, ss, _ = x_sample.shape
    l = 0

    wt = jnp.swapaxes(w_in[l], 0, 1)
    wdt = jnp.pad(wt[A_COLS:A_COLS + SSD_HEADS], ((0, LANES - SSD_HEADS), (0, 0))).astype(BF16)
    lnw = ln1_w[l].reshape(1, D_MODEL)
    prm = (ssd_conv_w[l], ssd_conv_b[l].reshape(1, -1), sc_conv_w[l],
           _pad_lanes(ssd_dt_bias[l]), _pad_lanes(ssd_A_log[l]),
           jnp.repeat(ssd_D[l].astype(F32), SSD_HEAD_DIM).reshape(1, SSD_INNER),
           ssd_norm_w[l].reshape(1, SSD_INNER))
    nblk = 8
    out32 = (w_ssd_out[l].reshape(nblk, SSD_INNER // nblk, D_MODEL),
             w_sc_out[l].reshape(nblk, SC_WIDTH // nblk, D_MODEL),
             w_o[l].reshape(nblk, D_MODEL // nblk, D_MODEL))
    ln2 = ln2_w[l].reshape(1, D_MODEL)
    wr = jnp.pad(jnp.concatenate([w_rg[l], w_re[l]], axis=1),
                 ((0, 0), (0, LANES - N_EGROUPS - N_EXPERTS))).astype(BF16)
    br = _pad_lanes(jnp.concatenate([b_rg[l], b_re[l]]))
    gcols = EXPERTS_PER_GROUP * D_EXPERT
    fw = final_norm_w.reshape(1, D_MODEL)
    halves = 2 * N_EXPERTS
    cast32 = (w_gate[l].reshape(halves, D_MODEL // 2, D_EXPERT),
              w_up[l].reshape(halves, D_MODEL // 2, D_EXPERT),
              w_down[l].reshape(halves, D_EXPERT // 2, D_MODEL))

    def mlp_tail(u, v, g, x, tmg):
        xg = _merge(u, v, g, x, wa, ws, wo, ln2, wr, br, tm=1024)
        inv, tg, nv = _route(xg, tmg)
        return _moe(xg, inv, tg, nv, wg, wu, wd, ln2, fw, tmg)

    q = LANES
    xm = jnp.pad(meta_tokens.astype(F32), ((q - N_META, 0), (0, 0)))
    xs = x_sample.reshape(bs * ss, D_MODEL)
    ((sproj, sdt), (mproj, mdt)), (wa, ws, wo) = _inproj(xs, lnw, wt, wdt, tm=bs * ss, extra=xm, cast=out32)
    wa = wa.reshape(SSD_INNER, D_MODEL)
    ws = ws.reshape(SC_WIDTH, D_MODEL)
    wo = wo.reshape(D_MODEL, D_MODEL)
    zeros_hist = jnp.zeros((1, SSD_CONV - 1, SSD_CONV_DIM), F32)
    zeros_sch = jnp.zeros((1, SC_CONV - 1, SC_WIDTH), F32)
    zeros_st = jnp.zeros((1, SSD_INNER, SSD_STATE), F32)
    _, _, m_st, m_cs, m_scs = _seqmix(mproj, mdt, zeros_hist, zeros_sch, zeros_st, prm,
                                      nb=1, nc=1, q=q, ns=1, pad_rows=q - N_META, init_bcast=True)

    xp = x_prompt.reshape(bp * sp, D_MODEL)
    def by_group_cols(s):
        e = s // 2
        return (e // EXPERTS_PER_GROUP, s % 2, e % EXPERTS_PER_GROUP)

    wide = ((N_EGROUPS, D_MODEL, gcols), by_group_cols)
    (pproj, pdt), (wg, wu, wd) = _inproj(xp, lnw, wt, wdt, tm=2048, cast=cast32, cast_dst=(wide, wide))
    wd = wd.reshape(N_EGROUPS, gcols, D_MODEL)
    pu, pv, p_st, p_cs, p_scs = _seqmix(pproj, pdt, m_cs, m_scs, m_st, prm, nb=bp, nc=sp // q, q=q,
                                        ns=1, pad_rows=0, init_bcast=True, cps=4)
    y_prompt = mlp_tail(pu, pv, pproj, xp, MOE_TILE).reshape(bp, sp, D_MODEL)

    ns = 8
    su, sv, s_st, s_cs, s_scs = _seqmix(sproj, sdt, state_ssd_conv[l], state_short_conv[l],
                                        state_ssm[l].reshape(bs, SSD_INNER, SSD_STATE), prm,
                                        nb=bs // ns, nc=1, q=ns * ss, ns=ns, pad_rows=0, init_bcast=False)
    y_sample = mlp_tail(su, sv, sproj, xs, MOE_TILE_SMALL).reshape(bs, ss, D_MODEL)

    hshape = (SSD_HEADS, SSD_HEAD_DIM, SSD_STATE)
    return (y_prompt, y_sample,
            p_st.reshape(1, bp, *hshape), p_cs[None], p_scs[None],
            s_st.reshape(1, bs, *hshape), s_cs[None], s_scs[None])
```

```python
import functools

import jax
import jax.numpy as jnp
import numpy as np
from jax import lax
from jax.experimental import pallas as pl
from jax.experimental.pallas import tpu as pltpu

F32 = jnp.float32
BF16 = jnp.bfloat16

D_MODEL = 1024
N_META = 16
SSD_INNER = 2048
SSD_HEAD_DIM = 64
SSD_HEADS = 32
SSD_GROUPS = 4
SSD_HPG = 8
SSD_STATE = 128
SSD_CONV = 4
SSD_CONV_DIM = 3072
SC_WIDTH = 1024
SC_CONV = 3
N_EGROUPS = 4
EXPERTS_PER_GROUP = 8
N_EXPERTS = 32
D_EXPERT = 256
EPS = 1e-6

LANES = 128
SUBLANES = 8
GROUP_COLS = SSD_HPG * SSD_HEAD_DIM
A_COLS = SSD_INNER + SSD_CONV_DIM
S_COLS = 3 * SC_WIDTH
G_COLS = 2 * D_MODEL
PROJ_COLS = A_COLS + S_COLS + G_COLS
XG_COLS = D_MODEL + LANES
MOE_TILE = 256
MOE_TILE_SMALL = 128
RANK_BLOCK = 512
NEG_BIG = -1e30
LOG2E = 1.4426950408889634
VMEM_LIMIT = 56 * 1024 * 1024


def _nt_dot(a, b):
    return lax.dot_general(a, b, (((1,), (1,)), ((), ())), preferred_element_type=F32)


def _dot01(m01_bf16, x):
    hi = x.astype(BF16)
    r1 = x - hi.astype(F32)
    mid = r1.astype(BF16)
    lo = (r1 - mid.astype(F32)).astype(BF16)
    out = jnp.dot(m01_bf16, hi, preferred_element_type=F32)
    out = out + jnp.dot(m01_bf16, mid, preferred_element_type=F32)
    return out + jnp.dot(m01_bf16, lo, preferred_element_type=F32)


def _softplus(x):
    return jnp.maximum(x, 0.0) + jnp.log1p(jnp.exp(-jnp.abs(x)))


def _sigmoid(x):
    return 1.0 / (1.0 + jnp.exp2(x * (-LOG2E)))


def _inproj_kernel(n_sets, n_cast, cast_blocks, *refs):
    x_refs = refs[:n_sets]
    lnw_ref, wt_ref, wdt_ref = refs[n_sets:n_sets + 3]
    cast_in = refs[n_sets + 3:n_sets + 3 + n_cast]
    outs = refs[n_sets + 3 + n_cast:n_sets + 3 + n_cast + 2 * n_sets]
    cast_out = refs[n_sets + 3 + n_cast + 2 * n_sets:n_sets + 3 + 2 * n_cast + 2 * n_sets]
    h_refs = refs[n_sets + 3 + 2 * n_cast + 2 * n_sets:]

    if n_cast:
        step = pl.program_id(0) * pl.num_programs(1) + pl.program_id(1)

        @pl.when(step < cast_blocks)
        def _():
            for src, dst in zip(cast_in, cast_out):
                dst[...] = src[...].astype(BF16)

    @pl.when(pl.program_id(1) == 0)
    def _():
        for k in range(n_sets):
            x = x_refs[k][...]
            ms = jnp.mean(x * x, axis=-1, keepdims=True)
            h = (x * lax.rsqrt(ms + EPS) * lnw_ref[...]).astype(BF16)
            h_refs[k][...] = h
            outs[2 * k + 1][...] = _nt_dot(h, wdt_ref[...])

    w = wt_ref[...].astype(BF16)
    for k in range(n_sets):
        outs[2 * k][...] = _nt_dot(h_refs[k][...], w).astype(BF16)


def _inproj(x, lnw, wt, wdt, tm, tn=1024, extra=None, cast=(), cast_dst=()):
    t = x.shape[0]
    n_a = A_COLS // tn
    nj = PROJ_COLS // tn
    cast_blocks = cast[0].shape[0] if cast else 0
    assert all(c.shape[0] == cast_blocks for c in cast) and cast_blocks <= (t // tm) * nj

    def cast_block(i, j):
        return jnp.minimum(i * nj + j, cast_blocks - 1)

    cast_specs = [pl.BlockSpec((1,) + c.shape[1:], lambda i, j: (cast_block(i, j), 0, 0)) for c in cast]
    cast_out_specs, cast_out_shapes = [], []
    for k, c in enumerate(cast):
        dst = cast_dst[k] if k < len(cast_dst) else None
        if dst is None:
            cast_out_specs.append(cast_specs[k])
            cast_out_shapes.append(jax.ShapeDtypeStruct(c.shape, BF16))
        else:
            shape, place = dst
            cast_out_specs.append(pl.BlockSpec((1,) + c.shape[1:],
                                               lambda i, j, place=place: place(cast_block(i, j))))
            cast_out_shapes.append(jax.ShapeDtypeStruct(shape, BF16))

    def w_rows(i, j):
        return (pl.multiple_of(jnp.where(j < n_a, j * tn, j * tn + SSD_HEADS), SSD_HEADS), 0)

    sets = [(x, tm, lambda i: i)]
    if extra is not None:
        assert t == tm, "extra rows are recomputed per row tile"
        sets.append((extra, extra.shape[0], lambda i: 0))
    x_specs, out_specs, out_shape, scratch = [], [], [], []
    for xs_, rows, ri in sets:
        x_specs.append(pl.BlockSpec((rows, D_MODEL), lambda i, j, ri=ri: (ri(i), 0)))
        out_specs += [pl.BlockSpec((rows, tn), lambda i, j, ri=ri: (ri(i), j)),
                      pl.BlockSpec((rows, LANES), lambda i, j, ri=ri: (ri(i), 0))]
        out_shape += [jax.ShapeDtypeStruct((xs_.shape[0], PROJ_COLS), BF16),
                      jax.ShapeDtypeStruct((xs_.shape[0], LANES), F32)]
        scratch.append(pltpu.VMEM((rows, D_MODEL), BF16))
    res = pl.pallas_call(
        functools.partial(_inproj_kernel, len(sets), len(cast), cast_blocks),
        grid=(t // tm, nj),
        in_specs=x_specs + [
            pl.BlockSpec((1, D_MODEL), lambda i, j: (0, 0)),
            pl.BlockSpec((pl.Element(tn), pl.Element(D_MODEL)), w_rows),
            pl.BlockSpec((LANES, D_MODEL), lambda i, j: (0, 0)),
        ] + cast_specs,
        out_specs=out_specs + cast_out_specs,
        out_shape=out_shape + cast_out_shapes,
        scratch_shapes=scratch,
        compiler_params=pltpu.CompilerParams(
            dimension_semantics=("arbitrary", "arbitrary"), vmem_limit_bytes=VMEM_LIMIT),
        name="inproj",
    )(*[st[0] for st in sets], lnw, wt, wdt, *cast)
    n_proj = 2 * len(sets)
    projs = tuple(tuple(res[2 * k:2 * k + 2]) for k in range(len(sets)))
    projs = projs if extra is not None else projs[0]
    return (projs, tuple(res[n_proj:])) if cast else projs


def halo_rows(ns):
    return max(2 * SUBLANES, ns * SUBLANES)


def _shift_matrices(q, ns, taps):
    L = q // ns
    hb = halo_rows(ns)
    r = np.arange(q)[:, None]
    col = np.arange(3 * hb + q)[None, :]
    seq, t = r // L, r % L
    mats = []
    for k in range(taps - 1):
        d = taps - 1 - k
        from_cur = (col >= 3 * hb) & (col - 3 * hb == r - d) & (t >= d)
        from_hist = (col < 3 * hb) & (col % hb == seq * SUBLANES + SUBLANES - d + t) & (t < d)
        mats.append(from_cur | from_hist)
    return jnp.asarray(np.stack(mats), dtype=BF16)


def _seqmix_kernel(Q, NS, CPS, one_step, pad_rows,
                   a_blk, sb_blk, sc_blk, sh_blk, dt_blk, hist_ref, schist_ref, st0_ref, shift_ref,
                   cw_ref, cb_ref, scw_ref, dtb_ref, alog_ref, drow_ref, nw_ref,
                   u_blk, v_blk, st_ref, cs_ref, scs_ref,
                   halo, scpad, xbc_s, y_s, xst_s, *ring):
    @pl.when(pl.program_id(1) == 0)
    def _():
        halo[...] = jnp.zeros_like(halo)
        for s in range(NS):
            halo[(s + 1) * SUBLANES - (SSD_CONV - 1):(s + 1) * SUBLANES, :] = hist_ref[s]
            scpad[s, SUBLANES - (SC_CONV - 1):SUBLANES, :] = schist_ref[s]
        if not one_step:
            st_ref[...] = st0_ref[...]

    src_ref = st0_ref if one_step else st_ref
    if ring:
        ring_ref, ring_sem = ring
        b = pl.program_id(0)

        def fetch(blk, slot):
            return pltpu.make_async_copy(st0_ref.at[pl.ds(blk * NS, NS)], ring_ref.at[slot], ring_sem.at[slot])

        @pl.when(b == 0)
        def _():
            fetch(0, 0).start()
            fetch(1, 1).start()

        @pl.when(b + 2 < pl.num_programs(0))
        def _():
            fetch(b + 2, (b + 2) % 3).start()

        fetch(b, b % 3).wait()
        src_ref = ring_ref.at[b % 3]
    shared = (shift_ref, cw_ref, cb_ref, scw_ref, dtb_ref, alog_ref, drow_ref, nw_ref,
              src_ref, st_ref, cs_ref, scs_ref, halo, scpad, xbc_s, y_s, xst_s)
    row_blocks = (a_blk, sb_blk, sc_blk, sh_blk, dt_blk, u_blk, v_blk)
    if CPS == 1:
        _seqmix_chunk(Q, NS, pad_rows, *row_blocks, *shared)
    else:
        def body(k, carry):
            rows = pl.ds(pl.multiple_of(k * Q, Q), Q)
            _seqmix_chunk(Q, NS, pad_rows, *[r.at[rows] for r in row_blocks], *shared)
            return carry
        lax.fori_loop(0, CPS, body, 0)


def _seqmix_chunk(Q, NS, pad_rows, a_ref, sb_ref, sc_ref, sh_ref, dt_ref, u_ref, v_ref,
                  shift_ref, cw_ref, cb_ref, scw_ref, dtb_ref, alog_ref, drow_ref, nw_ref,
                  src_ref, st_ref, cs_ref, scs_ref, halo, scpad, xbc_s, y_s, xst_s):
    L = Q // NS
    lg = L.bit_length() - 1

    CW = 256
    hb = halo_rows(NS)
    for cc in range(0, SSD_CONV_DIM, CW):
        raw_b = a_ref[:, SSD_INNER + cc:SSD_INNER + cc + CW]
        hl = halo[:, cc:cc + CW]
        if hb > hl.shape[0]:
            hl = jnp.concatenate([hl, jnp.zeros((hb - hl.shape[0], CW), F32)], axis=0)
        h_hi = hl.astype(BF16)
        h_r = hl - h_hi.astype(F32)
        h_mid = h_r.astype(BF16)
        h_lo = (h_r - h_mid.astype(F32)).astype(BF16)
        ext = jnp.concatenate([h_hi, h_mid, h_lo, raw_b], axis=0)
        raw = raw_b.astype(F32)
        acc = raw * cw_ref[3:4, cc:cc + CW] + cb_ref[:, cc:cc + CW]
        for k in range(SSD_CONV - 1):
            acc = acc + jnp.dot(shift_ref[k], ext, preferred_element_type=F32) * cw_ref[k:k + 1, cc:cc + CW]
        xbc_s[:, cc:cc + CW] = acc * _sigmoid(acc)
        for s in range(NS):
            halo[s * SUBLANES:(s + 1) * SUBLANES, cc:cc + CW] = raw[(s + 1) * L - SUBLANES:(s + 1) * L]
    for s in range(NS):
        cs_ref[s] = halo[(s + 1) * SUBLANES - (SSD_CONV - 1):(s + 1) * SUBLANES, :]

    for cc in range(0, SC_WIDTH, CW):
        scb = sb_ref[:, cc:cc + CW].astype(F32)
        ch_all = sc_ref[:, cc:cc + CW].astype(F32) * sh_ref[:, cc:cc + CW].astype(F32)
        for s in range(NS):
            ch = ch_all[s * L:(s + 1) * L]
            scpad[s, SUBLANES:SUBLANES + L, cc:cc + CW] = ch
            acc = ch * scw_ref[2:3, cc:cc + CW]
            for k in range(SC_CONV - 1):
                acc = acc + scpad[s, 6 + k:6 + k + L, cc:cc + CW] * scw_ref[k:k + 1, cc:cc + CW]
            v_ref[s * L:(s + 1) * L, cc:cc + CW] = (scb[s * L:(s + 1) * L] * acc).astype(BF16)
    for s in range(NS):
        scs_ref[s] = scpad[s, L + 6:L + 8, :]
        scpad[s, 0:SUBLANES, :] = scpad[s, L:L + SUBLANES, :]

    def padrows(x):
        if Q == LANES:
            return x
        return jnp.concatenate([x, jnp.zeros((LANES - Q, x.shape[1]), x.dtype)], axis=0)

    li = lax.broadcasted_iota(jnp.int32, (Q, Q), 0)
    si = lax.broadcasted_iota(jnp.int32, (Q, Q), 1)
    same = (li >> lg) == (si >> lg)
    causal = jnp.logical_and(same, si <= li)
    tril01 = jnp.where(causal, 1.0, 0.0).astype(BF16)
    same01 = jnp.where(same, 1.0, 0.0).astype(BF16)

    dt = _softplus(dt_ref[...] + dtb_ref[...])
    if pad_rows:
        ri = lax.broadcasted_iota(jnp.int32, (Q, LANES), 0)
        dt = jnp.where(ri >= pad_rows, dt, 0.0)
    da = dt * (-jnp.exp(alog_ref[...]))
    acum = _dot01(tril01, da)
    tot = _dot01(same01, da)
    acum_t = padrows(acum).T
    tot_t = padrows(tot).T
    dt_t = padrows(dt).T
    w_t = jnp.exp(tot_t - acum_t) * dt_t
    acum2 = acum * LOG2E
    rowq_t = acum_t * LOG2E - jnp.log2(dt_t)

    left_head = lax.broadcasted_iota(jnp.int32, (Q, LANES), 1) < SSD_HEAD_DIM
    rowseq = lax.broadcasted_iota(jnp.int32, (Q, GROUP_COLS), 0) >> lg
    rowseq_p = lax.broadcasted_iota(jnp.int32, (LANES, LANES), 0) >> lg

    for g in range(SSD_GROUPS):
        b_g = xbc_s[:, SSD_INNER + g * SSD_STATE:SSD_INNER + (g + 1) * SSD_STATE]
        c_g = xbc_s[:, SSD_INNER + GROUP_COLS + g * SSD_STATE:SSD_INNER + GROUP_COLS + (g + 1) * SSD_STATE]
        b_gb = b_g.astype(BF16)
        c_gb = c_g.astype(BF16)
        cbm = _nt_dot(c_gb, b_gb)
        yo = None
        for s in range(NS):
            h_s = src_ref[s, g * GROUP_COLS:(g + 1) * GROUP_COLS, :].astype(BF16)
            yo_s = _nt_dot(c_gb, h_s)
            yo = yo_s if yo is None else jnp.where(rowseq == s, yo_s, yo)
        for rp in range(SSD_HPG // 2):
            h0 = g * SSD_HPG + 2 * rp
            cols = slice(h0 * SSD_HEAD_DIM, (h0 + 2) * SSD_HEAD_DIM)
            colbs, w_pair = [], []
            for h in (h0, h0 + 1):
                colb = jnp.broadcast_to(acum2[:, h:h + 1], (Q, LANES))
                rowb = jnp.broadcast_to(rowq_t[h:h + 1, 0:Q], (Q, Q))
                dec_dt = jnp.exp2(jnp.where(causal, colb[:, 0:Q] - rowb, NEG_BIG))
                w_pair.append((cbm * dec_dt).astype(BF16))
                colbs.append(colb)
            x_p = xbc_s[:, cols]
            rhs = jnp.concatenate([jnp.where(left_head, x_p, 0.0).astype(BF16),
                                   jnp.where(left_head, 0.0, x_p).astype(BF16)], axis=0)
            yd = jnp.dot(jnp.concatenate(w_pair, axis=1), rhs, preferred_element_type=F32)
            ecol = jnp.exp2(jnp.where(left_head, colbs[0], colbs[1]))
            y_s[:, cols] = yd + yo[:, rp * LANES:(rp + 1) * LANES] * ecol + drow_ref[:, cols] * x_p

    for jb in range(SSD_INNER // LANES):
        xst_s[jb * LANES:(jb + 1) * LANES, :] = padrows(xbc_s[:, jb * LANES:(jb + 1) * LANES]).T
    for s in range(NS):
        da_b = jnp.exp(jnp.broadcast_to(tot_t[:, s * L:s * L + 1], (LANES, LANES)))
        for g in range(SSD_GROUPS):
            b_p = padrows(xbc_s[:, SSD_INNER + g * SSD_STATE:SSD_INNER + (g + 1) * SSD_STATE])
            if NS > 1:
                b_p = jnp.where(rowseq_p == s, b_p, 0.0)
            pieces = []
            for r in range(SSD_HPG):
                h = g * SSD_HPG + r
                pieces.append(xst_s[h * SSD_HEAD_DIM:(h + 1) * SSD_HEAD_DIM, :] * w_t[h:h + 1, :])
            xw_t = jnp.concatenate(pieces, axis=0).astype(BF16)
            upd = jnp.dot(xw_t, b_p.astype(BF16), preferred_element_type=F32)
            for r in range(SSD_HPG):
                h = g * SSD_HPG + r
                rows = slice(h * SSD_HEAD_DIM, (h + 1) * SSD_HEAD_DIM)
                dec_h = jnp.broadcast_to(da_b[h:h + 1, :], (SSD_HEAD_DIM, SSD_STATE))
                st_ref[s, rows, :] = dec_h * src_ref[s, rows, :] + upd[r * SSD_HEAD_DIM:(r + 1) * SSD_HEAD_DIM, :]

    for g in range(SSD_GROUPS):
        cols = slice(g * GROUP_COLS, (g + 1) * GROUP_COLS)
        z = a_ref[:, cols].astype(F32)
        ug = y_s[:, cols] * (z * _sigmoid(z))
        ms = jnp.mean(ug * ug, axis=-1, keepdims=True)
        u_ref[:, cols] = (ug * lax.rsqrt(ms + EPS) * nw_ref[:, cols]).astype(BF16)


def _seqmix(proj, pdt, hist, schist, st0, prm, *, nb, nc, q, ns, pad_rows, init_bcast, cps=1):
    L = q // ns
    cw, cb, scw, dtb, alog, drow, nw = prm
    steps = nc // cps
    rows = q * cps
    sc_block0 = A_COLS // SC_WIDTH

    def tok(b, c):
        return (b * steps + c, 0)

    def init3(b, c):
        return (0 if init_bcast else b, 0, 0)

    def const2(b, c):
        return (0, 0)

    def per_b(b, c):
        return (b, 0, 0)

    n_seq = nb * ns
    shifts = _shift_matrices(q, ns, SSD_CONV)
    ring = steps == 1 and cps == 1 and nb > 2 and not init_bcast
    sc_specs = [pl.BlockSpec((rows, SC_WIDTH), lambda b, c, k=k: (b * steps + c, sc_block0 + k))
                for k in range(3)]
    return pl.pallas_call(
        functools.partial(_seqmix_kernel, q, ns, cps, steps == 1 and cps == 1, pad_rows),
        grid=(nb, steps),
        in_specs=[pl.BlockSpec((rows, A_COLS), tok)] + sc_specs + [
            pl.BlockSpec((rows, LANES), tok),
            pl.BlockSpec((ns, SSD_CONV - 1, SSD_CONV_DIM), init3),
            pl.BlockSpec((ns, SC_CONV - 1, SC_WIDTH), init3),
            (pl.BlockSpec(memory_space=pltpu.HBM) if ring
             else pl.BlockSpec((ns, SSD_INNER, SSD_STATE), init3)),
            pl.BlockSpec(shifts.shape, lambda b, c: (0, 0, 0)),
            pl.BlockSpec((SSD_CONV, SSD_CONV_DIM), const2),
            pl.BlockSpec((1, SSD_CONV_DIM), const2),
            pl.BlockSpec((SC_CONV, SC_WIDTH), const2),
            pl.BlockSpec((1, LANES), const2),
            pl.BlockSpec((1, LANES), const2),
            pl.BlockSpec((1, SSD_INNER), const2),
            pl.BlockSpec((1, SSD_INNER), const2),
        ],
        out_specs=[
            pl.BlockSpec((rows, SSD_INNER), tok),
            pl.BlockSpec((rows, SC_WIDTH), tok),
            pl.BlockSpec((ns, SSD_INNER, SSD_STATE), per_b),
            pl.BlockSpec((ns, SSD_CONV - 1, SSD_CONV_DIM), per_b),
            pl.BlockSpec((ns, SC_CONV - 1, SC_WIDTH), per_b),
        ],
        out_shape=[
            jax.ShapeDtypeStruct((nb * nc * q, SSD_INNER), BF16),
            jax.ShapeDtypeStruct((nb * nc * q, SC_WIDTH), BF16),
            jax.ShapeDtypeStruct((n_seq, SSD_INNER, SSD_STATE), F32),
            jax.ShapeDtypeStruct((n_seq, SSD_CONV - 1, SSD_CONV_DIM), F32),
            jax.ShapeDtypeStruct((n_seq, SC_CONV - 1, SC_WIDTH), F32),
        ],
        scratch_shapes=[
            pltpu.VMEM((ns * SUBLANES, SSD_CONV_DIM), F32),
            pltpu.VMEM((ns, SUBLANES + L, SC_WIDTH), F32),
            pltpu.VMEM((q, SSD_CONV_DIM), F32),
            pltpu.VMEM((q, SSD_INNER), F32),
            pltpu.VMEM((SSD_INNER, LANES), F32),
        ] + ([pltpu.VMEM((3, ns, SSD_INNER, SSD_STATE), F32), pltpu.SemaphoreType.DMA((3,))] if ring else []),
        compiler_params=pltpu.CompilerParams(
            dimension_semantics=("arbitrary", "arbitrary"), vmem_limit_bytes=VMEM_LIMIT),
        name="seqmix",
    )(proj, proj, proj, proj, pdt, hist, schist, st0, shifts, cw, cb, scw, dtb, alog, drow, nw)


def _merge_kernel(u_ref, v_ref, g_ref, x_ref, wa_ref, ws_ref, wo_ref, ln2_ref, wr_ref, br_ref,
                  xg_ref, cnt_s):
    i = pl.program_id(0)

    @pl.when(i == 0)
    def _():
        cnt_s[...] = jnp.zeros_like(cnt_s)

    y_ssd = jnp.dot(u_ref[...], wa_ref[...], preferred_element_type=F32)
    y_sc = jnp.dot(v_ref[...], ws_ref[...], preferred_element_type=F32)
    g1 = g_ref[:, 0:D_MODEL].astype(F32)
    g2 = g_ref[:, D_MODEL:2 * D_MODEL].astype(F32)
    merged = (_sigmoid(g1) * y_ssd + _sigmoid(g2) * y_sc).astype(BF16)
    x1 = x_ref[...] + jnp.dot(merged, wo_ref[...], preferred_element_type=F32)
    xg_ref[:, 0:D_MODEL] = x1
    ms = jnp.mean(x1 * x1, axis=-1, keepdims=True)
    h2 = (x1 * lax.rsqrt(ms + EPS) * ln2_ref[...]).astype(BF16)

    logits = jnp.dot(h2, wr_ref[...], preferred_element_type=F32) + br_ref[...]
    lane = lax.broadcasted_iota(jnp.int32, logits.shape, 1).astype(F32)
    big = float(LANES)
    gl = jnp.where(lane < N_EGROUPS, logits, NEG_BIG)
    gmax = jnp.max(gl, axis=-1, keepdims=True)
    g_sel = jnp.min(jnp.where(gl == gmax, lane, big), axis=-1, keepdims=True)
    gsum = jnp.sum(jnp.exp(gl - gmax), axis=-1, keepdims=True)
    g_prob = 1.0 / gsum
    lo = N_EGROUPS + EXPERTS_PER_GROUP * g_sel
    emask = jnp.logical_and(lane >= lo, lane < lo + EXPERTS_PER_GROUP)
    el = jnp.where(emask, logits, NEG_BIG)
    m1 = jnp.max(el, axis=-1, keepdims=True)
    e = jnp.where(emask, jnp.exp(el - m1), -1.0)
    i1 = jnp.min(jnp.where(e == 1.0, lane, big), axis=-1, keepdims=True)
    e_rest = jnp.where(lane == i1, -1.0, e)
    e2 = jnp.max(e_rest, axis=-1, keepdims=True)
    i2 = jnp.min(jnp.where(e_rest == e2, lane, big), axis=-1, keepdims=True)
    denom = 1.0 + e2
    w1 = g_prob / denom
    w2 = g_prob * e2 / denom
    gates = jnp.where(lane == i1, w1, 0.0) + jnp.where(lane == i2, w2, 0.0)

    tm = logits.shape[0]
    rb = min(tm, RANK_BLOCK)
    onehot = jnp.where(lane == g_sel, 1.0, 0.0).astype(BF16)
    ri = lax.broadcasted_iota(jnp.int32, (rb, rb), 0)
    ci = lax.broadcasted_iota(jnp.int32, (rb, rb), 1)
    tril01 = jnp.where(ci <= ri, 1.0, 0.0).astype(BF16)
    carry = cnt_s[...]
    incl_blocks = []
    for r0 in range(0, tm, rb):
        incl_b = jnp.dot(tril01, onehot[r0:r0 + rb], preferred_element_type=F32) + carry
        carry = incl_b[rb - 1:rb, :]
        incl_blocks.append(incl_b)
    cnt_s[...] = carry
    incl = incl_blocks[0] if len(incl_blocks) == 1 else jnp.concatenate(incl_blocks, axis=0)
    rank = jnp.sum(jnp.where(lane == g_sel, incl - 1.0, 0.0), axis=-1, keepdims=True)
    xg_ref[:, D_MODEL:D_MODEL + LANES] = jnp.where(
        lane == 0, g_sel.astype(F32), jnp.where(lane == 1, rank, gates))


def _merge(u, v, g, x, wa, ws, wo, ln2, wr, br, tm):
    t = x.shape[0]
    row = lambda i: (i, 0)
    const = lambda i: (0, 0)
    return pl.pallas_call(
        _merge_kernel,
        grid=(t // tm,),
        in_specs=[
            pl.BlockSpec((tm, SSD_INNER), row),
            pl.BlockSpec((tm, SC_WIDTH), row),
            pl.BlockSpec((tm, G_COLS), lambda i: (i, (A_COLS + S_COLS) // G_COLS)),
            pl.BlockSpec((tm, D_MODEL), row),
            pl.BlockSpec((SSD_INNER, D_MODEL), const, pipeline_mode=pl.Buffered(1)),
            pl.BlockSpec((SC_WIDTH, D_MODEL), const, pipeline_mode=pl.Buffered(1)),
            pl.BlockSpec((D_MODEL, D_MODEL), const, pipeline_mode=pl.Buffered(1)),
            pl.BlockSpec((1, D_MODEL), const),
            pl.BlockSpec((D_MODEL, LANES), const),
            pl.BlockSpec((1, LANES), const),
        ],
        out_specs=pl.BlockSpec((tm, XG_COLS), row),
        out_shape=jax.ShapeDtypeStruct((t, XG_COLS), F32),
        scratch_shapes=[pltpu.VMEM((1, LANES), F32)],
        compiler_params=pltpu.CompilerParams(
            dimension_semantics=("arbitrary",), vmem_limit_bytes=VMEM_LIMIT),
        name="merge",
    )(u, v, g, x, wa, ws, wo, ln2, wr, br)


def _moe_kernel(tmg, inv_ref, tg_ref, nv_ref, xg_hbm, wg_ref, wu_ref, wd_ref, ln2_ref, fw_ref,
                y_hbm, xbuf, ybuf, sem_in, sem_out):
    i = pl.program_id(0)
    n = pl.num_programs(0)
    slot = i % 2

    def row_in(tok, r, slt):
        return pltpu.make_async_copy(xg_hbm.at[pl.ds(tok, 1)], xbuf.at[slt, pl.ds(r, 1)], sem_in.at[slt])

    def row_out(tok, r, slt):
        return pltpu.make_async_copy(ybuf.at[slt, pl.ds(r, 1)], y_hbm.at[pl.ds(tok, 1)], sem_out.at[slt])

    unroll = 8

    def gather(tile, slt):
        def body(r, carry):
            row_in(inv_ref[tile * tmg + r], r, slt).start()
            return carry
        lax.fori_loop(0, tmg, body, 0, unroll=unroll)

    def wait_gather(slt):
        pltpu.make_async_copy(xg_hbm.at[pl.ds(0, tmg)], xbuf.at[slt], sem_in.at[slt]).wait()

    def scatter(tile, slt):
        def body(r, carry):
            row_out(inv_ref[tile * tmg + r], r, slt).start()
            return carry
        full = nv_ref[tile] == tmg

        @pl.when(full)
        def _():
            for r in range(tmg):
                body(r, 0)

        @pl.when(jnp.logical_not(full))
        def _():
            lax.fori_loop(0, nv_ref[tile], body, 0)

    def wait_scatter(tile, slt):
        def body(r, carry):
            row_out(0, r, slt).wait()
            return carry
        full = nv_ref[tile] == tmg

        @pl.when(full)
        def _():
            pltpu.make_async_copy(ybuf.at[slt], y_hbm.at[pl.ds(0, tmg)], sem_out.at[slt]).wait()

        @pl.when(jnp.logical_not(full))
        def _():
            lax.fori_loop(0, nv_ref[tile], body, 0)

    @pl.when(i == 0)
    def _():
        gather(0, 0)

    nxt = jnp.minimum(i + 1, n - 1)

    @pl.when(i >= 2)
    def _():
        wait_scatter(jnp.maximum(i - 2, 0), slot)

    @pl.when(nv_ref[i] == 0)
    def _():
        gather(nxt, 1 - slot)
        wait_gather(slot)

    @pl.when(nv_ref[i] > 0)
    def _():
        wait_gather(slot)
        x1 = xbuf[slot, :, 0:D_MODEL]
        for r in range(tmg // 2):
            row_in(inv_ref[nxt * tmg + r], r, 1 - slot).start()
        gate = xbuf[slot, :, D_MODEL:D_MODEL + LANES]
        for r in range(tmg // 2, tmg):
            row_in(inv_ref[nxt * tmg + r], r, 1 - slot).start()
        x1_res = xbuf[slot, :, 0:D_MODEL]
        ms = jnp.mean(x1 * x1, axis=-1, keepdims=True)
        h2 = (x1 * lax.rsqrt(ms + EPS) * ln2_ref[...]).astype(BF16)
        lane = lax.broadcasted_iota(jnp.int32, gate.shape, 1)
        first = N_EGROUPS + EXPERTS_PER_GROUP * tg_ref[i]
        a = jnp.dot(h2, wg_ref[...], preferred_element_type=F32)
        up = jnp.dot(h2, wu_ref[...], preferred_element_type=F32)
        act = a * _sigmoid(a) * up
        pieces = []
        for e in range(EXPERTS_PER_GROUP):
            ge = jnp.sum(jnp.where(lane == first + e, gate, 0.0), axis=-1, keepdims=True)
            pieces.append((act[:, e * D_EXPERT:(e + 1) * D_EXPERT] * ge).astype(BF16))
        mo = jnp.dot(jnp.concatenate(pieces, axis=1), wd_ref[...], preferred_element_type=F32)
        x2 = x1_res + mo
        ms2 = jnp.mean(x2 * x2, axis=-1, keepdims=True)
        ybuf[slot] = x2 * lax.rsqrt(ms2 + EPS) * fw_ref[...]
        scatter(i, slot)

    @pl.when(i == n - 1)
    def _():
        wait_gather(1 - slot)

        @pl.when(i >= 1)
        def _():
            wait_scatter(jnp.maximum(i - 1, 0), 1 - slot)
        wait_scatter(i, slot)


def _moe(xg, inv, tile_group, n_valid, wg, wu, wd, ln2, fw, tmg):
    t = xg.shape[0]
    n_tiles = tile_group.shape[0]
    gcols = EXPERTS_PER_GROUP * D_EXPERT
    const = lambda i, inv, tg, nv: (0, 0)
    by_group = lambda i, inv, tg, nv: (tg[i], 0, 0)
    return pl.pallas_call(
        functools.partial(_moe_kernel, tmg),
        grid_spec=pltpu.PrefetchScalarGridSpec(
            num_scalar_prefetch=3,
            grid=(n_tiles,),
            in_specs=[
                pl.BlockSpec(memory_space=pltpu.HBM),
                pl.BlockSpec((None, D_MODEL, gcols), by_group),
                pl.BlockSpec((None, D_MODEL, gcols), by_group),
                pl.BlockSpec((None, gcols, D_MODEL), by_group),
                pl.BlockSpec((1, D_MODEL), const),
                pl.BlockSpec((1, D_MODEL), const),
            ],
            out_specs=pl.BlockSpec(memory_space=pltpu.HBM),
            scratch_shapes=[
                pltpu.VMEM((2, tmg, XG_COLS), F32),
                pltpu.VMEM((2, tmg, D_MODEL), F32),
                pltpu.SemaphoreType.DMA((2,)),
                pltpu.SemaphoreType.DMA((2,)),
            ],
        ),
        out_shape=jax.ShapeDtypeStruct((t, D_MODEL), F32),
        compiler_params=pltpu.CompilerParams(
            dimension_semantics=("arbitrary",), vmem_limit_bytes=VMEM_LIMIT),
        name="moe",
    )(inv, tile_group, n_valid, xg, wg, wu, wd, ln2, fw)


def _invperm_kernel(pos_ref, lo_ref, hi_ref, inv_ref):
    def clear(i, carry):
        inv_ref[i] = 0
        return carry
    for k in range(lo_ref.shape[0]):
        lax.fori_loop(lo_ref[k], hi_ref[k], clear, 0)

    def place(t, carry):
        inv_ref[pos_ref[t]] = t
        return carry
    lax.fori_loop(0, pos_ref.shape[0], place, 0, unroll=8)


def _invperm(pos, pad_lo, pad_hi, n):
    smem = pl.BlockSpec(memory_space=pltpu.SMEM)
    return pl.pallas_call(
        _invperm_kernel,
        in_specs=[smem, smem, smem],
        out_specs=smem,
        out_shape=jax.ShapeDtypeStruct((n,), jnp.int32),
        name="invperm",
    )(pos, pad_lo, pad_hi)


def _route(xg, tmg):
    t = xg.shape[0]
    g = xg[:, D_MODEL].astype(jnp.int32)
    rank = xg[:, D_MODEL + 1].astype(jnp.int32)
    counts = jnp.sum(g[:, None] == jnp.arange(N_EGROUPS, dtype=jnp.int32)[None, :], axis=0, dtype=jnp.int32)
    tiles_per = (counts + tmg - 1) // tmg
    tile_end = jnp.cumsum(tiles_per)
    tile_base = tile_end - tiles_per
    n_tiles = t // tmg + N_EGROUPS
    pos = tile_base[g] * tmg + rank
    pad_lo = jnp.concatenate([tile_base * tmg + counts, tile_end[-1:] * tmg]).astype(jnp.int32)
    pad_hi = jnp.concatenate([tile_end * tmg, jnp.full((1,), n_tiles * tmg, jnp.int32)]).astype(jnp.int32)
    inv = _invperm(pos, pad_lo, pad_hi, n_tiles * tmg)
    ti = jnp.arange(n_tiles, dtype=jnp.int32)
    tg = jnp.minimum(jnp.sum(ti[:, None] >= tile_end[None, :], axis=1, dtype=jnp.int32), N_EGROUPS - 1)
    nv = jnp.clip(counts[tg] - (ti - tile_base[tg]) * tmg, 0, tmg)
    nv = jnp.where(ti < tile_end[-1], nv, 0).astype(jnp.int32)
    return inv, tg, nv


def _pad_lanes(v, n=LANES):
    v = v.reshape(1, -1).astype(F32)
    return jnp.pad(v, ((0, 0), (0, n - v.shape[1])))


def kernel(x_prompt, x_sample, state_ssm, state_ssd_conv, state_short_conv, meta_tokens, ln1_w, w_in,
           ssd_conv_w, ssd_conv_b, ssd_dt_bias, ssd_A_log, ssd_D, ssd_norm_w, w_ssd_out, sc_conv_w,
           w_sc_out, w_o, ln2_w, w_rg, b_rg, w_re, b_re, w_gate, w_up, w_down, final_norm_w):
    assert ln1_w.shape[0] == 1, "single-layer trunk"
    bp, sp, _ = x_prompt.shape
    bs, ss, _ = x_sample.shape
    l = 0

    wt = jnp.swapaxes(w_in[l], 0, 1)
    wdt = jnp.pad(wt[A_COLS:A_COLS + SSD_HEADS], ((0, LANES - SSD_HEADS), (0, 0))).astype(BF16)
    lnw = ln1_w[l].reshape(1, D_MODEL)
    prm = (ssd_conv_w[l], ssd_conv_b[l].reshape(1, -1), sc_conv_w[l],
           _pad_lanes(ssd_dt_bias[l]), _pad_lanes(ssd_A_log[l]),
           jnp.repeat(ssd_D[l].astype(F32), SSD_HEAD_DIM).reshape(1, SSD_INNER),
           ssd_norm_w[l].reshape(1, SSD_INNER))
    wa = w_ssd_out[l].astype(BF16)
    ws = w_sc_out[l].astype(BF16)
    wo = w_o[l].astype(BF16)
    ln2 = ln2_w[l].reshape(1, D_MODEL)
    wr = jnp.pad(jnp.concatenate([w_rg[l], w_re[l]], axis=1),
                 ((0, 0), (0, LANES - N_EGROUPS - N_EXPERTS))).astype(BF16)
    br = _pad_lanes(jnp.concatenate([b_rg[l], b_re[l]]))
    gcols = EXPERTS_PER_GROUP * D_EXPERT
    fw = final_norm_w.reshape(1, D_MODEL)
    halves = 2 * N_EXPERTS
    cast32 = (w_gate[l].reshape(halves, D_MODEL // 2, D_EXPERT),
              w_up[l].reshape(halves, D_MODEL // 2, D_EXPERT),
              w_down[l].reshape(halves, D_EXPERT // 2, D_MODEL))

    def mlp_tail(u, v, g, x, tmg):
        xg = _merge(u, v, g, x, wa, ws, wo, ln2, wr, br, tm=1024)
        inv, tg, nv = _route(xg, tmg)
        return _moe(xg, inv, tg, nv, wg, wu, wd, ln2, fw, tmg)

    q = LANES
    xm = jnp.pad(meta_tokens.astype(F32), ((q - N_META, 0), (0, 0)))
    xs = x_sample.reshape(bs * ss, D_MODEL)
    (sproj, sdt), (mproj, mdt) = _inproj(xs, lnw, wt, wdt, tm=bs * ss, extra=xm)
    zeros_hist = jnp.zeros((1, SSD_CONV - 1, SSD_CONV_DIM), F32)
    zeros_sch = jnp.zeros((1, SC_CONV - 1, SC_WIDTH), F32)
    zeros_st = jnp.zeros((1, SSD_INNER, SSD_STATE), F32)
    _, _, m_st, m_cs, m_scs = _seqmix(mproj, mdt, zeros_hist, zeros_sch, zeros_st, prm,
                                      nb=1, nc=1, q=q, ns=1, pad_rows=q - N_META, init_bcast=True)

    xp = x_prompt.reshape(bp * sp, D_MODEL)
    def by_group_cols(s):
        e = s // 2
        return (e // EXPERTS_PER_GROUP, s % 2, e % EXPERTS_PER_GROUP)

    wide = ((N_EGROUPS, D_MODEL, gcols), by_group_cols)
    (pproj, pdt), (wg, wu, wd) = _inproj(xp, lnw, wt, wdt, tm=2048, cast=cast32, cast_dst=(wide, wide))
    wd = wd.reshape(N_EGROUPS, gcols, D_MODEL)
    pu, pv, p_st, p_cs, p_scs = _seqmix(pproj, pdt, m_cs, m_scs, m_st, prm, nb=bp, nc=sp // q, q=q,
                                        ns=1, pad_rows=0, init_bcast=True, cps=4)
    y_prompt = mlp_tail(pu, pv, pproj, xp, MOE_TILE).reshape(bp, sp, D_MODEL)

    ns = 8
    su, sv, s_st, s_cs, s_scs = _seqmix(sproj, sdt, state_ssd_conv[l], state_short_conv[l],
                                        state_ssm[l].reshape(bs, SSD_INNER, SSD_STATE), prm,
                                        nb=bs // ns, nc=1, q=ns * ss, ns=ns, pad_rows=0, init_bcast=False)
    y_sample = mlp_tail(su, sv, sproj, xs, MOE_TILE_SMALL).reshape(bs, ss, D_MODEL)

    hshape = (SSD_HEADS, SSD_HEAD_DIM, SSD_STATE)
    return (y_prompt, y_sample,
            p_st.reshape(1, bp, *hshape), p_cs[None], p_scs[None],
            s_st.reshape(1, bs, *hshape), s_cs[None], s_scs[None])
```
